```python
import jax
import jax.numpy as jnp
from jax import lax
import numpy as np

D_MODEL = 1024
BATCH = 8
SEQ = 2048
DEPTH = 2
DEC_BATCH = 32
DEC_SEQ = 1
PAST_LEN = 8192
PAGE_SIZE = 128

N_BRANCH = 4
BRANCH_DIM = D_MODEL // 2
SC_DIM = BRANCH_DIM
SC_WIDTH = 3
GDN_HEADS = 4
GDN_DK = BRANCH_DIM // GDN_HEADS
GDN_DV = BRANCH_DIM // GDN_HEADS
GDN_QKV = GDN_HEADS * (2 * GDN_DK + GDN_DV)
GDN_CONV_WIDTH = 4
GDN_CHUNK = 64
CONF_DIM = BRANCH_DIM
CONF_WIDTH = 31
FOX_HEADS = 8
FOX_DH = BRANCH_DIM // FOX_HEADS
Q_BLOCK = 128
MEM_LEN = 256
X_HEADS = 4
X_DH = 128
D_FF = 2816
FFN_CONV_WIDTH = 3
IN_WIDTHS = (SC_DIM, SC_DIM, SC_DIM,
             GDN_HEADS * GDN_DK, GDN_HEADS * GDN_DK, GDN_HEADS * GDN_DV, GDN_HEADS * GDN_DV, GDN_HEADS, GDN_HEADS,
             CONF_DIM, CONF_DIM,
             FOX_HEADS * FOX_DH, FOX_HEADS * FOX_DH, FOX_HEADS * FOX_DH, FOX_HEADS,
             N_BRANCH * D_MODEL)
N_IN = sum(IN_WIDTHS)
NEG_INF = -1e30
F32 = jnp.float32

kernel_name = 'hybrid_gated_branch_decoder_step'


def rmsnorm(x, g, eps=1e-6):
    xf = x.astype(F32)
    y = xf * lax.rsqrt(jnp.mean(xf * xf, axis=-1, keepdims=True) + eps)
    return y.astype(x.dtype) * g


def layernorm(x, g, b, eps=1e-5):
    xf = x.astype(F32)
    mu = jnp.mean(xf, axis=-1, keepdims=True)
    var = jnp.mean(jnp.square(xf - mu), axis=-1, keepdims=True)
    return ((xf - mu) * lax.rsqrt(var + eps)).astype(x.dtype) * g + b


def l2norm(x, eps=1e-6):
    xf = x.astype(F32)
    return xf * lax.rsqrt(jnp.sum(xf * xf, axis=-1, keepdims=True) + eps)


def split_cols(z, widths):
    parts, start = [], 0
    for w in widths:
        parts.append(z[..., start:start + w])
        start += w
    return parts


def causal_dwconv(x, hist, w):
    width = w.shape[0]
    xc = jnp.concatenate([hist.astype(x.dtype), x], axis=1)
    y = lax.conv_general_dilated(xc, w[:, None, :].astype(x.dtype), window_strides=(1,), padding='VALID',
                                 dimension_numbers=('NWC', 'WIO', 'NWC'), feature_group_count=x.shape[-1])
    return y, xc[:, xc.shape[1] - (width - 1):]


def gated_delta_chunked(q, k, v, g, beta, s0):
    B_, T, H, DK = q.shape
    DV = v.shape[-1]
    C = min(GDN_CHUNK, T)
    n = -(-T // C)
    pad = n * C - T

    def chunks(a):
        a = jnp.pad(a, [(0, 0), (0, pad)] + [(0, 0)] * (a.ndim - 2))
        return jnp.moveaxis(a.reshape((B_, n, C) + a.shape[2:]), 3, 1)

    q = chunks(q) * (DK ** -0.5)
    k, v, g, beta = chunks(k), chunks(v), chunks(g), chunks(beta)
    d = jnp.cumsum(g, axis=-1)
    tril = jnp.tril(jnp.ones((C, C), dtype=bool))
    strict = jnp.tril(jnp.ones((C, C), dtype=bool), -1)
    diff = d[..., :, None] - d[..., None, :]
    L = jnp.where(tril, jnp.exp(jnp.where(tril, diff, 0.0)), 0.0)
    kb = k * beta[..., None]
    A = jnp.where(strict, jnp.einsum('bhnik,bhnjk->bhnij', kb, k) * L, 0.0) + jnp.eye(C, dtype=F32)
    rhs = jnp.concatenate([v * beta[..., None], kb * jnp.exp(d)[..., None]], axis=-1)
    sol = lax.linalg.triangular_solve(A, rhs, left_side=True, lower=True, unit_diagonal=True)
    u, w = sol[..., :DV], sol[..., DV:]
    qk = jnp.where(tril, jnp.einsum('bhnik,bhnjk->bhnij', q, k) * L, 0.0)
    q_dec = q * jnp.exp(d)[..., None]
    k_dec = k * jnp.exp(d[..., -1:] - d)[..., None]
    d_last = jnp.exp(d[..., -1])

    def step(S, xs):
        u_i, w_i, qk_i, qd_i, kd_i, dl_i = xs
        v_new = u_i - jnp.einsum('bhck,bhkv->bhcv', w_i, S)
        o = jnp.einsum('bhck,bhkv->bhcv', qd_i, S) + jnp.einsum('bhcs,bhsv->bhcv', qk_i, v_new)
        S = S * dl_i[..., None, None] + jnp.einsum('bhck,bhcv->bhkv', kd_i, v_new)
        return S, o

    xs = tuple(jnp.moveaxis(a, 2, 0) for a in (u, w, qk, q_dec, k_dec, d_last))
    S, o = lax.scan(step, s0, xs)
    o = o.transpose(1, 0, 3, 2, 4).reshape(B_, n * C, H, DV)[:, :T]
    return o, S


def forgetting_attention(q, k, v, c_q, c_k, q_pos, k_pos):
    B_, Tq, H, d = q.shape
    blk = Q_BLOCK if Tq % Q_BLOCK == 0 else Tq
    nb = Tq // blk
    ck = jnp.swapaxes(c_k, 1, 2)[:, :, None, :]
    qb = jnp.swapaxes(q.reshape(B_, nb, blk, H, d), 0, 1)
    cqb = jnp.swapaxes(c_q.reshape(B_, nb, blk, H), 0, 1)
    pb = q_pos.reshape(nb, blk)

    def block(args):
        qi, ci, pi = args
        s = jnp.einsum('bqhd,bkhd->bhqk', qi, k, preferred_element_type=F32) * (d ** -0.5)
        s = s + jnp.swapaxes(ci, 1, 2)[..., None] - ck
        s = jnp.where(k_pos[None, None, None, :] <= pi[None, None, :, None], s, NEG_INF)
        p = jax.nn.softmax(s, axis=-1)
        return jnp.einsum('bhqk,bkhd->bqhd', p.astype(v.dtype), v)

    o = lax.map(block, (qb, cqb, pb))
    return jnp.swapaxes(o, 0, 1).reshape(B_, Tq, H, d)


def token_mixer(xn, p, hist, past):
    B_, T, _ = xn.shape
    z = xn @ p['w_in']
    (a_h, a_b, a_c, g_q, g_k, g_v, g_z, g_a, g_b, c_v, c_g, f_q, f_k, f_v, f_f, gate) = split_cols(z, IN_WIDTHS)

    ya, sc_hist = causal_dwconv(a_c * a_h, hist['sconv'], p['sconv_w'])
    o_a = a_b * ya

    qkv, gdn_conv_hist = causal_dwconv(jnp.concatenate([g_q, g_k, g_v], axis=-1), hist['gdn_conv'], p['gdn_conv_w'])
    q, k, v = split_cols(jax.nn.silu(qkv), (GDN_HEADS * GDN_DK, GDN_HEADS * GDN_DK, GDN_HEADS * GDN_DV))
    q = l2norm(q.reshape(B_, T, GDN_HEADS, GDN_DK))
    k = l2norm(k.reshape(B_, T, GDN_HEADS, GDN_DK))
    v = v.reshape(B_, T, GDN_HEADS, GDN_DV).astype(F32)
    beta = jax.nn.sigmoid(g_b.astype(F32))
    g_log = -jnp.exp(p['gdn_a_log'].astype(F32)) * jax.nn.softplus(g_a.astype(F32) + p['gdn_dt_bias'].astype(F32))
    o, S = gated_delta_chunked(q, k, v, g_log, beta, hist['gdn'].astype(F32))
    o = rmsnorm(o, p['gdn_norm']) * jax.nn.silu(g_z.reshape(B_, T, GDN_HEADS, GDN_DV).astype(F32))
    o_b = o.astype(xn.dtype).reshape(B_, T, GDN_HEADS * GDN_DV)

    glu = c_v * jax.nn.sigmoid(c_g)
    yc, conf_hist = causal_dwconv(glu, hist['conf'], p['conf_dw_w'])
    o_c = jax.nn.silu(layernorm(yc + p['conf_dw_b'], p['conf_ln_g'], p['conf_ln_b']))

    fq = f_q.reshape(B_, T, FOX_HEADS, FOX_DH)
    fk = f_k.reshape(B_, T, FOX_HEADS, FOX_DH)
    fv = f_v.reshape(B_, T, FOX_HEADS, FOX_DH)
    flog = jax.nn.log_sigmoid((f_f + p['fox_fbias']).astype(F32))
    if past is None:
        kk, vv, ll = fk, fv, flog
        q_pos = jnp.arange(T, dtype=jnp.int32)
        k_pos = q_pos
    else:
        kp, vp, lp = past
        P = kp.shape[1]
        kk = jnp.concatenate([kp.astype(fk.dtype), fk], axis=1)
        vv = jnp.concatenate([vp.astype(fv.dtype), fv], axis=1)
        ll = jnp.concatenate([lp.astype(F32), flog], axis=1)
        q_pos = P + jnp.arange(T, dtype=jnp.int32)
        k_pos = jnp.arange(P + T, dtype=jnp.int32)
    c = jnp.cumsum(ll, axis=1)
    o_d = forgetting_attention(fq, kk, vv, c[:, -T:], c, q_pos, k_pos).reshape(B_, T, FOX_HEADS * FOX_DH)

    gates = jax.nn.sigmoid(gate).reshape(B_, T, N_BRANCH, D_MODEL)
    merged = 0.0
    for i, o_i in enumerate((o_a, o_b, o_c, o_d)):
        merged = merged + gates[:, :, i] * (o_i @ p['w_branch'][i])
    out = merged @ p['w_out']
    state = dict(sconv=sc_hist, gdn_conv=gdn_conv_hist, gdn=S, conf=conf_hist, fox_k=fk, fox_v=fv, fox_logf=flog)
    return out, state


def cross_attention(xn, mk, mv, w_cq, w_co):
    B_, T, _ = xn.shape
    q = (xn @ w_cq).reshape(B_, T, X_HEADS, X_DH)
    s = jnp.einsum('bqhd,bkhd->bhqk', q, mk, preferred_element_type=F32) * (X_DH ** -0.5)
    pr = jax.nn.softmax(s, axis=-1).astype(mv.dtype)
    o = jnp.einsum('bhqk,bkhd->bqhd', pr, mv)
    return o.reshape(B_, T, X_HEADS * X_DH) @ w_co


def conv_ffn(xn, p, hist):
    gu = xn @ p['w_ffn_in']
    g, u = gu[..., :D_FF], gu[..., D_FF:]
    gc, new_hist = causal_dwconv(g, hist, p['ffn_conv_w'])
    return (jax.nn.silu(gc) * u) @ p['w_ffn_out'], new_hist


def trunk(x, params, hists, mems, pasts, norm_final):
    states = []
    for l in range(DEPTH):
        p = params[l]
        mix, st = token_mixer(rmsnorm(x, p['norm_mix']), p, hists[l], pasts[l])
        x = x + mix
        x = x + cross_attention(rmsnorm(x, p['norm_cross']), mems[l][0], mems[l][1], p['w_cq'], p['w_co'])
        ff, st['ffn'] = conv_ffn(rmsnorm(x, p['norm_ffn']), p, hists[l]['ffn'])
        x = x + ff
        states.append(st)
    return rmsnorm(x, norm_final), states


def stack_state(states, name, axis):
    return jnp.stack([s[name] for s in states], axis=axis)


def setup_inputs(seed: int = 0) -> dict:
    key = jax.random.key(seed)
    ks = iter(jax.random.split(key, 64))

    def nrm(shape, scale=1.0):
        return jax.random.normal(next(ks), shape, F32) * scale

    def gain(shape):
        return 1.0 + nrm(shape, 0.02)

    n_pages = PAST_LEN // PAGE_SIZE
    n_used = DEC_BATCH * n_pages
    n_pool = n_used + n_used // 4
    inp = {}
    inp['x_prompt'] = nrm((BATCH, SEQ, D_MODEL))
    inp['x_sample'] = nrm((DEC_BATCH, DEC_SEQ, D_MODEL))
    inp['cache_fox_k'] = nrm((n_pool, PAGE_SIZE, DEPTH, FOX_HEADS, FOX_DH))
    inp['cache_fox_v'] = nrm((n_pool, PAGE_SIZE, DEPTH, FOX_HEADS, FOX_DH))
    inp['cache_fox_logf'] = jax.nn.log_sigmoid(nrm((n_pool, PAGE_SIZE, DEPTH, FOX_HEADS)) + 3.5)
    inp['cache_mem_k'] = nrm((DEC_BATCH, DEPTH, MEM_LEN, X_HEADS, X_DH))
    inp['cache_mem_v'] = nrm((DEC_BATCH, DEPTH, MEM_LEN, X_HEADS, X_DH))
    inp['state_sconv'] = nrm((DEC_BATCH, DEPTH, SC_WIDTH - 1, SC_DIM))
    inp['state_gdn_conv'] = nrm((DEC_BATCH, DEPTH, GDN_CONV_WIDTH - 1, GDN_QKV))
    inp['state_gdn'] = nrm((DEC_BATCH, DEPTH, GDN_HEADS, GDN_DK, GDN_DV), 0.1)
    inp['state_conf_conv'] = nrm((DEC_BATCH, DEPTH, CONF_WIDTH - 1, CONF_DIM))
    inp['state_ffn_conv'] = nrm((DEC_BATCH, DEPTH, FFN_CONV_WIDTH - 1, D_FF))
    inp['page_table'] = jax.random.permutation(next(ks), n_pool)[:n_used].reshape(DEC_BATCH, n_pages).astype(jnp.int32)
    inp['mem_prompt'] = nrm((BATCH, MEM_LEN, D_MODEL))
    inp['norm_mix'] = gain((DEPTH, D_MODEL))
    inp['w_in'] = nrm((DEPTH, D_MODEL, N_IN), D_MODEL ** -0.5)
    inp['w_branch'] = nrm((DEPTH, N_BRANCH, BRANCH_DIM, D_MODEL), BRANCH_DIM ** -0.5)
    inp['w_out'] = nrm((DEPTH, D_MODEL, D_MODEL), D_MODEL ** -0.5)
    inp['sconv_w'] = nrm((DEPTH, SC_WIDTH, SC_DIM), SC_WIDTH ** -0.5)
    inp['gdn_conv_w'] = nrm((DEPTH, GDN_CONV_WIDTH, GDN_QKV), GDN_CONV_WIDTH ** -0.5)
    inp['gdn_a_log'] = jnp.log(jax.random.uniform(next(ks), (DEPTH, GDN_HEADS), F32, minval=1.0, maxval=16.0))
    dt = jnp.exp(jax.random.uniform(next(ks), (DEPTH, GDN_HEADS), F32, minval=-6.9, maxval=-2.3))
    inp['gdn_dt_bias'] = dt + jnp.log(-jnp.expm1(-dt))
    inp['gdn_norm'] = gain((DEPTH, GDN_DV))
    inp['conf_dw_w'] = nrm((DEPTH, CONF_WIDTH, CONF_DIM), CONF_WIDTH ** -0.5)
    inp['conf_dw_b'] = nrm((DEPTH, CONF_DIM), 0.02)
    inp['conf_ln_g'] = gain((DEPTH, CONF_DIM))
    inp['conf_ln_b'] = nrm((DEPTH, CONF_DIM), 0.02)
    inp['fox_fbias'] = jax.random.uniform(next(ks), (DEPTH, FOX_HEADS), F32, minval=1.0, maxval=6.0)
    inp['norm_cross'] = gain((DEPTH, D_MODEL))
    inp['norm_mem'] = gain((DEPTH, D_MODEL))
    inp['w_cq'] = nrm((DEPTH, D_MODEL, X_HEADS * X_DH), D_MODEL ** -0.5)
    inp['w_ckv'] = nrm((DEPTH, D_MODEL, 2 * X_HEADS * X_DH), D_MODEL ** -0.5)
    inp['w_co'] = nrm((DEPTH, X_HEADS * X_DH, D_MODEL), (X_HEADS * X_DH) ** -0.5)
    inp['norm_ffn'] = gain((DEPTH, D_MODEL))
    inp['w_ffn_in'] = nrm((DEPTH, D_MODEL, 2 * D_FF), D_MODEL ** -0.5)
    inp['ffn_conv_w'] = nrm((DEPTH, FFN_CONV_WIDTH, D_FF), FFN_CONV_WIDTH ** -0.5)
    inp['w_ffn_out'] = nrm((DEPTH, D_FF, D_MODEL), D_FF ** -0.5)
    inp['norm_final'] = gain((D_MODEL,))
    return inp


def reference(x_prompt, x_sample, cache_fox_k, cache_fox_v, cache_fox_logf, cache_mem_k, cache_mem_v,
              state_sconv, state_gdn_conv, state_gdn, state_conf_conv, state_ffn_conv, page_table, mem_prompt,
              norm_mix, w_in, w_branch, w_out, sconv_w, gdn_conv_w, gdn_a_log, gdn_dt_bias, gdn_norm,
              conf_dw_w, conf_dw_b, conf_ln_g, conf_ln_b, fox_fbias, norm_cross, norm_mem, w_cq, w_ckv, w_co,
              norm_ffn, w_ffn_in, ffn_conv_w, w_ffn_out, norm_final):
    params = [dict(norm_mix=norm_mix[l], w_in=w_in[l], w_branch=w_branch[l], w_out=w_out[l], sconv_w=sconv_w[l],
                   gdn_conv_w=gdn_conv_w[l], gdn_a_log=gdn_a_log[l], gdn_dt_bias=gdn_dt_bias[l], gdn_norm=gdn_norm[l],
                   conf_dw_w=conf_dw_w[l], conf_dw_b=conf_dw_b[l], conf_ln_g=conf_ln_g[l], conf_ln_b=conf_ln_b[l],
                   fox_fbias=fox_fbias[l], norm_cross=norm_cross[l], w_cq=w_cq[l], w_co=w_co[l],
                   norm_ffn=norm_ffn[l], w_ffn_in=w_ffn_in[l], ffn_conv_w=ffn_conv_w[l], w_ffn_out=w_ffn_out[l])
              for l in range(DEPTH)]

    b_p = x_prompt.shape[0]
    dt = x_prompt.dtype
    xw = X_HEADS * X_DH
    mems_p = []
    for l in range(DEPTH):
        kv = rmsnorm(mem_prompt, norm_mem[l]) @ w_ckv[l]
        mems_p.append((kv[..., :xw].reshape(b_p, MEM_LEN, X_HEADS, X_DH),
                       kv[..., xw:].reshape(b_p, MEM_LEN, X_HEADS, X_DH)))
    zero_hist = dict(sconv=jnp.zeros((b_p, SC_WIDTH - 1, SC_DIM), dt),
                     gdn_conv=jnp.zeros((b_p, GDN_CONV_WIDTH - 1, GDN_QKV), dt),
                     gdn=jnp.zeros((b_p, GDN_HEADS, GDN_DK, GDN_DV), F32),
                     conf=jnp.zeros((b_p, CONF_WIDTH - 1, CONF_DIM), dt),
                     ffn=jnp.zeros((b_p, FFN_CONV_WIDTH - 1, D_FF), dt))
    y_prompt, st_p = trunk(x_prompt, params, [zero_hist] * DEPTH, mems_p, [None] * DEPTH, norm_final)

    b_s = x_sample.shape[0]
    hists_s = [dict(sconv=state_sconv[:, l], gdn_conv=state_gdn_conv[:, l], gdn=state_gdn[:, l],
                    conf=state_conf_conv[:, l], ffn=state_ffn_conv[:, l]) for l in range(DEPTH)]
    mems_s = [(cache_mem_k[:, l], cache_mem_v[:, l]) for l in range(DEPTH)]
    pasts_s = [(cache_fox_k[page_table, :, l].reshape(b_s, -1, FOX_HEADS, FOX_DH),
                cache_fox_v[page_table, :, l].reshape(b_s, -1, FOX_HEADS, FOX_DH),
                cache_fox_logf[page_table, :, l].reshape(b_s, -1, FOX_HEADS)) for l in range(DEPTH)]
    y_sample, st_s = trunk(x_sample, params, hists_s, mems_s, pasts_s, norm_final)

    return (y_prompt, y_sample,
            stack_state(st_p, 'fox_k', 2), stack_state(st_p, 'fox_v', 2), stack_state(st_p, 'fox_logf', 2),
            jnp.stack([m[0] for m in mems_p], axis=1), jnp.stack([m[1] for m in mems_p], axis=1),
            stack_state(st_p, 'sconv', 1), stack_state(st_p, 'gdn_conv', 1), stack_state(st_p, 'gdn', 1),
            stack_state(st_p, 'conf', 1), stack_state(st_p, 'ffn', 1),
            stack_state(st_s, 'fox_k', 2), stack_state(st_s, 'fox_v', 2), stack_state(st_s, 'fox_logf', 2),
            stack_state(st_s, 'sconv', 1), stack_state(st_s, 'gdn_conv', 1), stack_state(st_s, 'gdn', 1),
            stack_state(st_s, 'conf', 1), stack_state(st_s, 'ffn', 1))
```

```python
import functools

import jax
import jax.numpy as jnp
from jax import lax
from jax.experimental import pallas as pl
from jax.experimental.pallas import tpu as pltpu

F32 = jnp.float32
BF16 = jnp.bfloat16
HIGHEST = lax.Precision.HIGHEST

D_MODEL = 1024
BRANCH = 512
GDN_HEADS = 4
GDN_DK = 128
FOX_HEADS = 8
FOX_DH = 64
X_HEADS = 4
X_DH = 128
D_FF = 2816
MEM_LEN = 256
PAGE = 128
SC_W, GDN_W, CONF_W, FFN_W = 3, 4, 31, 3
NEG_INF = -1e30

C_AH, C_AB, C_AC = 0, 512, 1024
C_GQKV = 1536
C_GZ = 3072
C_CV, C_CG = 3584, 4096
C_FQ, C_FK, C_FV = 4608, 5120, 5632
C_GATE = 6144
NZ = 10240
SMALL = 128
LANES = 128
GDN_CHUNK = 128
PAGES_PER_STEP = 8


def _cparams(n_axes, vmem_mb=48):
    return pltpu.CompilerParams(dimension_semantics=("arbitrary",) * n_axes,
                                vmem_limit_bytes=vmem_mb * 1024 * 1024)


def _dot(a, b, precision=None):
    return jnp.dot(a, b, preferred_element_type=F32, precision=precision)


def _dot_nt(a, b, precision=None):
    return lax.dot_general(a, b, (((1,), (1,)), ((), ())), preferred_element_type=F32, precision=precision)


def _rms(x, g, eps=1e-6):
    return (x * lax.rsqrt(jnp.mean(x * x, axis=-1, keepdims=True) + eps)) * g


def _softplus(t):
    return jnp.maximum(t, 0.0) + jnp.log1p(jnp.exp(-jnp.abs(t)))


def _silu(t):
    return t * jax.nn.sigmoid(t)


def _rms_mm_kernel(*refs, has_small):
    if has_small:
        x_ref, g_ref, w_ref, ws_ref, o_ref, os_ref, xn_ref = refs
    else:
        x_ref, g_ref, w_ref, o_ref, xn_ref = refs

    @pl.when(pl.program_id(1) == 0)
    def _():
        xn_ref[...] = _rms(x_ref[...], g_ref[...]).astype(BF16)
        if has_small:
            os_ref[...] = _dot(xn_ref[...], ws_ref[...])

    o_ref[...] = _dot(xn_ref[...], w_ref[...])


def _rms_matmul(x, g, w, w_small, tm, tn):
    m, k = x.shape
    n = w.shape[1]
    has_small = w_small is not None
    in_specs = [pl.BlockSpec((tm, k), lambda i, j: (i, 0)),
                pl.BlockSpec((1, k), lambda i, j: (0, 0)),
                pl.BlockSpec((k, tn), lambda i, j: (0, j))]
    out_specs = [pl.BlockSpec((tm, tn), lambda i, j: (i, j))]
    out_shape = [jax.ShapeDtypeStruct((m, n), F32)]
    args = [x, g.reshape(1, k), w]
    if has_small:
        in_specs.append(pl.BlockSpec((k, SMALL), lambda i, j: (0, 0)))
        out_specs.append(pl.BlockSpec((tm, SMALL), lambda i, j: (i, 0)))
        out_shape.append(jax.ShapeDtypeStruct((m, SMALL), F32))
        args.append(w_small)
    res = pl.pallas_call(
        functools.partial(_rms_mm_kernel, has_small=has_small),
        grid=(m // tm, n // tn), in_specs=in_specs, out_specs=out_specs, out_shape=out_shape,
        scratch_shapes=[pltpu.VMEM((tm, k), BF16)], compiler_params=_cparams(2), name="rms_matmul")(*args)
    return res if has_small else res[0]


def _halo_rows(width):
    return 8 * ((width - 1 + 7) // 8)


def _conv_load_tile(x, hist_ref, ext_ref, tt, width):
    hp = _halo_rows(width)
    t = pl.program_id(1)

    @pl.when(t == 0)
    def _():
        ext_ref[0:hp, :] = jnp.zeros((hp, ext_ref.shape[1]), F32)
        ext_ref[hp - (width - 1):hp, :] = hist_ref[...]

    @pl.when(t > 0)
    def _():
        ext_ref[0:hp, :] = ext_ref[tt:tt + hp, :]

    ext_ref[hp:hp + tt, :] = x


def _conv_rows(ext_ref, w_ref, r0, nrows, width, c0=0, ncols=None):
    hp = _halo_rows(width)
    ncols = ext_ref.shape[1] if ncols is None else ncols
    acc = None
    for i in range(width):
        off = hp - (width - 1) + i + r0
        term = w_ref[i:i + 1, c0:c0 + ncols] * ext_ref[off:off + nrows, c0:c0 + ncols]
        acc = term if acc is None else acc + term
    return acc


def _conv_state(ext_ref, st_ref, tt, width):
    hp = _halo_rows(width)
    st_ref[...] = ext_ref[hp + tt - (width - 1):hp + tt, :]


def _sconv_kernel(h_ref, b_ref, c_ref, hist_ref, w_ref, o_ref, st_ref, ext_ref, *, tt):
    _conv_load_tile(c_ref[...] * h_ref[...], hist_ref, ext_ref, tt, SC_W)
    rb = min(tt, 128)
    for r0 in range(0, tt, rb):
        o_ref[r0:r0 + rb, :] = b_ref[r0:r0 + rb, :] * _conv_rows(ext_ref, w_ref, r0, rb, SC_W)
    _conv_state(ext_ref, st_ref, tt, SC_W)


def _sconv(z3, hist, w, tt):
    b, t, _ = z3.shape
    col = lambda c: pl.BlockSpec((None, tt, BRANCH), lambda i, j, c=c: (i, j, c // BRANCH))
    return pl.pallas_call(
        functools.partial(_sconv_kernel, tt=tt), grid=(b, t // tt),
        in_specs=[col(C_AH), col(C_AB), col(C_AC),
                  pl.BlockSpec((None, SC_W - 1, BRANCH), lambda i, j: (i, 0, 0)),
                  pl.BlockSpec((SC_W, BRANCH), lambda i, j: (0, 0))],
        out_specs=[pl.BlockSpec((None, tt, BRANCH), lambda i, j: (i, j, 0)),
                   pl.BlockSpec((None, SC_W - 1, BRANCH), lambda i, j: (i, 0, 0))],
        out_shape=[jax.ShapeDtypeStruct((b, t, BRANCH), F32),
                   jax.ShapeDtypeStruct((b, SC_W - 1, BRANCH), F32)],
        scratch_shapes=[pltpu.VMEM((_halo_rows(SC_W) + tt, BRANCH), F32)],
        compiler_params=_cparams(2), name="sconv")(z3, z3, z3, hist, w)


def _conf_kernel(v_ref, g_ref, hist_ref, w_ref, cb_ref, lg_ref, lb_ref, o_ref, st_ref, ext_ref, *, tt):
    _conv_load_tile(v_ref[...] * jax.nn.sigmoid(g_ref[...]), hist_ref, ext_ref, tt, CONF_W)
    rb = min(tt, 64)
    for r0 in range(0, tt, rb):
        y = _conv_rows(ext_ref, w_ref, r0, rb, CONF_W) + cb_ref[...]
        mu = jnp.mean(y, axis=-1, keepdims=True)
        yc = y - mu
        var = jnp.mean(yc * yc, axis=-1, keepdims=True)
        o_ref[r0:r0 + rb, :] = _silu((yc * lax.rsqrt(var + 1e-5)) * lg_ref[...] + lb_ref[...])
    _conv_state(ext_ref, st_ref, tt, CONF_W)


def _conformer(z3, hist, w, cb, lg, lb, tt):
    b, t, _ = z3.shape
    col = lambda c: pl.BlockSpec((None, tt, BRANCH), lambda i, j, c=c: (i, j, c // BRANCH))
    row = pl.BlockSpec((1, BRANCH), lambda i, j: (0, 0))
    return pl.pallas_call(
        functools.partial(_conf_kernel, tt=tt), grid=(b, t // tt),
        in_specs=[col(C_CV), col(C_CG),
                  pl.BlockSpec((None, CONF_W - 1, BRANCH), lambda i, j: (i, 0, 0)),
                  pl.BlockSpec((CONF_W, BRANCH), lambda i, j: (0, 0)), row, row, row],
        out_specs=[pl.BlockSpec((None, tt, BRANCH), lambda i, j: (i, j, 0)),
                   pl.BlockSpec((None, CONF_W - 1, BRANCH), lambda i, j: (i, 0, 0))],
        out_shape=[jax.ShapeDtypeStruct((b, t, BRANCH), F32),
                   jax.ShapeDtypeStruct((b, CONF_W - 1, BRANCH), F32)],
        scratch_shapes=[pltpu.VMEM((_halo_rows(CONF_W) + tt, BRANCH), F32)],
        compiler_params=_cparams(2), name="conformer")(
            z3, z3, hist, w, cb.reshape(1, BRANCH), lg.reshape(1, BRANCH), lb.reshape(1, BRANCH))


def _gdn_pre_kernel(x_ref, hist_ref, w_ref, q_ref, k_ref, v_ref, st_ref, ext_ref, *, tt):
    _conv_load_tile(x_ref[...], hist_ref, ext_ref, tt, GDN_W)
    rb = min(tt, 128)
    for r0 in range(0, tt, rb):
        for part, dst in enumerate((q_ref, k_ref, v_ref)):
            y = _silu(_conv_rows(ext_ref, w_ref, r0, rb, GDN_W, c0=part * BRANCH, ncols=BRANCH))
            if part == 2:
                dst[r0:r0 + rb, :] = y
                continue
            scale = GDN_DK ** -0.5 if part == 0 else 1.0
            for h in range(GDN_HEADS):
                blk = y[:, h * GDN_DK:(h + 1) * GDN_DK]
                nrm = lax.rsqrt(jnp.sum(blk * blk, axis=-1, keepdims=True) + 1e-6)
                dst[r0:r0 + rb, h * GDN_DK:(h + 1) * GDN_DK] = blk * nrm * scale
    _conv_state(ext_ref, st_ref, tt, GDN_W)


def _gdn_pre(z3, hist, w, tt):
    b, t, _ = z3.shape
    wq = 3 * BRANCH
    out = pl.BlockSpec((None, tt, BRANCH), lambda i, j: (i, j, 0))
    return pl.pallas_call(
        functools.partial(_gdn_pre_kernel, tt=tt), grid=(b, t // tt),
        in_specs=[pl.BlockSpec((None, tt, wq), lambda i, j: (i, j, C_GQKV // wq)),
                  pl.BlockSpec((None, GDN_W - 1, wq), lambda i, j: (i, 0, 0)),
                  pl.BlockSpec((GDN_W, wq), lambda i, j: (0, 0))],
        out_specs=[out, out, out, pl.BlockSpec((None, GDN_W - 1, wq), lambda i, j: (i, 0, 0))],
        out_shape=[jax.ShapeDtypeStruct((b, t, BRANCH), F32)] * 3 + [jax.ShapeDtypeStruct((b, GDN_W - 1, wq), F32)],
        scratch_shapes=[pltpu.VMEM((_halo_rows(GDN_W) + tt, wq), F32)],
        compiler_params=_cparams(2), name="gdn_pre")(z3, hist, w)


def _gates_kernel(zs_ref, bias_ref, alog_ref, p_ref, *rest, tt, with_t):
    if with_t:
        pt_ref, carry_ref = rest
    else:
        (carry_ref,) = rest
    zs = zs_ref[...]
    lane = lax.broadcasted_iota(jnp.int32, zs.shape, 1)
    tb = zs + bias_ref[...]
    g_log = -jnp.exp(alog_ref[...]) * _softplus(tb)
    beta = jax.nn.sigmoid(zs)
    flog = -_softplus(-tb)
    is_f = (lane >= 8) & (lane < 16)
    f_only = jnp.where(is_f, flog, 0.0)
    if tt > 1:
        r = lax.broadcasted_iota(jnp.int32, (tt, tt), 0)
        c = lax.broadcasted_iota(jnp.int32, (tt, tt), 1)
        csum = _dot(jnp.where(r >= c, 1.0, 0.0).astype(F32), f_only, precision=HIGHEST)
    else:
        csum = f_only

    @pl.when(pl.program_id(1) == 0)
    def _():
        carry_ref[...] = jnp.zeros_like(carry_ref)

    csum = csum + carry_ref[...]
    carry_ref[...] = csum[tt - 1:tt, :]
    p = jnp.where(lane < 4, g_log, jnp.where(lane < 8, beta, f_only)) + pltpu.roll(csum, 8, axis=1)
    p_ref[...] = p
    if with_t:
        pt_ref[...] = p.T


def _gates(zs3, bias_row, alog_row, tt):
    b, t, _ = zs3.shape
    with_t = tt % LANES == 0
    row = pl.BlockSpec((1, SMALL), lambda i, j: (0, 0))
    out_specs = [pl.BlockSpec((None, tt, SMALL), lambda i, j: (i, j, 0))]
    out_shape = [jax.ShapeDtypeStruct((b, t, SMALL), F32)]
    if with_t:
        out_specs.append(pl.BlockSpec((None, SMALL, tt), lambda i, j: (i, 0, j)))
        out_shape.append(jax.ShapeDtypeStruct((b, SMALL, t), F32))
    res = pl.pallas_call(
        functools.partial(_gates_kernel, tt=tt, with_t=with_t), grid=(b, t // tt),
        in_specs=[pl.BlockSpec((None, tt, SMALL), lambda i, j: (i, j, 0)), row, row],
        out_specs=out_specs, out_shape=out_shape,
        scratch_shapes=[pltpu.VMEM((1, SMALL), F32)],
        compiler_params=_cparams(2), name="gates")(zs3, bias_row, alog_row)
    return res if with_t else (res[0], None)


def _unit_lower_inverse(n, c):
    r = lax.broadcasted_iota(jnp.int32, (c, c), 0)
    q = lax.broadcasted_iota(jnp.int32, (c, c), 1)
    mm = lambda a, b: _dot(a.astype(BF16), b.astype(BF16))
    n0 = jnp.where((r >> 3) == (q >> 3), n, 0.0)
    x = jnp.where(r == q, 1.0, 0.0) - n0
    m = mm(n0, n0)
    x = x + mm(x, m)
    m = mm(m, m)
    x = x + mm(x, m)
    s = 3
    while (1 << s) < c:
        e = jnp.where(((r >> (s + 1)) == (q >> (s + 1))) & ((r >> s) != (q >> s)), n, 0.0)
        x = x - mm(mm(x, e), x)
        s += 1
    return x


def _gdn_chunk_kernel(q_ref, k_ref, v_ref, p_ref, pt_ref, gz_ref, nw_ref, s0_ref, o_ref, so_ref, s_ref, *, tt):
    c = GDN_CHUNK

    @pl.when(pl.program_id(1) == 0)
    def _():
        s_ref[...] = s0_ref[...]

    r = lax.broadcasted_iota(jnp.int32, (c, c), 0)
    q_i = lax.broadcasted_iota(jnp.int32, (c, c), 1)
    tril = r >= q_i
    strict = r > q_i
    tri_f = jnp.where(tril, 1.0, 0.0).astype(F32)
    triu_f = jnp.where(r <= q_i, 1.0, 0.0).astype(F32)
    for ci in range(tt // c):
        rows = slice(ci * c, (ci + 1) * c)
        pc = p_ref[rows, :]
        d_cols = _dot(tri_f, pc, precision=HIGHEST)
        d_rows = _dot(pt_ref[:, rows], triu_f, precision=HIGHEST)
        for h in range(GDN_HEADS):
            hs = slice(h * GDN_DK, (h + 1) * GDN_DK)
            qc, kc, vc = q_ref[rows, hs], k_ref[rows, hs], v_ref[rows, hs]
            beta = pc[:, 4 + h:5 + h]
            dcol = d_cols[:, h:h + 1]
            drow = d_rows[h:h + 1, :]
            decay = jnp.where(tril, jnp.exp(jnp.where(tril, dcol - drow, 0.0)), 0.0)
            kb = kc * beta
            kbf = kc.astype(BF16)
            n = jnp.where(strict, _dot_nt(kb.astype(BF16), kbf) * decay, 0.0)
            t_inv = _unit_lower_inverse(n, c)
            ed = jnp.exp(dcol)
            rhs = jnp.concatenate([vc * beta, kb * ed], axis=1)
            sol = _dot(t_inv.astype(BF16), rhs.astype(BF16))
            u, w = sol[:, :GDN_DK], sol[:, GDN_DK:]
            qk = jnp.where(tril, _dot_nt(qc.astype(BF16), kbf) * decay, 0.0)
            dlast = dcol[c - 1:c, :]
            kd = kc * jnp.exp(dlast - dcol)
            s_old = s_ref[h]
            s_bf = s_old.astype(BF16)
            v_new = u - _dot(w.astype(BF16), s_bf)
            vn_bf = v_new.astype(BF16)
            o = _dot((qc * ed).astype(BF16), s_bf) + _dot(qk.astype(BF16), vn_bf)
            s_ref[h] = s_old * jnp.exp(dlast) + _dot(kd.T.astype(BF16), vn_bf)
            on = _rms(o, nw_ref[...])
            o_ref[rows, hs] = on * _silu(gz_ref[rows, hs])
    so_ref[...] = s_ref[...]


def _gdn_chunked(q, k, v, p, pt, z3, nw, s0, tt):
    b, t, _ = q.shape
    blk = pl.BlockSpec((None, tt, BRANCH), lambda i, j: (i, j, 0))
    st = pl.BlockSpec((None, GDN_HEADS, GDN_DK, GDN_DK), lambda i, j: (i, 0, 0, 0))
    return pl.pallas_call(
        functools.partial(_gdn_chunk_kernel, tt=tt), grid=(b, t // tt),
        in_specs=[blk, blk, blk,
                  pl.BlockSpec((None, tt, SMALL), lambda i, j: (i, j, 0)),
                  pl.BlockSpec((None, SMALL, tt), lambda i, j: (i, 0, j)),
                  pl.BlockSpec((None, tt, BRANCH), lambda i, j: (i, j, C_GZ // BRANCH)),
                  pl.BlockSpec((1, GDN_DK), lambda i, j: (0, 0)), st],
        out_specs=[blk, st],
        out_shape=[jax.ShapeDtypeStruct((b, t, BRANCH), F32),
                   jax.ShapeDtypeStruct((b, GDN_HEADS, GDN_DK, GDN_DK), F32)],
        scratch_shapes=[pltpu.VMEM((GDN_HEADS, GDN_DK, GDN_DK), F32)],
        compiler_params=_cparams(2), name="gdn_chunked")(q, k, v, p, pt, z3, nw.reshape(1, GDN_DK), s0)


def _gdn_step_kernel(q_ref, k_ref, v_ref, p_ref, gz_ref, nw_ref, s0_ref, o_ref, so_ref):
    r8 = lax.broadcasted_iota(jnp.int32, (8, GDN_DK), 0)
    rr = lax.broadcasted_iota(jnp.int32, (GDN_DK, GDN_DK), 0)
    cc = lax.broadcasted_iota(jnp.int32, (GDN_DK, GDN_DK), 1)
    p = p_ref[...]
    for h in range(GDN_HEADS):
        hs = slice(h * GDN_DK, (h + 1) * GDN_DK)
        q, k, v = q_ref[:, hs], k_ref[:, hs], v_ref[:, hs]
        g = p[:, h:h + 1]
        beta = p[:, 4 + h:5 + h]
        a = jnp.exp(g)
        w = k * beta * a
        s_old = s0_ref[h]
        s_bf = s_old.astype(BF16)
        lhs = jnp.where(r8 == 0, jnp.broadcast_to(w, (8, GDN_DK)),
                        jnp.where(r8 == 1, jnp.broadcast_to(q * a, (8, GDN_DK)), 0.0))
        ws = _dot(lhs.astype(BF16), s_bf)
        v_new = v * beta - ws[0:1, :]
        qk = jnp.sum(q.astype(BF16).astype(F32) * k.astype(BF16).astype(F32), axis=-1, keepdims=True)
        o = ws[1:2, :] + qk.astype(BF16).astype(F32) * v_new.astype(BF16).astype(F32)
        kdiag = jnp.where(rr == cc, jnp.broadcast_to(k, (GDN_DK, GDN_DK)), 0.0)
        outer = _dot(kdiag.astype(BF16), jnp.broadcast_to(v_new, (GDN_DK, GDN_DK)).astype(BF16))
        so_ref[h] = s_old * a + outer
        on = _rms(o, nw_ref[...])
        o_ref[:, hs] = on * _silu(gz_ref[:, hs])


def _gdn_step(q, k, v, p, z3, nw, s0):
    b = q.shape[0]
    blk = pl.BlockSpec((None, 1, BRANCH), lambda i: (i, 0, 0))
    st = pl.BlockSpec((None, GDN_HEADS, GDN_DK, GDN_DK), lambda i: (i, 0, 0, 0))
    return pl.pallas_call(
        _gdn_step_kernel, grid=(b,),
        in_specs=[blk, blk, blk, pl.BlockSpec((None, 1, SMALL), lambda i: (i, 0, 0)),
                  pl.BlockSpec((None, 1, BRANCH), lambda i: (i, 0, C_GZ // BRANCH)),
                  pl.BlockSpec((1, GDN_DK), lambda i: (0, 0)), st],
        out_specs=[blk, st],
        out_shape=[jax.ShapeDtypeStruct((b, 1, BRANCH), F32),
                   jax.ShapeDtypeStruct((b, GDN_HEADS, GDN_DK, GDN_DK), F32)],
        compiler_params=_cparams(1), name="gdn_step")(q, k, v, p, z3, nw.reshape(1, GDN_DK), s0)


def _fox_kernel(q_ref, k_ref, v_ref, p_ref, pt_ref, o_ref, m_ref, l_ref, acc_ref, *, tq, tk):
    qi, kj = pl.program_id(1), pl.program_id(2)

    @pl.when(kj == 0)
    def _():
        m_ref[...] = jnp.full(m_ref.shape, NEG_INF, F32)
        l_ref[...] = jnp.zeros(l_ref.shape, F32)
        acc_ref[...] = jnp.zeros(acc_ref.shape, F32)

    @pl.when(kj <= qi)
    def _():
        row = qi * tq + lax.broadcasted_iota(jnp.int32, (tq, tk), 0)
        col = kj * tk + lax.broadcasted_iota(jnp.int32, (tq, tk), 1)
        causal = col <= row
        lane = lax.broadcasted_iota(jnp.int32, (1, LANES), 1)
        for pair in range(FOX_HEADS // 2):
            ps = slice(pair * LANES, (pair + 1) * LANES)
            q2 = q_ref[:, ps] * (FOX_DH ** -0.5)
            k2 = k_ref[:, ps].astype(BF16)
            v2 = v_ref[:, ps].astype(BF16)
            alphas, pvs = [], []
            for hh in range(2):
                h = 2 * pair + hh
                own = (lane >= hh * FOX_DH) & (lane < (hh + 1) * FOX_DH)
                s = _dot_nt(jnp.where(own, q2, 0.0).astype(BF16), k2)
                s = s + p_ref[:, 16 + h:17 + h] - pt_ref[16 + h:17 + h, :]
                s = jnp.where(causal, s, NEG_INF)
                m_old = m_ref[h][:, 0:1]
                m_new = jnp.maximum(m_old, jnp.max(s, axis=-1, keepdims=True))
                alpha = jnp.exp(m_old - m_new)
                pr = jnp.exp(s - m_new)
                l_new = alpha * l_ref[h][:, 0:1] + jnp.sum(pr, axis=-1, keepdims=True)
                m_ref[h] = jnp.broadcast_to(m_new, (tq, LANES))
                l_ref[h] = jnp.broadcast_to(l_new, (tq, LANES))
                alphas.append(alpha)
                pvs.append(_dot(pr.astype(BF16), v2))
            first = lane < FOX_DH
            acc_ref[:, ps] = acc_ref[:, ps] * jnp.where(first, alphas[0], alphas[1]) + jnp.where(first, pvs[0], pvs[1])

    @pl.when(kj == qi)
    def _():
        lane = lax.broadcasted_iota(jnp.int32, (1, LANES), 1)
        for pair in range(FOX_HEADS // 2):
            ps = slice(pair * LANES, (pair + 1) * LANES)
            l0 = l_ref[2 * pair][:, 0:1]
            l1 = l_ref[2 * pair + 1][:, 0:1]
            o_ref[:, ps] = acc_ref[:, ps] / jnp.where(lane < FOX_DH, l0, l1)


def _fox_prompt(z3, p, pt, tq):
    b, t, _ = z3.shape
    tk = tq
    nq = t // tq
    kv = lambda c: pl.BlockSpec((None, tk, BRANCH), lambda i, qi, kj, c=c: (i, jnp.minimum(kj, qi), c // BRANCH))
    return pl.pallas_call(
        functools.partial(_fox_kernel, tq=tq, tk=tk), grid=(b, nq, nq),
        in_specs=[pl.BlockSpec((None, tq, BRANCH), lambda i, qi, kj: (i, qi, C_FQ // BRANCH)),
                  kv(C_FK), kv(C_FV),
                  pl.BlockSpec((None, tq, SMALL), lambda i, qi, kj: (i, qi, 0)),
                  pl.BlockSpec((None, SMALL, tk), lambda i, qi, kj: (i, 0, jnp.minimum(kj, qi)))],
        out_specs=pl.BlockSpec((None, tq, BRANCH), lambda i, qi, kj: (i, qi, 0)),
        out_shape=jax.ShapeDtypeStruct((b, t, BRANCH), F32),
        scratch_shapes=[pltpu.VMEM((FOX_HEADS, tq, LANES), F32), pltpu.VMEM((FOX_HEADS, tq, LANES), F32),
                        pltpu.VMEM((tq, BRANCH), F32)],
        compiler_params=_cparams(3), name="fox_prompt")(z3, z3, z3, p, pt)


def _fox_decode_kernel(pt_ref, q_ref, kc_ref, vc_ref, fc_ref, *refs):
    del pt_ref
    n = PAGES_PER_STEP
    k_refs, v_refs, lf_refs = refs[:n], refs[n:2 * n], refs[2 * n:3 * n]
    o_ref, qb_ref, m_ref, l_ref, acc_ref, run_ref = refs[3 * n:]
    j = pl.program_id(1)
    nj = pl.num_programs(1)
    lane = lax.broadcasted_iota(jnp.int32, (1, LANES), 1)
    row8 = lax.broadcasted_iota(jnp.int32, (FOX_HEADS, LANES), 0)

    def to_cols(row):
        return jnp.broadcast_to(row, (LANES, BRANCH)).T

    def head_sum(x):
        out = jnp.zeros((FOX_HEADS, LANES), F32)
        for h in range(FOX_HEADS):
            sh = jnp.sum(x[h * FOX_DH:(h + 1) * FOX_DH, :], axis=0, keepdims=True)
            out = jnp.where(row8 == h, jnp.broadcast_to(sh, (FOX_HEADS, LANES)), out)
        return out

    @pl.when(j == 0)
    def _():
        qb_ref[...] = to_cols(q_ref[...] * (FOX_DH ** -0.5))
        m_ref[...] = head_sum(qb_ref[...] * to_cols(kc_ref[...]))
        l_ref[...] = jnp.ones(l_ref.shape, F32)
        acc_ref[...] = jnp.where(lane == 0, to_cols(vc_ref[...]), 0.0)
        run_ref[...] = fc_ref[...]

    qb = qb_ref[...]
    lf = jnp.concatenate([lf_refs[i][...] for i in range(n)], axis=0)
    pr_ = lax.broadcasted_iota(jnp.int32, (LANES, LANES), 0)
    pc_ = lax.broadcasted_iota(jnp.int32, (LANES, LANES), 1)
    in_page = _dot(lf, jnp.where(pr_ > pc_, 1.0, 0.0).astype(F32), precision=HIGHEST)
    tot = jnp.sum(lf, axis=-1, keepdims=True)
    run = run_ref[...]
    logits = [None] * n
    for i in reversed(range(n)):
        s_i = head_sum(k_refs[i][...].reshape(BRANCH, LANES) * qb)
        logits[i] = s_i + in_page[8 * i:8 * (i + 1), :] + run
        run = run + tot[8 * i:8 * (i + 1), :]
    run_ref[...] = run
    mx = logits[0]
    for i in range(1, n):
        mx = jnp.maximum(mx, logits[i])
    m_old = m_ref[...]
    m_new = jnp.maximum(m_old, jnp.max(mx, axis=-1, keepdims=True))
    alpha = jnp.exp(m_old - m_new)
    ps = [jnp.exp(logits[i] - m_new) for i in range(n)]
    psum = ps[0]
    for i in range(1, n):
        psum = psum + ps[i]
    l_ref[...] = alpha * l_ref[...] + jnp.sum(psum, axis=-1, keepdims=True)
    m_ref[...] = m_new
    for h in range(FOX_HEADS):
        hs = slice(h * FOX_DH, (h + 1) * FOX_DH)
        a = acc_ref[hs, :] * alpha[h:h + 1, :]
        for i in range(n):
            a = a + v_refs[i][h] * ps[i][h:h + 1, :]
        acc_ref[hs, :] = a

    @pl.when(j == nj - 1)
    def _():
        l_all = l_ref[...]
        l_cols = jnp.concatenate([jnp.broadcast_to(l_all[h:h + 1, :], (FOX_DH, LANES)) for h in range(FOX_HEADS)],
                                 axis=0)
        o_cols = jnp.broadcast_to(jnp.sum(acc_ref[...], axis=-1, keepdims=True), (BRANCH, LANES)) / l_cols
        o_ref[...] = o_cols.T[0:1, :]


def _fox_decode(page_table, q, kc, vc, fc, cache_kt, cache_vt, cache_lft, layer):
    b, n_pages = page_table.shape
    nj = n_pages // PAGES_PER_STEP
    tok = pl.BlockSpec((None, 1, BRANCH), lambda i, j, pt: (i, 0, 0))
    pidx = lambda i, j, pt, k: pt[i, (nj - 1 - j) * PAGES_PER_STEP + k]
    page = lambda k: pl.BlockSpec((None, None, FOX_HEADS, FOX_DH, PAGE),
                                  lambda i, j, pt, k=k: (pidx(i, j, pt, k), layer, 0, 0, 0))
    lpage = lambda k: pl.BlockSpec((None, None, FOX_HEADS, PAGE),
                                   lambda i, j, pt, k=k: (pidx(i, j, pt, k), layer, 0, 0))
    rng = range(PAGES_PER_STEP)
    in_specs = ([tok, tok, tok, pl.BlockSpec((None, FOX_HEADS, LANES), lambda i, j, pt: (i, 0, 0))]
                + [page(k) for k in rng] * 2 + [lpage(k) for k in rng])
    return pl.pallas_call(
        _fox_decode_kernel,
        grid_spec=pltpu.PrefetchScalarGridSpec(
            num_scalar_prefetch=1, grid=(b, nj), in_specs=in_specs, out_specs=tok,
            scratch_shapes=[pltpu.VMEM((BRANCH, LANES), F32), pltpu.VMEM((FOX_HEADS, LANES), F32),
                            pltpu.VMEM((FOX_HEADS, LANES), F32), pltpu.VMEM((BRANCH, LANES), F32),
                            pltpu.VMEM((FOX_HEADS, LANES), F32)]),
        out_shape=jax.ShapeDtypeStruct((b, 1, BRANCH), F32),
        compiler_params=_cparams(2), name="fox_decode")(
            page_table, q, kc, vc, fc, *([cache_kt] * PAGES_PER_STEP), *([cache_vt] * PAGES_PER_STEP),
            *([cache_lft] * PAGES_PER_STEP))


def _merge_kernel(oa, ob, oc, od, g0, g1, g2, g3, x_ref, wb_ref, wo_ref, out_ref):
    merged = None
    for i, (o_ref, g_ref) in enumerate(((oa, g0), (ob, g1), (oc, g2), (od, g3))):
        term = jax.nn.sigmoid(g_ref[...]) * _dot(o_ref[...].astype(BF16), wb_ref[i])
        merged = term if merged is None else merged + term
    out_ref[...] = x_ref[...] + _dot(merged.astype(BF16), wo_ref[...])


def _merge(o_list, z2, x2, wb, wo, tm):
    m = x2.shape[0]
    o_spec = pl.BlockSpec((tm, BRANCH), lambda i: (i, 0))
    gate = lambda g: pl.BlockSpec((tm, D_MODEL), lambda i, g=g: (i, C_GATE // D_MODEL + g))
    xs = pl.BlockSpec((tm, D_MODEL), lambda i: (i, 0))
    return pl.pallas_call(
        _merge_kernel, grid=(m // tm,),
        in_specs=[o_spec] * 4 + [gate(g) for g in range(4)] + [
            xs, pl.BlockSpec((4, BRANCH, D_MODEL), lambda i: (0, 0, 0)),
            pl.BlockSpec((D_MODEL, D_MODEL), lambda i: (0, 0))],
        out_specs=xs, out_shape=jax.ShapeDtypeStruct((m, D_MODEL), F32),
        compiler_params=_cparams(1), name="merge")(*o_list, z2, z2, z2, z2, x2, wb, wo)


def _cross_kernel(x_ref, g_ref, wq_ref, mk_ref, mv_ref, wo_ref, out_ref):
    x = x_ref[...]
    q = _dot(_rms(x, g_ref[...]).astype(BF16), wq_ref[...])
    mk = mk_ref[...].astype(BF16)
    mv = mv_ref[...].astype(BF16)
    outs = []
    for h in range(X_HEADS):
        hs = slice(h * X_DH, (h + 1) * X_DH)
        s = _dot_nt(q[:, hs].astype(BF16), mk[:, hs]) * (X_DH ** -0.5)
        e = jnp.exp(s - jnp.max(s, axis=-1, keepdims=True))
        pr = e / jnp.sum(e, axis=-1, keepdims=True)
        outs.append(_dot(pr.astype(BF16), mv[:, hs]))
    out_ref[...] = x + _dot(jnp.concatenate(outs, axis=1).astype(BF16), wo_ref[...])


def _cross(x3, g, wq, mk_arr, mk_spec, mv_arr, mv_spec, wo, tt):
    b, t, _ = x3.shape
    xw = X_HEADS * X_DH
    xs = pl.BlockSpec((None, tt, D_MODEL), lambda i, j: (i, j, 0))
    return pl.pallas_call(
        _cross_kernel, grid=(b, t // tt),
        in_specs=[xs, pl.BlockSpec((1, D_MODEL), lambda i, j: (0, 0)),
                  pl.BlockSpec((D_MODEL, xw), lambda i, j: (0, 0)), mk_spec, mv_spec,
                  pl.BlockSpec((xw, D_MODEL), lambda i, j: (0, 0))],
        out_specs=xs, out_shape=jax.ShapeDtypeStruct((b, t, D_MODEL), F32),
        compiler_params=_cparams(2), name="cross_attn")(x3, g.reshape(1, D_MODEL), wq, mk_arr, mv_arr, wo)


def _ffn_kernel(x_ref, g_ref, wg_ref, wu_ref, cw_ref, wd_ref, hist_ref, out_ref, st_ref,
                xn_ref, acc_ref, halo_ref, ext_ref, *, tt):
    t, f = pl.program_id(1), pl.program_id(2)
    nf = pl.num_programs(2)
    hp = _halo_rows(FFN_W)

    @pl.when(f == 0)
    def _():
        xn_ref[...] = _rms(x_ref[...], g_ref[...]).astype(BF16)
        acc_ref[...] = jnp.zeros(acc_ref.shape, F32)

    gcol = _dot(xn_ref[...], wg_ref[...])
    u = _dot(xn_ref[...], wu_ref[...])

    @pl.when(t == 0)
    def _():
        ext_ref[0:hp, :] = jnp.zeros((hp, ext_ref.shape[1]), F32)
        ext_ref[hp - (FFN_W - 1):hp, :] = hist_ref[...]

    @pl.when(t > 0)
    def _():
        ext_ref[0:hp, :] = halo_ref[f]

    ext_ref[hp:hp + tt, :] = gcol
    halo_ref[f] = ext_ref[tt:tt + hp, :]
    st_ref[...] = ext_ref[hp + tt - (FFN_W - 1):hp + tt, :]
    rb = min(tt, 256)
    for r0 in range(0, tt, rb):
        gc = _conv_rows(ext_ref, cw_ref, r0, rb, FFN_W)
        act = _silu(gc) * u[r0:r0 + rb, :]
        acc_ref[r0:r0 + rb, :] += _dot(act.astype(BF16), wd_ref[...])

    @pl.when(f == nf - 1)
    def _():
        out_ref[...] = x_ref[...] + acc_ref[...]


def _ffn(x3, g, w_in, cw, w_out, hist, tt, tf):
    b, t, _ = x3.shape
    nf = D_FF // tf
    xs = pl.BlockSpec((None, tt, D_MODEL), lambda i, j, f: (i, j, 0))
    st = pl.BlockSpec((None, FFN_W - 1, tf), lambda i, j, f: (i, 0, f))
    return pl.pallas_call(
        functools.partial(_ffn_kernel, tt=tt), grid=(b, t // tt, nf),
        in_specs=[xs, pl.BlockSpec((1, D_MODEL), lambda i, j, f: (0, 0)),
                  pl.BlockSpec((D_MODEL, tf), lambda i, j, f: (0, f)),
                  pl.BlockSpec((D_MODEL, tf), lambda i, j, f: (0, nf + f)),
                  pl.BlockSpec((FFN_W, tf), lambda i, j, f: (0, f)),
                  pl.BlockSpec((tf, D_MODEL), lambda i, j, f: (f, 0)), st],
        out_specs=[xs, st],
        out_shape=[jax.ShapeDtypeStruct((b, t, D_MODEL), F32), jax.ShapeDtypeStruct((b, FFN_W - 1, D_FF), F32)],
        scratch_shapes=[pltpu.VMEM((tt, D_MODEL), BF16), pltpu.VMEM((tt, D_MODEL), F32),
                        pltpu.VMEM((nf, _halo_rows(FFN_W), tf), F32),
                        pltpu.VMEM((_halo_rows(FFN_W) + tt, tf), F32)],
        compiler_params=_cparams(3), name="conv_ffn")(x3, g.reshape(1, D_MODEL), w_in, w_in, cw, w_out, hist)


def _ffn_step_kernel(x_ref, g_ref, wg_ref, wu_ref, cw_ref, wd_ref, h0_ref, h1_ref, out_ref, s0_ref, s1_ref,
                     xn_ref, acc_ref):
    f = pl.program_id(0)

    @pl.when(f == 0)
    def _():
        xn_ref[...] = _rms(x_ref[...], g_ref[...]).astype(BF16)
        acc_ref[...] = jnp.zeros(acc_ref.shape, F32)

    gcol = _dot(xn_ref[...], wg_ref[...])
    u = _dot(xn_ref[...], wu_ref[...])
    gc = cw_ref[0:1, :] * h0_ref[...] + cw_ref[1:2, :] * h1_ref[...] + cw_ref[2:3, :] * gcol
    s0_ref[...] = h1_ref[...]
    s1_ref[...] = gcol
    acc_ref[...] += _dot((_silu(gc) * u).astype(BF16), wd_ref[...])

    @pl.when(f == pl.num_programs(0) - 1)
    def _():
        out_ref[...] = x_ref[...] + acc_ref[...]


def _ffn_step(x2, g, w_in, cw, w_out, h0, h1, tf):
    b = x2.shape[0]
    nf = D_FF // tf
    xs = pl.BlockSpec((b, D_MODEL), lambda f: (0, 0))
    hs = pl.BlockSpec((b, tf), lambda f: (0, f))
    return pl.pallas_call(
        _ffn_step_kernel, grid=(nf,),
        in_specs=[xs, pl.BlockSpec((1, D_MODEL), lambda f: (0, 0)),
                  pl.BlockSpec((D_MODEL, tf), lambda f: (0, f)),
                  pl.BlockSpec((D_MODEL, tf), lambda f: (0, nf + f)),
                  pl.BlockSpec((FFN_W, tf), lambda f: (0, f)),
                  pl.BlockSpec((tf, D_MODEL), lambda f: (f, 0)), hs, hs],
        out_specs=[xs, hs, hs],
        out_shape=[jax.ShapeDtypeStruct((b, D_MODEL), F32), jax.ShapeDtypeStruct((b, D_FF), F32),
                   jax.ShapeDtypeStruct((b, D_FF), F32)],
        scratch_shapes=[pltpu.VMEM((b, D_MODEL), BF16), pltpu.VMEM((b, D_MODEL), F32)],
        compiler_params=_cparams(1), name="conv_ffn_step")(x2, g.reshape(1, D_MODEL), w_in, w_in, cw, w_out, h0, h1)


def _final_norm_kernel(x_ref, g_ref, o_ref):
    o_ref[...] = _rms(x_ref[...], g_ref[...])


def _final_norm(x2, g, tm):
    m = x2.shape[0]
    xs = pl.BlockSpec((tm, D_MODEL), lambda i: (i, 0))
    return pl.pallas_call(
        _final_norm_kernel, grid=(m // tm,),
        in_specs=[xs, pl.BlockSpec((1, D_MODEL), lambda i: (0, 0))], out_specs=xs,
        out_shape=jax.ShapeDtypeStruct((m, D_MODEL), F32),
        compiler_params=_cparams(1), name="final_norm")(x2, g.reshape(1, D_MODEL))


def _prep_layer(l, w_in, w_branch, w_out, w_cq, w_co, w_ffn_in, w_ffn_out, gdn_a_log, gdn_dt_bias, fox_fbias):
    w = w_in[l]
    w_main = jnp.concatenate([w[:, :3584], w[:, 3592:6152], w[:, 6160:]], axis=1).astype(BF16)
    w_small = jnp.concatenate([w[:, 3584:3592], w[:, 6152:6160],
                               jnp.zeros((D_MODEL, SMALL - 16), F32)], axis=1).astype(BF16)
    zeros4 = jnp.zeros((4,), F32)
    bias_row = jnp.concatenate([gdn_dt_bias[l], zeros4, fox_fbias[l], jnp.zeros((SMALL - 16,), F32)]).reshape(1, SMALL)
    alog_row = jnp.concatenate([gdn_a_log[l], jnp.zeros((SMALL - 4,), F32)]).reshape(1, SMALL)
    return dict(w_main=w_main, w_small=w_small, bias_row=bias_row, alog_row=alog_row,
                wb=w_branch[l].astype(BF16), wo=w_out[l].astype(BF16), wcq=w_cq[l].astype(BF16),
                wco=w_co[l].astype(BF16), wfi=w_ffn_in[l].astype(BF16), wfo=w_ffn_out[l].astype(BF16))


def kernel(x_prompt, x_sample, cache_fox_k, cache_fox_v, cache_fox_logf, cache_mem_k, cache_mem_v, state_sconv, state_gdn_conv, state_gdn, state_conf_conv, state_ffn_conv, page_table, mem_prompt, norm_mix, w_in, w_branch, w_out, sconv_w, gdn_conv_w, gdn_a_log, gdn_dt_bias, gdn_norm, conf_dw_w, conf_dw_b, conf_ln_g, conf_ln_b, fox_fbias, norm_cross, norm_mem, w_cq, w_ckv, w_co, norm_ffn, w_ffn_in, ffn_conv_w, w_ffn_out, norm_final):
    depth = w_in.shape[0]
    bp, t, _ = x_prompt.shape
    bs = x_sample.shape[0]
    n_pool = cache_fox_k.shape[0]
    xw = X_HEADS * X_DH
    prep = [_prep_layer(l, w_in, w_branch, w_out, w_cq, w_co, w_ffn_in, w_ffn_out, gdn_a_log, gdn_dt_bias, fox_fbias)
            for l in range(depth)]

    tt = min(t, 512)
    mp = bp * t
    x = x_prompt.reshape(mp, D_MODEL)
    mem2 = mem_prompt.reshape(bp * MEM_LEN, D_MODEL)
    p_out = {k: [] for k in ("fk", "fv", "fl", "mk", "mv", "sc", "gc", "gs", "cc", "ff")}
    zero = lambda *s: jnp.zeros(s, F32)
    for l in range(depth):
        w = prep[l]
        z, zs = _rms_matmul(x, norm_mix[l], w["w_main"], w["w_small"], tm=min(mp, 1024), tn=512)
        z3 = z.reshape(bp, t, NZ)
        o_a, st_sc = _sconv(z3, zero(bp, SC_W - 1, BRANCH), sconv_w[l], tt)
        q, k, v, st_gc = _gdn_pre(z3, zero(bp, GDN_W - 1, 3 * BRANCH), gdn_conv_w[l], tt)
        p, pt = _gates(zs.reshape(bp, t, SMALL), w["bias_row"], w["alog_row"], tt)
        o_b, st_gs = _gdn_chunked(q, k, v, p, pt, z3, gdn_norm[l], zero(bp, GDN_HEADS, GDN_DK, GDN_DK), tt)
        o_c, st_cc = _conformer(z3, zero(bp, CONF_W - 1, BRANCH), conf_dw_w[l], conf_dw_b[l],
                                conf_ln_g[l], conf_ln_b[l], tt)
        o_d = _fox_prompt(z3, p, pt, tt)
        o_list = [o.reshape(mp, BRANCH) for o in (o_a, o_b, o_c, o_d)]
        x = _merge(o_list, z, x, w["wb"], w["wo"], tm=min(mp, 256))
        kv = _rms_matmul(mem2, norm_mem[l], w_ckv[l].astype(BF16), None, tm=min(bp * MEM_LEN, 1024), tn=512)
        kv3 = kv.reshape(bp, MEM_LEN, 2 * xw)
        x = _cross(x.reshape(bp, t, D_MODEL), norm_cross[l], w["wcq"],
                   kv3, pl.BlockSpec((None, MEM_LEN, xw), lambda i, j: (i, 0, 0)),
                   kv3, pl.BlockSpec((None, MEM_LEN, xw), lambda i, j: (i, 0, 1)), w["wco"], tt)
        x, st_ff = _ffn(x, norm_ffn[l], w["wfi"], ffn_conv_w[l], w["wfo"], zero(bp, FFN_W - 1, D_FF),
                        tt=min(t, 1024), tf=256)
        x = x.reshape(mp, D_MODEL)
        p_out["fk"].append(z3[:, :, C_FK:C_FK + BRANCH])
        p_out["fv"].append(z3[:, :, C_FV:C_FV + BRANCH])
        p_out["fl"].append(p[:, :, 8:16])
        p_out["mk"].append(kv3[:, :, :xw])
        p_out["mv"].append(kv3[:, :, xw:])
        for key, val in (("sc", st_sc), ("gc", st_gc), ("gs", st_gs), ("cc", st_cc), ("ff", st_ff)):
            p_out[key].append(val)
    y_prompt = _final_norm(x, norm_final, tm=min(mp, 1024)).reshape(bp, t, D_MODEL)

    ckt = jnp.transpose(cache_fox_k, (0, 2, 3, 4, 1))
    cvt = jnp.transpose(cache_fox_v, (0, 2, 3, 4, 1))
    clt = jnp.transpose(cache_fox_logf, (0, 2, 3, 1))
    cmk = cache_mem_k.reshape(bs, depth, MEM_LEN, xw)
    cmv = cache_mem_v.reshape(bs, depth, MEM_LEN, xw)
    xs_ = x_sample.reshape(bs, D_MODEL)
    s_out = {k: [] for k in ("fk", "fv", "fl", "sc", "gc", "gs", "cc", "ff")}
    for l in range(depth):
        w = prep[l]
        z, zs = _rms_matmul(xs_, norm_mix[l], w["w_main"], w["w_small"], tm=bs, tn=2048)
        z3 = z.reshape(bs, 1, NZ)
        o_a, st_sc = _sconv(z3, state_sconv[:, l], sconv_w[l], 1)
        q, k, v, st_gc = _gdn_pre(z3, state_gdn_conv[:, l], gdn_conv_w[l], 1)
        p, _ = _gates(zs.reshape(bs, 1, SMALL), w["bias_row"], w["alog_row"], 1)
        o_b, st_gs = _gdn_step(q, k, v, p, z3, gdn_norm[l], state_gdn[:, l])
        o_c, st_cc = _conformer(z3, state_conf_conv[:, l], conf_dw_w[l], conf_dw_b[l], conf_ln_g[l], conf_ln_b[l], 1)
        fq, fk, fv = (z3[:, :, c:c + BRANCH] for c in (C_FQ, C_FK, C_FV))
        flog = p[:, :, 8:16]
        fc = jnp.broadcast_to(jnp.swapaxes(flog, 1, 2), (bs, FOX_HEADS, LANES))
        o_d = _fox_decode(page_table, fq, fk, fv, fc, ckt, cvt, clt, l)
        o_list = [o.reshape(bs, BRANCH) for o in (o_a, o_b, o_c, o_d)]
        xs_ = _merge(o_list, z, xs_, w["wb"], w["wo"], tm=bs)
        xs_ = _cross(xs_.reshape(bs, 1, D_MODEL), norm_cross[l], w["wcq"],
                     cmk, pl.BlockSpec((None, None, MEM_LEN, xw), lambda i, j, l=l: (i, l, 0, 0)),
                     cmv, pl.BlockSpec((None, None, MEM_LEN, xw), lambda i, j, l=l: (i, l, 0, 0)), w["wco"], 1)
        xs_, h_a, h_b = _ffn_step(xs_.reshape(bs, D_MODEL), norm_ffn[l], w["wfi"], ffn_conv_w[l], w["wfo"],
                                  state_ffn_conv[:, l, 0], state_ffn_conv[:, l, 1], tf=256)
        s_out["fk"].append(fk)
        s_out["fv"].append(fv)
        s_out["fl"].append(flog)
        for key, val in (("sc", st_sc), ("gc", st_gc), ("gs", st_gs), ("cc", st_cc),
                         ("ff", jnp.stack([h_a, h_b], axis=1))):
            s_out[key].append(val)
    y_sample = _final_norm(xs_, norm_final, tm=bs).reshape(bs, 1, D_MODEL)

    heads = lambda a: a.reshape(a.shape[0], a.shape[1], depth, FOX_HEADS, FOX_DH)
    memh = lambda a: a.reshape(bp, depth, MEM_LEN, X_HEADS, X_DH)
    return (y_prompt, y_sample,
            heads(jnp.stack(p_out["fk"], axis=2)), heads(jnp.stack(p_out["fv"], axis=2)),
            jnp.stack(p_out["fl"], axis=2),
            memh(jnp.stack(p_out["mk"], axis=1)), memh(jnp.stack(p_out["mv"], axis=1)),
            jnp.stack(p_out["sc"], axis=1), jnp.stack(p_out["gc"], axis=1), jnp.stack(p_out["gs"], axis=1),
            jnp.stack(p_out["cc"], axis=1), jnp.stack(p_out["ff"], axis=1),
            heads(jnp.stack(s_out["fk"], axis=2)), heads(jnp.stack(s_out["fv"], axis=2)),
            jnp.stack(s_out["fl"], axis=2),
            jnp.stack(s_out["sc"], axis=1), jnp.stack(s_out["gc"], axis=1), jnp.stack(s_out["gs"], axis=1),
            jnp.stack(s_out["cc"], axis=1), jnp.stack(s_out["ff"], axis=1))
```

```python
import functools

import jax
import jax.numpy as jnp
import numpy as np
from jax import lax
from jax.experimental import pallas as pl
from jax.experimental.pallas import tpu as pltpu

F32 = jnp.float32
BF16 = jnp.bfloat16
HIGHEST = lax.Precision.HIGHEST

D_MODEL = 1024
BRANCH = 512
GDN_HEADS = 4
GDN_DK = 128
FOX_HEADS = 8
FOX_DH = 64
X_HEADS = 4
X_DH = 128
D_FF = 2816
MEM_LEN = 256
PAGE = 128
SC_W, GDN_W, CONF_W, FFN_W = 3, 4, 31, 3
NEG_INF = -1e30

C_AH, C_AB, C_AC = 0, 512, 1024
C_GQKV = 1536
C_GZ = 3072
C_CV, C_CG = 3584, 4096
C_FQ, C_FK, C_FV = 4608, 5120, 5632
C_GATE = 6144
NZ = 10240
SMALL = 128
LANES = 128
GDN_CHUNK = 128
PAGES_PER_STEP = 8


def _cparams(n_axes, vmem_mb=48):
    return pltpu.CompilerParams(dimension_semantics=("arbitrary",) * n_axes,
                                vmem_limit_bytes=vmem_mb * 1024 * 1024)


def _dot(a, b, precision=None):
    return jnp.dot(a, b, preferred_element_type=F32, precision=precision)


def _dot_nt(a, b, precision=None):
    return lax.dot_general(a, b, (((1,), (1,)), ((), ())), preferred_element_type=F32, precision=precision)


def _rms(x, g, eps=1e-6):
    return (x * lax.rsqrt(jnp.mean(x * x, axis=-1, keepdims=True) + eps)) * g


def _softplus(t):
    return jnp.maximum(t, 0.0) + jnp.log1p(jnp.exp(-jnp.abs(t)))


def _silu(t):
    return t * jax.nn.sigmoid(t)


def _rms_mm_kernel(*refs, has_small):
    if has_small:
        x_ref, g_ref, w_ref, ws_ref, o_ref, os_ref, xn_ref = refs
    else:
        x_ref, g_ref, w_ref, o_ref, xn_ref = refs

    @pl.when(pl.program_id(1) == 0)
    def _():
        xn_ref[...] = _rms(x_ref[...], g_ref[...]).astype(BF16)
        if has_small:
            os_ref[...] = _dot(xn_ref[...], ws_ref[...])

    o_ref[...] = _dot(xn_ref[...], w_ref[...]).astype(o_ref.dtype)


def _rms_matmul(x, g, w, w_small, tm, tn, out_dtype=F32):
    m, k = x.shape
    n = w.shape[1]
    has_small = w_small is not None
    in_specs = [pl.BlockSpec((tm, k), lambda i, j: (i, 0)),
                pl.BlockSpec((1, k), lambda i, j: (0, 0)),
                pl.BlockSpec((k, tn), lambda i, j: (0, j))]
    out_specs = [pl.BlockSpec((tm, tn), lambda i, j: (i, j))]
    out_shape = [jax.ShapeDtypeStruct((m, n), out_dtype)]
    args = [x, g.reshape(1, k), w]
    if has_small:
        in_specs.append(pl.BlockSpec((k, SMALL), lambda i, j: (0, 0)))
        out_specs.append(pl.BlockSpec((tm, SMALL), lambda i, j: (i, 0)))
        out_shape.append(jax.ShapeDtypeStruct((m, SMALL), F32))
        args.append(w_small)
    res = pl.pallas_call(
        functools.partial(_rms_mm_kernel, has_small=has_small),
        grid=(m // tm, n // tn), in_specs=in_specs, out_specs=out_specs, out_shape=out_shape,
        scratch_shapes=[pltpu.VMEM((tm, k), BF16)], compiler_params=_cparams(2), name="rms_matmul")(*args)
    return res if has_small else res[0]


def _halo_rows(width):
    return 8 * ((width - 1 + 7) // 8)


def _conv_load_tile(x, hist_ref, ext_ref, tt, width):
    hp = _halo_rows(width)
    t = pl.program_id(1)

    @pl.when(t == 0)
    def _():
        ext_ref[0:hp, :] = jnp.zeros((hp, ext_ref.shape[1]), F32)
        ext_ref[hp - (width - 1):hp, :] = hist_ref[...]

    @pl.when(t > 0)
    def _():
        ext_ref[0:hp, :] = ext_ref[tt:tt + hp, :]

    ext_ref[hp:hp + tt, :] = x


def _conv_rows(ext_ref, w_ref, r0, nrows, width, c0=0, ncols=None):
    hp = _halo_rows(width)
    ncols = ext_ref.shape[1] if ncols is None else ncols
    acc = None
    for i in range(width):
        off = hp - (width - 1) + i + r0
        term = w_ref[i:i + 1, c0:c0 + ncols] * ext_ref[off:off + nrows, c0:c0 + ncols]
        acc = term if acc is None else acc + term
    return acc


def _conv_state(ext_ref, st_ref, tt, width):
    hp = _halo_rows(width)
    st_ref[...] = ext_ref[hp + tt - (width - 1):hp + tt, :]


def _sconv_kernel(h_ref, b_ref, c_ref, hist_ref, w_ref, o_ref, st_ref, ext_ref, *, tt):
    _conv_load_tile(c_ref[...].astype(F32) * h_ref[...].astype(F32), hist_ref, ext_ref, tt, SC_W)
    rb = min(tt, 128)
    for r0 in range(0, tt, rb):
        o_ref[r0:r0 + rb, :] = b_ref[r0:r0 + rb, :].astype(F32) * _conv_rows(ext_ref, w_ref, r0, rb, SC_W)
    _conv_state(ext_ref, st_ref, tt, SC_W)


def _sconv(z3, hist, w, tt):
    b, t, _ = z3.shape
    col = lambda c: pl.BlockSpec((None, tt, BRANCH), lambda i, j, c=c: (i, j, c // BRANCH))
    return pl.pallas_call(
        functools.partial(_sconv_kernel, tt=tt), grid=(b, t // tt),
        in_specs=[col(C_AH), col(C_AB), col(C_AC),
                  pl.BlockSpec((None, SC_W - 1, BRANCH), lambda i, j: (i, 0, 0)),
                  pl.BlockSpec((SC_W, BRANCH), lambda i, j: (0, 0))],
        out_specs=[pl.BlockSpec((None, tt, BRANCH), lambda i, j: (i, j, 0)),
                   pl.BlockSpec((None, SC_W - 1, BRANCH), lambda i, j: (i, 0, 0))],
        out_shape=[jax.ShapeDtypeStruct((b, t, BRANCH), F32),
                   jax.ShapeDtypeStruct((b, SC_W - 1, BRANCH), F32)],
        scratch_shapes=[pltpu.VMEM((_halo_rows(SC_W) + tt, BRANCH), F32)],
        compiler_params=_cparams(2), name="sconv")(z3, z3, z3, hist, w)


def _conf_kernel(v_ref, g_ref, hist_ref, w_ref, cb_ref, lg_ref, lb_ref, o_ref, st_ref, ext_ref, *, tt):
    _conv_load_tile(v_ref[...].astype(F32) * jax.nn.sigmoid(g_ref[...].astype(F32)), hist_ref, ext_ref, tt, CONF_W)
    rb = min(tt, 64)
    for r0 in range(0, tt, rb):
        y = _conv_rows(ext_ref, w_ref, r0, rb, CONF_W) + cb_ref[...]
        mu = jnp.mean(y, axis=-1, keepdims=True)
        yc = y - mu
        var = jnp.mean(yc * yc, axis=-1, keepdims=True)
        o_ref[r0:r0 + rb, :] = _silu((yc * lax.rsqrt(var + 1e-5)) * lg_ref[...] + lb_ref[...])
    _conv_state(ext_ref, st_ref, tt, CONF_W)


def _conformer(z3, hist, w, cb, lg, lb, tt):
    b, t, _ = z3.shape
    col = lambda c: pl.BlockSpec((None, tt, BRANCH), lambda i, j, c=c: (i, j, c // BRANCH))
    row = pl.BlockSpec((1, BRANCH), lambda i, j: (0, 0))
    return pl.pallas_call(
        functools.partial(_conf_kernel, tt=tt), grid=(b, t // tt),
        in_specs=[col(C_CV), col(C_CG),
                  pl.BlockSpec((None, CONF_W - 1, BRANCH), lambda i, j: (i, 0, 0)),
                  pl.BlockSpec((CONF_W, BRANCH), lambda i, j: (0, 0)), row, row, row],
        out_specs=[pl.BlockSpec((None, tt, BRANCH), lambda i, j: (i, j, 0)),
                   pl.BlockSpec((None, CONF_W - 1, BRANCH), lambda i, j: (i, 0, 0))],
        out_shape=[jax.ShapeDtypeStruct((b, t, BRANCH), F32),
                   jax.ShapeDtypeStruct((b, CONF_W - 1, BRANCH), F32)],
        scratch_shapes=[pltpu.VMEM((_halo_rows(CONF_W) + tt, BRANCH), F32)],
        compiler_params=_cparams(2), name="conformer")(
            z3, z3, hist, w, cb.reshape(1, BRANCH), lg.reshape(1, BRANCH), lb.reshape(1, BRANCH))


def _gdn_pre_kernel(x_ref, hist_ref, w_ref, q_ref, k_ref, v_ref, st_ref, ext_ref, *, tt):
    _conv_load_tile(x_ref[...].astype(F32), hist_ref, ext_ref, tt, GDN_W)
    rb = min(tt, 128)
    for r0 in range(0, tt, rb):
        for part, dst in enumerate((q_ref, k_ref, v_ref)):
            y = _silu(_conv_rows(ext_ref, w_ref, r0, rb, GDN_W, c0=part * BRANCH, ncols=BRANCH))
            if part == 2:
                dst[r0:r0 + rb, :] = y
                continue
            scale = GDN_DK ** -0.5 if part == 0 else 1.0
            for h in range(GDN_HEADS):
                blk = y[:, h * GDN_DK:(h + 1) * GDN_DK]
                nrm = lax.rsqrt(jnp.sum(blk * blk, axis=-1, keepdims=True) + 1e-6)
                dst[r0:r0 + rb, h * GDN_DK:(h + 1) * GDN_DK] = blk * nrm * scale
    _conv_state(ext_ref, st_ref, tt, GDN_W)


def _gdn_pre(z3, hist, w, tt):
    b, t, _ = z3.shape
    wq = 3 * BRANCH
    out = pl.BlockSpec((None, tt, BRANCH), lambda i, j: (i, j, 0))
    return pl.pallas_call(
        functools.partial(_gdn_pre_kernel, tt=tt), grid=(b, t // tt),
        in_specs=[pl.BlockSpec((None, tt, wq), lambda i, j: (i, j, C_GQKV // wq)),
                  pl.BlockSpec((None, GDN_W - 1, wq), lambda i, j: (i, 0, 0)),
                  pl.BlockSpec((GDN_W, wq), lambda i, j: (0, 0))],
        out_specs=[out, out, out, pl.BlockSpec((None, GDN_W - 1, wq), lambda i, j: (i, 0, 0))],
        out_shape=[jax.ShapeDtypeStruct((b, t, BRANCH), F32)] * 3 + [jax.ShapeDtypeStruct((b, GDN_W - 1, wq), F32)],
        scratch_shapes=[pltpu.VMEM((_halo_rows(GDN_W) + tt, wq), F32)],
        compiler_params=_cparams(2), name="gdn_pre")(z3, hist, w)


def _fox_bias_placement():
    pq = np.zeros((SMALL, BRANCH), np.float32)
    pk = np.zeros((SMALL, BRANCH), np.float32)
    for h in range(FOX_HEADS):
        base = (h // 2) * LANES + (FOX_DH if h % 2 == 0 else 0)
        for part in range(3):
            pq[8 * (part + 1) + h, base + part] = 1.0
            pq[0, base + 3 + part] = 1.0
            pk[0, base + part] = 1.0
            pk[8 * (part + 1) + h, base + 3 + part] = -1.0
    return jnp.asarray(pq, BF16), jnp.asarray(pk, BF16)


def _gates_kernel(zs_ref, bias_ref, alog_ref, *rest, tt, with_t):
    if with_t:
        pq_ref, pk_ref, p_ref, pt_ref, qa_ref, ka_ref, carry_ref = rest
    else:
        p_ref, carry_ref = rest
    zs = zs_ref[...]
    lane = lax.broadcasted_iota(jnp.int32, zs.shape, 1)
    tb = zs + bias_ref[...]
    g_log = -jnp.exp(alog_ref[...]) * _softplus(tb)
    beta = jax.nn.sigmoid(zs)
    flog = -_softplus(-tb)
    is_f = (lane >= 8) & (lane < 16)
    f_only = jnp.where(is_f, flog, 0.0)
    if tt > 1:
        r = lax.broadcasted_iota(jnp.int32, (tt, tt), 0)
        c = lax.broadcasted_iota(jnp.int32, (tt, tt), 1)
        csum = _dot(jnp.where(r >= c, 1.0, 0.0).astype(F32), f_only, precision=HIGHEST)
    else:
        csum = f_only

    @pl.when(pl.program_id(1) == 0)
    def _():
        carry_ref[...] = jnp.zeros_like(carry_ref)

    csum = csum + carry_ref[...]
    carry_ref[...] = csum[tt - 1:tt, :]
    p = jnp.where(lane < 4, g_log, jnp.where(lane < 8, beta, f_only)) + pltpu.roll(csum, 8, axis=1)
    p_ref[...] = p
    if with_t:
        pt_ref[...] = p.T
        hi = csum.astype(BF16).astype(F32)
        r1 = csum - hi
        mid = r1.astype(BF16).astype(F32)
        lo = (r1 - mid).astype(BF16).astype(F32)
        c3 = (hi + pltpu.roll(mid, 8, axis=1) + pltpu.roll(lo, 16, axis=1) + jnp.where(lane == 0, 1.0, 0.0)).astype(BF16)
        qa_ref[...] = _dot(c3, pq_ref[...]).astype(BF16)
        ka_ref[...] = _dot(c3, pk_ref[...]).astype(BF16)


def _gates(zs3, bias_row, alog_row, tt):
    b, t, _ = zs3.shape
    with_t = tt % LANES == 0
    row = pl.BlockSpec((1, SMALL), lambda i, j: (0, 0))
    in_specs = [pl.BlockSpec((None, tt, SMALL), lambda i, j: (i, j, 0)), row, row]
    args = [zs3, bias_row, alog_row]
    out_specs = [pl.BlockSpec((None, tt, SMALL), lambda i, j: (i, j, 0))]
    out_shape = [jax.ShapeDtypeStruct((b, t, SMALL), F32)]
    if with_t:
        place = pl.BlockSpec((SMALL, BRANCH), lambda i, j: (0, 0))
        aug = pl.BlockSpec((None, tt, BRANCH), lambda i, j: (i, j, 0))
        in_specs += [place, place]
        args += list(_fox_bias_placement())
        out_specs += [pl.BlockSpec((None, SMALL, tt), lambda i, j: (i, 0, j)), aug, aug]
        out_shape += [jax.ShapeDtypeStruct((b, SMALL, t), F32)] + [jax.ShapeDtypeStruct((b, t, BRANCH), BF16)] * 2
    res = pl.pallas_call(
        functools.partial(_gates_kernel, tt=tt, with_t=with_t), grid=(b, t // tt),
        in_specs=in_specs, out_specs=out_specs, out_shape=out_shape,
        scratch_shapes=[pltpu.VMEM((1, SMALL), F32)],
        compiler_params=_cparams(2), name="gates")(*args)
    return res if with_t else (res[0], None, None, None)


def _unit_lower_inverse_many(ns, c):
    r = lax.broadcasted_iota(jnp.int32, (c, c), 0)
    q = lax.broadcasted_iota(jnp.int32, (c, c), 1)
    mm = lambda a, b: _dot(a.astype(BF16), b.astype(BF16))
    blk8 = (r >> 3) == (q >> 3)
    eye = jnp.where(r == q, 1.0, 0.0)
    n0s = [jnp.where(blk8, n, 0.0) for n in ns]
    xs = [eye - n0 for n0 in n0s]
    ms = [mm(n0, n0) for n0 in n0s]
    xs = [x + mm(x, m) for x, m in zip(xs, ms)]
    ms = [mm(m, m) for m in ms]
    xs = [x + mm(x, m) for x, m in zip(xs, ms)]
    s = 3
    while (1 << s) < c:
        lower_left = ((r >> (s + 1)) == (q >> (s + 1))) & ((r >> s) != (q >> s))
        ts = [mm(x, jnp.where(lower_left, n, 0.0)) for x, n in zip(xs, ns)]
        xs = [x - mm(t, x) for x, t in zip(xs, ts)]
        s += 1
    return xs


def _gdn_chunk_kernel(q_ref, k_ref, v_ref, p_ref, pt_ref, gz_ref, nw_ref, s0_ref, o_ref, so_ref,
                      s_ref, u_ref, w_ref, qd_ref, kdt_ref, qk_ref, *, tt):
    c = GDN_CHUNK
    nc = tt // c

    @pl.when(pl.program_id(1) == 0)
    def _():
        s_ref[...] = s0_ref[...]

    r = lax.broadcasted_iota(jnp.int32, (c, c), 0)
    q_i = lax.broadcasted_iota(jnp.int32, (c, c), 1)
    tril = r >= q_i
    strict = r > q_i
    tri_f = jnp.where(tril, 1.0, 0.0).astype(F32)
    triu_f = jnp.where(r <= q_i, 1.0, 0.0).astype(F32)
    probs = [(ci, h) for ci in range(nc) for h in range(GDN_HEADS)]
    rows_of = lambda ci: slice(ci * c, (ci + 1) * c)
    cols_of = lambda h: slice(h * GDN_DK, (h + 1) * GDN_DK)

    d_cols = [_dot(tri_f, p_ref[rows_of(ci), :], precision=HIGHEST) for ci in range(nc)]
    d_rows = [_dot(pt_ref[:, rows_of(ci)], triu_f, precision=HIGHEST) for ci in range(nc)]
    ns, rhss, dls = [], [], []
    for ci, h in probs:
        rows, hs = rows_of(ci), cols_of(h)
        qc, kc, vc = q_ref[rows, hs], k_ref[rows, hs], v_ref[rows, hs]
        beta = p_ref[rows, 4 + h:5 + h]
        dcol = d_cols[ci][:, h:h + 1]
        drow = d_rows[ci][h:h + 1, :]
        decay = jnp.where(tril, jnp.exp(jnp.where(tril, dcol - drow, 0.0)), 0.0)
        kb = kc * beta
        kbf = kc.astype(BF16)
        ns.append(jnp.where(strict, _dot_nt(kb.astype(BF16), kbf) * decay, 0.0))
        qk_ref[ci * GDN_HEADS + h] = jnp.where(tril, _dot_nt(qc.astype(BF16), kbf) * decay, 0.0).astype(BF16)
        ed = jnp.exp(dcol)
        dlast = dcol[c - 1:c, :]
        qd_ref[rows, hs] = (qc * ed).astype(BF16)
        kdt_ref[ci * GDN_HEADS + h] = (kc * jnp.exp(dlast - dcol)).T.astype(BF16)
        rhss.append(jnp.concatenate([vc * beta, kb * ed], axis=1).astype(BF16))
        dls.append(jnp.exp(dlast))
    t_invs = _unit_lower_inverse_many(ns, c)
    for (ci, h), t_inv, rhs in zip(probs, t_invs, rhss):
        sol = _dot(t_inv.astype(BF16), rhs)
        u_ref[rows_of(ci), cols_of(h)] = sol[:, :GDN_DK]
        w_ref[rows_of(ci), cols_of(h)] = sol[:, GDN_DK:].astype(BF16)

    for ci in range(nc):
        rows = rows_of(ci)
        s_olds = [s_ref[h] for h in range(GDN_HEADS)]
        s_bfs = [s.astype(BF16) for s in s_olds]
        vns = [(u_ref[rows, cols_of(h)] - _dot(w_ref[rows, cols_of(h)], s_bfs[h])).astype(BF16)
               for h in range(GDN_HEADS)]
        for h in range(GDN_HEADS):
            hs = cols_of(h)
            idx = ci * GDN_HEADS + h
            o = _dot(qd_ref[rows, hs], s_bfs[h]) + _dot(qk_ref[idx], vns[h])
            s_ref[h] = s_olds[h] * dls[idx] + _dot(kdt_ref[idx], vns[h])
            o_ref[rows, hs] = _rms(o, nw_ref[...]) * _silu(gz_ref[rows, hs].astype(F32))
    so_ref[...] = s_ref[...]


def _gdn_chunked(q, k, v, p, pt, z3, nw, s0, tt):
    b, t, _ = q.shape
    blk = pl.BlockSpec((None, tt, BRANCH), lambda i, j: (i, j, 0))
    st = pl.BlockSpec((None, GDN_HEADS, GDN_DK, GDN_DK), lambda i, j: (i, 0, 0, 0))
    return pl.pallas_call(
        functools.partial(_gdn_chunk_kernel, tt=tt), grid=(b, t // tt),
        in_specs=[blk, blk, blk,
                  pl.BlockSpec((None, tt, SMALL), lambda i, j: (i, j, 0)),
                  pl.BlockSpec((None, SMALL, tt), lambda i, j: (i, 0, j)),
                  pl.BlockSpec((None, tt, BRANCH), lambda i, j: (i, j, C_GZ // BRANCH)),
                  pl.BlockSpec((1, GDN_DK), lambda i, j: (0, 0)), st],
        out_specs=[blk, st],
        out_shape=[jax.ShapeDtypeStruct((b, t, BRANCH), F32),
                   jax.ShapeDtypeStruct((b, GDN_HEADS, GDN_DK, GDN_DK), F32)],
        scratch_shapes=[pltpu.VMEM((GDN_HEADS, GDN_DK, GDN_DK), F32),
                        pltpu.VMEM((tt, BRANCH), F32), pltpu.VMEM((tt, BRANCH), BF16), pltpu.VMEM((tt, BRANCH), BF16),
                        pltpu.VMEM((tt // GDN_CHUNK * GDN_HEADS, GDN_DK, GDN_CHUNK), BF16),
                        pltpu.VMEM((tt // GDN_CHUNK * GDN_HEADS, GDN_CHUNK, GDN_CHUNK), BF16)],
        compiler_params=_cparams(2), name="gdn_chunked")(q, k, v, p, pt, z3, nw.reshape(1, GDN_DK), s0)


def _gdn_step_kernel(q_ref, k_ref, v_ref, p_ref, gz_ref, nw_ref, s0_ref, o_ref, so_ref):
    r8 = lax.broadcasted_iota(jnp.int32, (8, GDN_DK), 0)
    rr = lax.broadcasted_iota(jnp.int32, (GDN_DK, GDN_DK), 0)
    cc = lax.broadcasted_iota(jnp.int32, (GDN_DK, GDN_DK), 1)
    p = p_ref[...]
    for h in range(GDN_HEADS):
        hs = slice(h * GDN_DK, (h + 1) * GDN_DK)
        q, k, v = q_ref[:, hs], k_ref[:, hs], v_ref[:, hs]
        g = p[:, h:h + 1]
        beta = p[:, 4 + h:5 + h]
        a = jnp.exp(g)
        w = k * beta * a
        s_old = s0_ref[h]
        s_bf = s_old.astype(BF16)
        lhs = jnp.where(r8 == 0, jnp.broadcast_to(w, (8, GDN_DK)),
                        jnp.where(r8 == 1, jnp.broadcast_to(q * a, (8, GDN_DK)), 0.0))
        ws = _dot(lhs.astype(BF16), s_bf)
        v_new = v * beta - ws[0:1, :]
        qk = jnp.sum(q.astype(BF16).astype(F32) * k.astype(BF16).astype(F32), axis=-1, keepdims=True)
        o = ws[1:2, :] + qk.astype(BF16).astype(F32) * v_new.astype(BF16).astype(F32)
        kdiag = jnp.where(rr == cc, jnp.broadcast_to(k, (GDN_DK, GDN_DK)), 0.0)
        outer = _dot(kdiag.astype(BF16), jnp.broadcast_to(v_new, (GDN_DK, GDN_DK)).astype(BF16))
        so_ref[h] = s_old * a + outer
        on = _rms(o, nw_ref[...])
        o_ref[:, hs] = on * _silu(gz_ref[:, hs])


def _gdn_step(q, k, v, p, z3, nw, s0):
    b = q.shape[0]
    blk = pl.BlockSpec((None, 1, BRANCH), lambda i: (i, 0, 0))
    st = pl.BlockSpec((None, GDN_HEADS, GDN_DK, GDN_DK), lambda i: (i, 0, 0, 0))
    return pl.pallas_call(
        _gdn_step_kernel, grid=(b,),
        in_specs=[blk, blk, blk, pl.BlockSpec((None, 1, SMALL), lambda i: (i, 0, 0)),
                  pl.BlockSpec((None, 1, BRANCH), lambda i: (i, 0, C_GZ // BRANCH)),
                  pl.BlockSpec((1, GDN_DK), lambda i: (0, 0)), st],
        out_specs=[blk, st],
        out_shape=[jax.ShapeDtypeStruct((b, 1, BRANCH), F32),
                   jax.ShapeDtypeStruct((b, GDN_HEADS, GDN_DK, GDN_DK), F32)],
        compiler_params=_cparams(1), name="gdn_step")(q, k, v, p, z3, nw.reshape(1, GDN_DK), s0)


def _fox_kernel(q_ref, k_ref, v_ref, qa_ref, ka_ref, o_ref, m_ref, acc_ref, *, tq, tk):
    qi, kj = pl.program_id(1), pl.program_id(2)

    @pl.when(kj == 0)
    def _():
        m_ref[...] = jnp.full(m_ref.shape, NEG_INF, F32)
        acc_ref[...] = jnp.zeros(acc_ref.shape, F32)

    def step(on_diagonal):
        lane = lax.broadcasted_iota(jnp.int32, (1, LANES), 1)
        if on_diagonal:
            causal = (lax.broadcasted_iota(jnp.int32, (tq, tk), 1) <= lax.broadcasted_iota(jnp.int32, (tq, tk), 0))
        for h in range(FOX_HEADS):
            ps = slice((h // 2) * LANES, (h // 2 + 1) * LANES)
            own = (lane >= (h % 2) * FOX_DH) & (lane < (h % 2 + 1) * FOX_DH)
            q2 = (q_ref[:, ps] * (FOX_DH ** -0.5)).astype(BF16)
            s = _dot_nt(jnp.where(own, q2, qa_ref[:, ps]), jnp.where(own, k_ref[:, ps].astype(BF16), ka_ref[:, ps]))
            if on_diagonal:
                s = jnp.where(causal, s, NEG_INF)
            m_old = m_ref[h]
            m_new = jnp.maximum(m_old, jnp.max(s, axis=-1, keepdims=True))
            pr = jnp.exp(s - pltpu.repeat(m_new, tk // LANES, axis=1))
            v1 = jnp.where(own, v_ref[:, ps].astype(BF16), jnp.ones((), BF16))
            acc_ref[h] = acc_ref[h] * jnp.exp(m_old - m_new) + _dot(pr.astype(BF16), v1)
            m_ref[h] = m_new

    @pl.when(kj < qi)
    def _():
        step(False)

    @pl.when(kj == qi)
    def _():
        step(True)
        first = lax.broadcasted_iota(jnp.int32, (1, LANES), 1) < FOX_DH
        for pair in range(FOX_HEADS // 2):
            a0, a1 = acc_ref[2 * pair], acc_ref[2 * pair + 1]
            o0 = a0 / pltpu.roll(a0, FOX_DH, axis=1)
            o1 = a1 / pltpu.roll(a1, FOX_DH, axis=1)
            o_ref[:, pair * LANES:(pair + 1) * LANES] = jnp.where(first, o0, o1)


def _fox_prompt(z3, qa, ka, tq):
    b, t, _ = z3.shape
    tk = tq
    nq = t // tq
    kv = lambda c: pl.BlockSpec((None, tk, BRANCH), lambda i, qi, kj, c=c: (i, jnp.minimum(kj, qi), c // BRANCH))
    return pl.pallas_call(
        functools.partial(_fox_kernel, tq=tq, tk=tk), grid=(b, nq, nq),
        in_specs=[pl.BlockSpec((None, tq, BRANCH), lambda i, qi, kj: (i, qi, C_FQ // BRANCH)),
                  kv(C_FK), kv(C_FV),
                  pl.BlockSpec((None, tq, BRANCH), lambda i, qi, kj: (i, qi, 0)),
                  pl.BlockSpec((None, tk, BRANCH), lambda i, qi, kj: (i, jnp.minimum(kj, qi), 0))],
        out_specs=pl.BlockSpec((None, tq, BRANCH), lambda i, qi, kj: (i, qi, 0)),
        out_shape=jax.ShapeDtypeStruct((b, t, BRANCH), F32),
        scratch_shapes=[pltpu.VMEM((FOX_HEADS, tq, LANES), F32), pltpu.VMEM((FOX_HEADS, tq, LANES), F32)],
        compiler_params=_cparams(3), name="fox_prompt")(z3, z3, z3, qa, ka)


def _fox_decode_kernel(pt_ref, q_ref, kc_ref, vc_ref, fc_ref, *refs):
    del pt_ref
    n = PAGES_PER_STEP
    k_refs, v_refs, lf_refs = refs[:n], refs[n:2 * n], refs[2 * n:3 * n]
    o_ref, qb_ref, m_ref, l_ref, acc_ref, run_ref = refs[3 * n:]
    j = pl.program_id(1)
    nj = pl.num_programs(1)
    lane = lax.broadcasted_iota(jnp.int32, (1, LANES), 1)
    row8 = lax.broadcasted_iota(jnp.int32, (FOX_HEADS, LANES), 0)

    def to_cols(row):
        return jnp.broadcast_to(row, (LANES, BRANCH)).T

    def head_sum(x):
        out = jnp.zeros((FOX_HEADS, LANES), F32)
        for h in range(FOX_HEADS):
            sh = jnp.sum(x[h * FOX_DH:(h + 1) * FOX_DH, :], axis=0, keepdims=True)
            out = jnp.where(row8 == h, jnp.broadcast_to(sh, (FOX_HEADS, LANES)), out)
        return out

    @pl.when(j == 0)
    def _():
        qb_ref[...] = to_cols(q_ref[...] * (FOX_DH ** -0.5))
        m_ref[...] = head_sum(qb_ref[...] * to_cols(kc_ref[...]))
        l_ref[...] = jnp.ones(l_ref.shape, F32)
        acc_ref[...] = jnp.where(lane == 0, to_cols(vc_ref[...]), 0.0)
        run_ref[...] = fc_ref[...]

    qb = qb_ref[...]
    lf = jnp.concatenate([lf_refs[i][...] for i in range(n)], axis=0)
    pr_ = lax.broadcasted_iota(jnp.int32, (LANES, LANES), 0)
    pc_ = lax.broadcasted_iota(jnp.int32, (LANES, LANES), 1)
    in_page = _dot(lf, jnp.where(pr_ > pc_, 1.0, 0.0).astype(F32), precision=HIGHEST)
    tot = jnp.sum(lf, axis=-1, keepdims=True)
    run = run_ref[...]
    logits = [None] * n
    for i in reversed(range(n)):
        s_i = head_sum(k_refs[i][...].reshape(BRANCH, LANES) * qb)
        logits[i] = s_i + in_page[8 * i:8 * (i + 1), :] + run
        run = run + tot[8 * i:8 * (i + 1), :]
    run_ref[...] = run
    mx = logits[0]
    for i in range(1, n):
        mx = jnp.maximum(mx, logits[i])
    m_old = m_ref[...]
    m_new = jnp.maximum(m_old, jnp.max(mx, axis=-1, keepdims=True))
    alpha = jnp.exp(m_old - m_new)
    ps = [jnp.exp(logits[i] - m_new) for i in range(n)]
    psum = ps[0]
    for i in range(1, n):
        psum = psum + ps[i]
    l_ref[...] = alpha * l_ref[...] + jnp.sum(psum, axis=-1, keepdims=True)
    m_ref[...] = m_new
    for h in range(FOX_HEADS):
        hs = slice(h * FOX_DH, (h + 1) * FOX_DH)
        a = acc_ref[hs, :] * alpha[h:h + 1, :]
        for i in range(n):
            a = a + v_refs[i][h] * ps[i][h:h + 1, :]
        acc_ref[hs, :] = a

    @pl.when(j == nj - 1)
    def _():
        l_all = l_ref[...]
        l_cols = jnp.concatenate([jnp.broadcast_to(l_all[h:h + 1, :], (FOX_DH, LANES)) for h in range(FOX_HEADS)],
                                 axis=0)
        o_cols = jnp.broadcast_to(jnp.sum(acc_ref[...], axis=-1, keepdims=True), (BRANCH, LANES)) / l_cols
        o_ref[...] = o_cols.T[0:1, :]


def _fox_decode(page_table, q, kc, vc, fc, cache_kt, cache_vt, cache_lft, layer):
    b, n_pages = page_table.shape
    nj = n_pages // PAGES_PER_STEP
    tok = pl.BlockSpec((None, 1, BRANCH), lambda i, j, pt: (i, 0, 0))
    pidx = lambda i, j, pt, k: pt[i, (nj - 1 - j) * PAGES_PER_STEP + k]
    page = lambda k: pl.BlockSpec((None, None, FOX_HEADS, FOX_DH, PAGE),
                                  lambda i, j, pt, k=k: (pidx(i, j, pt, k), layer, 0, 0, 0))
    lpage = lambda k: pl.BlockSpec((None, None, FOX_HEADS, PAGE),
                                   lambda i, j, pt, k=k: (pidx(i, j, pt, k), layer, 0, 0))
    rng = range(PAGES_PER_STEP)
    in_specs = ([tok, tok, tok, pl.BlockSpec((None, FOX_HEADS, LANES), lambda i, j, pt: (i, 0, 0))]
                + [page(k) for k in rng] * 2 + [lpage(k) for k in rng])
    return pl.pallas_call(
        _fox_decode_kernel,
        grid_spec=pltpu.PrefetchScalarGridSpec(
            num_scalar_prefetch=1, grid=(b, nj), in_specs=in_specs, out_specs=tok,
            scratch_shapes=[pltpu.VMEM((BRANCH, LANES), F32), pltpu.VMEM((FOX_HEADS, LANES), F32),
                            pltpu.VMEM((FOX_HEADS, LANES), F32), pltpu.VMEM((BRANCH, LANES), F32),
                            pltpu.VMEM((FOX_HEADS, LANES), F32)]),
        out_shape=jax.ShapeDtypeStruct((b, 1, BRANCH), F32),
        compiler_params=_cparams(2), name="fox_decode")(
            page_table, q, kc, vc, fc, *([cache_kt] * PAGES_PER_STEP), *([cache_vt] * PAGES_PER_STEP),
            *([cache_lft] * PAGES_PER_STEP))


def _merge_kernel(oa, ob, oc, od, g0, g1, g2, g3, x_ref, wb_ref, wo_ref, out_ref):
    merged = None
    for i, (o_ref, g_ref) in enumerate(((oa, g0), (ob, g1), (oc, g2), (od, g3))):
        term = jax.nn.sigmoid(g_ref[...].astype(F32)) * _dot(o_ref[...].astype(BF16), wb_ref[i])
        merged = term if merged is None else merged + term
    out_ref[...] = x_ref[...] + _dot(merged.astype(BF16), wo_ref[...])


def _merge(o_list, z2, x2, wb, wo, tm):
    m = x2.shape[0]
    o_spec = pl.BlockSpec((tm, BRANCH), lambda i: (i, 0))
    gate = lambda g: pl.BlockSpec((tm, D_MODEL), lambda i, g=g: (i, C_GATE // D_MODEL + g))
    xs = pl.BlockSpec((tm, D_MODEL), lambda i: (i, 0))
    return pl.pallas_call(
        _merge_kernel, grid=(m // tm,),
        in_specs=[o_spec] * 4 + [gate(g) for g in range(4)] + [
            xs, pl.BlockSpec((4, BRANCH, D_MODEL), lambda i: (0, 0, 0)),
            pl.BlockSpec((D_MODEL, D_MODEL), lambda i: (0, 0))],
        out_specs=xs, out_shape=jax.ShapeDtypeStruct((m, D_MODEL), F32),
        compiler_params=_cparams(1), name="merge")(*o_list, z2, z2, z2, z2, x2, wb, wo)


def _cross_kernel(x_ref, g_ref, wq_ref, mk_ref, mv_ref, wo_ref, out_ref):
    x = x_ref[...]
    q = _dot(_rms(x, g_ref[...]).astype(BF16), wq_ref[...])
    mk = mk_ref[...].astype(BF16)
    mv = mv_ref[...].astype(BF16)
    outs = []
    for h in range(X_HEADS):
        hs = slice(h * X_DH, (h + 1) * X_DH)
        s = _dot_nt(q[:, hs].astype(BF16), mk[:, hs]) * (X_DH ** -0.5)
        e = jnp.exp(s - jnp.max(s, axis=-1, keepdims=True))
        pr = e / jnp.sum(e, axis=-1, keepdims=True)
        outs.append(_dot(pr.astype(BF16), mv[:, hs]))
    out_ref[...] = x + _dot(jnp.concatenate(outs, axis=1).astype(BF16), wo_ref[...])


def _cross(x3, g, wq, mk_arr, mk_spec, mv_arr, mv_spec, wo, tt):
    b, t, _ = x3.shape
    xw = X_HEADS * X_DH
    xs = pl.BlockSpec((None, tt, D_MODEL), lambda i, j: (i, j, 0))
    return pl.pallas_call(
        _cross_kernel, grid=(b, t // tt),
        in_specs=[xs, pl.BlockSpec((1, D_MODEL), lambda i, j: (0, 0)),
                  pl.BlockSpec((D_MODEL, xw), lambda i, j: (0, 0)), mk_spec, mv_spec,
                  pl.BlockSpec((xw, D_MODEL), lambda i, j: (0, 0))],
        out_specs=xs, out_shape=jax.ShapeDtypeStruct((b, t, D_MODEL), F32),
        compiler_params=_cparams(2), name="cross_attn")(x3, g.reshape(1, D_MODEL), wq, mk_arr, mv_arr, wo)


def _ffn_kernel(x_ref, g_ref, wg_ref, wu_ref, cw_ref, wd_ref, hist_ref, out_ref, st_ref,
                xn_ref, acc_ref, halo_ref, ext_ref, *, tt, tf, nf):
    t, f = pl.program_id(1), pl.program_id(2)
    hp = _halo_rows(FFN_W)

    @pl.when(f == 0)
    def _():
        xn_ref[...] = _rms(x_ref[...], g_ref[...]).astype(BF16)
        acc_ref[...] = jnp.zeros(acc_ref.shape, F32)

    gcol = _dot(xn_ref[...], wg_ref[...])
    u = _dot(xn_ref[...], wu_ref[...])

    @pl.when(t == 0)
    def _():
        ext_ref[0:hp, :] = jnp.zeros((hp, ext_ref.shape[1]), F32)
        ext_ref[hp - (FFN_W - 1):hp, :] = hist_ref[...]

    @pl.when(t > 0)
    def _():
        ext_ref[0:hp, :] = halo_ref[f]

    ext_ref[hp:hp + tt, :] = gcol
    halo_ref[f] = ext_ref[tt:tt + hp, :]
    rb = min(tt, 256)
    for r0 in range(0, tt, rb):
        gc = _conv_rows(ext_ref, cw_ref, r0, rb, FFN_W)
        act = _silu(gc) * u[r0:r0 + rb, :]
        acc_ref[r0:r0 + rb, :] += _dot(act.astype(BF16), wd_ref[...])

    @pl.when(f == nf - 1)
    def _():
        out_ref[...] = x_ref[...] + acc_ref[...]

    @pl.when((f == nf - 1) & (t == pl.num_programs(1) - 1))
    def _():
        for ff in range(nf):
            st_ref[:, ff * tf:(ff + 1) * tf] = halo_ref[ff][hp - (FFN_W - 1):hp, :]


def _ffn(x3, g, w_in, cw, w_out, hist, tt, tf):
    b, t, _ = x3.shape
    nf = D_FF // tf
    xs = pl.BlockSpec((None, tt, D_MODEL), lambda i, j, f: (i, j, 0))
    st = pl.BlockSpec((None, FFN_W - 1, tf), lambda i, j, f: (i, 0, f))
    st_out = pl.BlockSpec((None, FFN_W - 1, D_FF), lambda i, j, f: (i, 0, 0))
    return pl.pallas_call(
        functools.partial(_ffn_kernel, tt=tt, tf=tf, nf=nf), grid=(b, t // tt, nf),
        in_specs=[xs, pl.BlockSpec((1, D_MODEL), lambda i, j, f: (0, 0)),
                  pl.BlockSpec((D_MODEL, tf), lambda i, j, f: (0, f)),
                  pl.BlockSpec((D_MODEL, tf), lambda i, j, f: (0, nf + f)),
                  pl.BlockSpec((FFN_W, tf), lambda i, j, f: (0, f)),
                  pl.BlockSpec((tf, D_MODEL), lambda i, j, f: (f, 0)), st],
        out_specs=[xs, st_out],
        out_shape=[jax.ShapeDtypeStruct((b, t, D_MODEL), F32), jax.ShapeDtypeStruct((b, FFN_W - 1, D_FF), F32)],
        scratch_shapes=[pltpu.VMEM((tt, D_MODEL), BF16), pltpu.VMEM((tt, D_MODEL), F32),
                        pltpu.VMEM((nf, _halo_rows(FFN_W), tf), F32),
                        pltpu.VMEM((_halo_rows(FFN_W) + tt, tf), F32)],
        compiler_params=_cparams(3), name="conv_ffn")(x3, g.reshape(1, D_MODEL), w_in, w_in, cw, w_out, hist)


def _ffn_step_kernel(x_ref, g_ref, wg_ref, wu_ref, cw_ref, wd_ref, h0_ref, h1_ref, out_ref, s0_ref, s1_ref,
                     xn_ref, acc_ref):
    f = pl.program_id(0)

    @pl.when(f == 0)
    def _():
        xn_ref[...] = _rms(x_ref[...], g_ref[...]).astype(BF16)
        acc_ref[...] = jnp.zeros(acc_ref.shape, F32)

    gcol = _dot(xn_ref[...], wg_ref[...])
    u = _dot(xn_ref[...], wu_ref[...])
    gc = cw_ref[0:1, :] * h0_ref[...] + cw_ref[1:2, :] * h1_ref[...] + cw_ref[2:3, :] * gcol
    s0_ref[...] = h1_ref[...]
    s1_ref[...] = gcol
    acc_ref[...] += _dot((_silu(gc) * u).astype(BF16), wd_ref[...])

    @pl.when(f == pl.num_programs(0) - 1)
    def _():
        out_ref[...] = x_ref[...] + acc_ref[...]


def _ffn_step(x2, g, w_in, cw, w_out, h0, h1, tf):
    b = x2.shape[0]
    nf = D_FF // tf
    xs = pl.BlockSpec((b, D_MODEL), lambda f: (0, 0))
    hs = pl.BlockSpec((b, tf), lambda f: (0, f))
    return pl.pallas_call(
        _ffn_step_kernel, grid=(nf,),
        in_specs=[xs, pl.BlockSpec((1, D_MODEL), lambda f: (0, 0)),
                  pl.BlockSpec((D_MODEL, tf), lambda f: (0, f)),
                  pl.BlockSpec((D_MODEL, tf), lambda f: (0, nf + f)),
                  pl.BlockSpec((FFN_W, tf), lambda f: (0, f)),
                  pl.BlockSpec((tf, D_MODEL), lambda f: (f, 0)), hs, hs],
        out_specs=[xs, hs, hs],
        out_shape=[jax.ShapeDtypeStruct((b, D_MODEL), F32), jax.ShapeDtypeStruct((b, D_FF), F32),
                   jax.ShapeDtypeStruct((b, D_FF), F32)],
        scratch_shapes=[pltpu.VMEM((b, D_MODEL), BF16), pltpu.VMEM((b, D_MODEL), F32)],
        compiler_params=_cparams(1), name="conv_ffn_step")(x2, g.reshape(1, D_MODEL), w_in, w_in, cw, w_out, h0, h1)


def _final_norm_kernel(x_ref, g_ref, o_ref):
    o_ref[...] = _rms(x_ref[...], g_ref[...])


def _final_norm(x2, g, tm):
    m = x2.shape[0]
    xs = pl.BlockSpec((tm, D_MODEL), lambda i: (i, 0))
    return pl.pallas_call(
        _final_norm_kernel, grid=(m // tm,),
        in_specs=[xs, pl.BlockSpec((1, D_MODEL), lambda i: (0, 0))], out_specs=xs,
        out_shape=jax.ShapeDtypeStruct((m, D_MODEL), F32),
        compiler_params=_cparams(1), name="final_norm")(x2, g.reshape(1, D_MODEL))


def _prep_layer(l, w_in, w_branch, w_out, w_cq, w_co, w_ffn_in, w_ffn_out, gdn_a_log, gdn_dt_bias, fox_fbias):
    w = w_in[l]
    w_main = jnp.concatenate([w[:, :3584], w[:, 3592:6152], w[:, 6160:]], axis=1).astype(BF16)
    w_small = jnp.concatenate([w[:, 3584:3592], w[:, 6152:6160],
                               jnp.zeros((D_MODEL, SMALL - 16), F32)], axis=1).astype(BF16)
    zeros4 = jnp.zeros((4,), F32)
    bias_row = jnp.concatenate([gdn_dt_bias[l], zeros4, fox_fbias[l], jnp.zeros((SMALL - 16,), F32)]).reshape(1, SMALL)
    alog_row = jnp.concatenate([gdn_a_log[l], jnp.zeros((SMALL - 4,), F32)]).reshape(1, SMALL)
    return dict(w_main=w_main, w_small=w_small, bias_row=bias_row, alog_row=alog_row,
                wb=w_branch[l].astype(BF16), wo=w_out[l].astype(BF16), wcq=w_cq[l].astype(BF16),
                wco=w_co[l].astype(BF16), wfi=w_ffn_in[l].astype(BF16), wfo=w_ffn_out[l].astype(BF16))


def kernel(x_prompt, x_sample, cache_fox_k, cache_fox_v, cache_fox_logf, cache_mem_k, cache_mem_v, state_sconv, state_gdn_conv, state_gdn, state_conf_conv, state_ffn_conv, page_table, mem_prompt, norm_mix, w_in, w_branch, w_out, sconv_w, gdn_conv_w, gdn_a_log, gdn_dt_bias, gdn_norm, conf_dw_w, conf_dw_b, conf_ln_g, conf_ln_b, fox_fbias, norm_cross, norm_mem, w_cq, w_ckv, w_co, norm_ffn, w_ffn_in, ffn_conv_w, w_ffn_out, norm_final):
    depth = w_in.shape[0]
    bp, t, _ = x_prompt.shape
    bs = x_sample.shape[0]
    n_pool = cache_fox_k.shape[0]
    xw = X_HEADS * X_DH
    prep = [_prep_layer(l, w_in, w_branch, w_out, w_cq, w_co, w_ffn_in, w_ffn_out, gdn_a_log, gdn_dt_bias, fox_fbias)
            for l in range(depth)]

    tt = min(t, 512)
    mp = bp * t
    x = x_prompt.reshape(mp, D_MODEL)
    mem2 = mem_prompt.reshape(bp * MEM_LEN, D_MODEL)
    p_out = {k: [] for k in ("fk", "fv", "fl", "mk", "mv", "sc", "gc", "gs", "cc", "ff")}
    zero = lambda *s: jnp.zeros(s, F32)
    for l in range(depth):
        w = prep[l]
        z, zs = _rms_matmul(x, norm_mix[l], w["w_main"], w["w_small"], tm=min(mp, 2048), tn=512, out_dtype=BF16)
        z3 = z.reshape(bp, t, NZ)
        o_a, st_sc = _sconv(z3, zero(bp, SC_W - 1, BRANCH), sconv_w[l], tt)
        q, k, v, st_gc = _gdn_pre(z3, zero(bp, GDN_W - 1, 3 * BRANCH), gdn_conv_w[l], tt)
        p, pt, qa, ka = _gates(zs.reshape(bp, t, SMALL), w["bias_row"], w["alog_row"], tt)
        o_b, st_gs = _gdn_chunked(q, k, v, p, pt, z3, gdn_norm[l], zero(bp, GDN_HEADS, GDN_DK, GDN_DK), tt)
        o_c, st_cc = _conformer(z3, zero(bp, CONF_W - 1, BRANCH), conf_dw_w[l], conf_dw_b[l],
                                conf_ln_g[l], conf_ln_b[l], tt)
        o_d = _fox_prompt(z3, qa, ka, tt)
        o_list = [o.reshape(mp, BRANCH) for o in (o_a, o_b, o_c, o_d)]
        x = _merge(o_list, z, x, w["wb"], w["wo"], tm=min(mp, 256))
        kv = _rms_matmul(mem2, norm_mem[l], w_ckv[l].astype(BF16), None, tm=min(bp * MEM_LEN, 1024), tn=512)
        kv3 = kv.reshape(bp, MEM_LEN, 2 * xw)
        x = _cross(x.reshape(bp, t, D_MODEL), norm_cross[l], w["wcq"],
                   kv3, pl.BlockSpec((None, MEM_LEN, xw), lambda i, j: (i, 0, 0)),
                   kv3, pl.BlockSpec((None, MEM_LEN, xw), lambda i, j: (i, 0, 1)), w["wco"], tt)
        x, st_ff = _ffn(x, norm_ffn[l], w["wfi"], ffn_conv_w[l], w["wfo"], zero(bp, FFN_W - 1, D_FF),
                        tt=min(t, 512), tf=D_FF // 2)
        x = x.reshape(mp, D_MODEL)
        p_out["fk"].append(z3[:, :, C_FK:C_FK + BRANCH].astype(F32))
        p_out["fv"].append(z3[:, :, C_FV:C_FV + BRANCH].astype(F32))
        p_out["fl"].append(p[:, :, 8:16])
        p_out["mk"].append(kv3[:, :, :xw])
        p_out["mv"].append(kv3[:, :, xw:])
        for key, val in (("sc", st_sc), ("gc", st_gc), ("gs", st_gs), ("cc", st_cc), ("ff", st_ff)):
            p_out[key].append(val)
    y_prompt = _final_norm(x, norm_final, tm=min(mp, 1024)).reshape(bp, t, D_MODEL)

    ckt = jnp.transpose(cache_fox_k, (0, 2, 3, 4, 1))
    cvt = jnp.transpose(cache_fox_v, (0, 2, 3, 4, 1))
    clt = jnp.transpose(cache_fox_logf, (0, 2, 3, 1))
    cmk = cache_mem_k.reshape(bs, depth, MEM_LEN, xw)
    cmv = cache_mem_v.reshape(bs, depth, MEM_LEN, xw)
    xs_ = x_sample.reshape(bs, D_MODEL)
    s_out = {k: [] for k in ("fk", "fv", "fl", "sc", "gc", "gs", "cc", "ff")}
    for l in range(depth):
        w = prep[l]
        z, zs = _rms_matmul(xs_, norm_mix[l], w["w_main"], w["w_small"], tm=bs, tn=2048)
        z3 = z.reshape(bs, 1, NZ)
        o_a, st_sc = _sconv(z3, state_sconv[:, l], sconv_w[l], 1)
        q, k, v, st_gc = _gdn_pre(z3, state_gdn_conv[:, l], gdn_conv_w[l], 1)
        p = _gates(zs.reshape(bs, 1, SMALL), w["bias_row"], w["alog_row"], 1)[0]
        o_b, st_gs = _gdn_step(q, k, v, p, z3, gdn_norm[l], state_gdn[:, l])
        o_c, st_cc = _conformer(z3, state_conf_conv[:, l], conf_dw_w[l], conf_dw_b[l], conf_ln_g[l], conf_ln_b[l], 1)
        fq, fk, fv = (z3[:, :, c:c + BRANCH] for c in (C_FQ, C_FK, C_FV))
        flog = p[:, :, 8:16]
        fc = jnp.broadcast_to(jnp.swapaxes(flog, 1, 2), (bs, FOX_HEADS, LANES))
        o_d = _fox_decode(page_table, fq, fk, fv, fc, ckt, cvt, clt, l)
        o_list = [o.reshape(bs, BRANCH) for o in (o_a, o_b, o_c, o_d)]
        xs_ = _merge(o_list, z, xs_, w["wb"], w["wo"], tm=bs)
        xs_ = _cross(xs_.reshape(bs, 1, D_MODEL), norm_cross[l], w["wcq"],
                     cmk, pl.BlockSpec((None, None, MEM_LEN, xw), lambda i, j, l=l: (i, l, 0, 0)),
                     cmv, pl.BlockSpec((None, None, MEM_LEN, xw), lambda i, j, l=l: (i, l, 0, 0)), w["wco"], 1)
        xs_, h_a, h_b = _ffn_step(xs_.reshape(bs, D_MODEL), norm_ffn[l], w["wfi"], ffn_conv_w[l], w["wfo"],
                                  state_ffn_conv[:, l, 0], state_ffn_conv[:, l, 1], tf=256)
        s_out["fk"].append(fk)
        s_out["fv"].append(fv)
        s_out["fl"].append(flog)
        for key, val in (("sc", st_sc), ("gc", st_gc), ("gs", st_gs), ("cc", st_cc),
                         ("ff", jnp.stack([h_a, h_b], axis=1))):
            s_out[key].append(val)
    y_sample = _final_norm(xs_, norm_final, tm=bs).reshape(bs, 1, D_MODEL)

    heads = lambda a: a.reshape(a.shape[0], a.shape[1], depth, FOX_HEADS, FOX_DH)
    memh = lambda a: a.reshape(bp, depth, MEM_LEN, X_HEADS, X_DH)
    return (y_prompt, y_sample,
            heads(jnp.stack(p_out["fk"], axis=2)), heads(jnp.stack(p_out["fv"], axis=2)),
            jnp.stack(p_out["fl"], axis=2),
            memh(jnp.stack(p_out["mk"], axis=1)), memh(jnp.stack(p_out["mv"], axis=1)),
            jnp.stack(p_out["sc"], axis=1), jnp.stack(p_out["gc"], axis=1), jnp.stack(p_out["gs"], axis=1),
            jnp.stack(p_out["cc"], axis=1), jnp.stack(p_out["ff"], axis=1),
            heads(jnp.stack(s_out["fk"], axis=2)), heads(jnp.stack(s_out["fv"], axis=2)),
            jnp.stack(s_out["fl"], axis=2),
            jnp.stack(s_out["sc"], axis=1), jnp.stack(s_out["gc"], axis=1), jnp.stack(s_out["gs"], axis=1),
            jnp.stack(s_out["cc"], axis=1), jnp.stack(s_out["ff"], axis=1))
```

```python
import functools

import jax
import jax.numpy as jnp
import numpy as np
from jax import lax
from jax.experimental import pallas as pl
from jax.experimental.pallas import tpu as pltpu

F32 = jnp.float32
BF16 = jnp.bfloat16
HIGHEST = lax.Precision.HIGHEST

D_MODEL = 1024
BRANCH = 512
GDN_HEADS = 4
GDN_DK = 128
FOX_HEADS = 8
FOX_DH = 64
X_HEADS = 4
X_DH = 128
D_FF = 2816
MEM_LEN = 256
PAGE = 128
SC_W, GDN_W, CONF_W, FFN_W = 3, 4, 31, 3
NEG_INF = -1e30

C_AH, C_AB, C_AC = 0, 512, 1024
C_GQKV = 1536
C_GZ = 3072
C_CV, C_CG = 3584, 4096
C_FQ, C_FK, C_FV = 4608, 5120, 5632
C_GATE = 6144
NZ = 10240
SMALL = 128
LANES = 128
GDN_CHUNK = 128
PAGES_PER_STEP = 16


def _cparams(n_axes, vmem_mb=48):
    return pltpu.CompilerParams(dimension_semantics=("arbitrary",) * n_axes,
                                vmem_limit_bytes=vmem_mb * 1024 * 1024)


def _dot(a, b, precision=None):
    return jnp.dot(a, b, preferred_element_type=F32, precision=precision)


def _dot_nt(a, b, precision=None):
    return lax.dot_general(a, b, (((1,), (1,)), ((), ())), preferred_element_type=F32, precision=precision)


def _rms(x, g, eps=1e-6):
    return (x * lax.rsqrt(jnp.mean(x * x, axis=-1, keepdims=True) + eps)) * g


def _softplus(t):
    return jnp.maximum(t, 0.0) + jnp.log1p(jnp.exp(-jnp.abs(t)))


def _silu(t):
    return t * jax.nn.sigmoid(t)


def _rms_mm_kernel(*refs, has_small):
    if has_small:
        x_ref, g_ref, w_ref, ws_ref, o_ref, os_ref, xn_ref = refs
    else:
        x_ref, g_ref, w_ref, o_ref, xn_ref = refs

    @pl.when(pl.program_id(1) == 0)
    def _():
        xn_ref[...] = _rms(x_ref[...], g_ref[...]).astype(BF16)
        if has_small:
            os_ref[...] = _dot(xn_ref[...], ws_ref[...])

    o_ref[...] = _dot(xn_ref[...], w_ref[...]).astype(o_ref.dtype)


def _rms_matmul(x, g, w, w_small, tm, tn, out_dtype=F32):
    m, k = x.shape
    n = w.shape[1]
    has_small = w_small is not None
    in_specs = [pl.BlockSpec((tm, k), lambda i, j: (i, 0)),
                pl.BlockSpec((1, k), lambda i, j: (0, 0)),
                pl.BlockSpec((k, tn), lambda i, j: (0, j))]
    out_specs = [pl.BlockSpec((tm, tn), lambda i, j: (i, j))]
    out_shape = [jax.ShapeDtypeStruct((m, n), out_dtype)]
    args = [x, g.reshape(1, k), w]
    if has_small:
        in_specs.append(pl.BlockSpec((k, SMALL), lambda i, j: (0, 0)))
        out_specs.append(pl.BlockSpec((tm, SMALL), lambda i, j: (i, 0)))
        out_shape.append(jax.ShapeDtypeStruct((m, SMALL), F32))
        args.append(w_small)
    res = pl.pallas_call(
        functools.partial(_rms_mm_kernel, has_small=has_small),
        grid=(m // tm, n // tn), in_specs=in_specs, out_specs=out_specs, out_shape=out_shape,
        scratch_shapes=[pltpu.VMEM((tm, k), BF16)], compiler_params=_cparams(2), name="rms_matmul")(*args)
    return res if has_small else res[0]


def _halo_rows(width):
    return 8 * ((width - 1 + 7) // 8)


def _conv_load_tile(x, hist_ref, ext_ref, tt, width):
    hp = _halo_rows(width)
    t = pl.program_id(1)

    @pl.when(t == 0)
    def _():
        ext_ref[0:hp, :] = jnp.zeros((hp, ext_ref.shape[1]), F32)
        ext_ref[hp - (width - 1):hp, :] = hist_ref[...]

    @pl.when(t > 0)
    def _():
        ext_ref[0:hp, :] = ext_ref[tt:tt + hp, :]

    ext_ref[hp:hp + tt, :] = x


def _conv_rows(ext_ref, w_ref, r0, nrows, width, c0=0, ncols=None):
    hp = _halo_rows(width)
    ncols = ext_ref.shape[1] if ncols is None else ncols
    acc = None
    for i in range(width):
        off = hp - (width - 1) + i + r0
        term = w_ref[i:i + 1, c0:c0 + ncols] * ext_ref[off:off + nrows, c0:c0 + ncols]
        acc = term if acc is None else acc + term
    return acc


def _conv_state(ext_ref, st_ref, tt, width):
    hp = _halo_rows(width)
    st_ref[...] = ext_ref[hp + tt - (width - 1):hp + tt, :]


def _sconv_kernel(h_ref, b_ref, c_ref, hist_ref, w_ref, o_ref, st_ref, ext_ref, *, tt):
    _conv_load_tile(c_ref[...].astype(F32) * h_ref[...].astype(F32), hist_ref, ext_ref, tt, SC_W)
    rb = min(tt, 128)
    for r0 in range(0, tt, rb):
        o_ref[r0:r0 + rb, :] = b_ref[r0:r0 + rb, :].astype(F32) * _conv_rows(ext_ref, w_ref, r0, rb, SC_W)
    _conv_state(ext_ref, st_ref, tt, SC_W)


def _hist_spec(hist, layer, rows, cols):
    if hist.ndim == 4:
        return pl.BlockSpec((None, None, rows, cols), lambda i, j: (i, layer, 0, 0))
    return pl.BlockSpec((None, rows, cols), lambda i, j: (i, 0, 0))


def _sconv(z3, hist, w, tt, layer=0):
    b, t, _ = z3.shape
    col = lambda c: pl.BlockSpec((None, tt, BRANCH), lambda i, j, c=c: (i, j, c // BRANCH))
    return pl.pallas_call(
        functools.partial(_sconv_kernel, tt=tt), grid=(b, t // tt),
        in_specs=[col(C_AH), col(C_AB), col(C_AC),
                  _hist_spec(hist, layer, SC_W - 1, BRANCH),
                  pl.BlockSpec((SC_W, BRANCH), lambda i, j: (0, 0))],
        out_specs=[pl.BlockSpec((None, tt, BRANCH), lambda i, j: (i, j, 0)),
                   pl.BlockSpec((None, SC_W - 1, BRANCH), lambda i, j: (i, 0, 0))],
        out_shape=[jax.ShapeDtypeStruct((b, t, BRANCH), F32),
                   jax.ShapeDtypeStruct((b, SC_W - 1, BRANCH), F32)],
        scratch_shapes=[pltpu.VMEM((_halo_rows(SC_W) + tt, BRANCH), F32)],
        compiler_params=_cparams(2), name="sconv")(z3, z3, z3, hist, w)


def _conf_kernel(v_ref, g_ref, hist_ref, w_ref, cb_ref, lg_ref, lb_ref, o_ref, st_ref, ext_ref, *phase, tt):
    _conv_load_tile(v_ref[...].astype(F32) * jax.nn.sigmoid(g_ref[...].astype(F32)), hist_ref, ext_ref, tt, CONF_W)
    hp = _halo_rows(CONF_W)
    if phase:
        (ph_ref,) = phase
        n = hp + tt - 8
        for r in range(1, 8):
            ph_ref[r - 1, 0:n, :] = ext_ref[r:r + n, :]
    rb = min(tt, 64)
    for r0 in range(0, tt, rb):
        if phase:
            y = None
            for i in range(CONF_W):
                a, r = divmod(hp - (CONF_W - 1) + i, 8)
                rows = slice(8 * a + r0, 8 * a + r0 + rb)
                term = w_ref[i:i + 1, :] * (ext_ref[rows, :] if r == 0 else ph_ref[r - 1, rows, :])
                y = term if y is None else y + term
            y = y + cb_ref[...]
        else:
            y = _conv_rows(ext_ref, w_ref, r0, rb, CONF_W) + cb_ref[...]
        mu = jnp.mean(y, axis=-1, keepdims=True)
        yc = y - mu
        var = jnp.mean(yc * yc, axis=-1, keepdims=True)
        o_ref[r0:r0 + rb, :] = _silu((yc * lax.rsqrt(var + 1e-5)) * lg_ref[...] + lb_ref[...])
    _conv_state(ext_ref, st_ref, tt, CONF_W)


def _conformer(z3, hist, w, cb, lg, lb, tt, layer=0):
    b, t, _ = z3.shape
    col = lambda c: pl.BlockSpec((None, tt, BRANCH), lambda i, j, c=c: (i, j, c // BRANCH))
    row = pl.BlockSpec((1, BRANCH), lambda i, j: (0, 0))
    scratch = [pltpu.VMEM((_halo_rows(CONF_W) + tt, BRANCH), F32)]
    if tt % 8 == 0:
        scratch.append(pltpu.VMEM((7, _halo_rows(CONF_W) + tt, BRANCH), F32))
    return pl.pallas_call(
        functools.partial(_conf_kernel, tt=tt), grid=(b, t // tt),
        in_specs=[col(C_CV), col(C_CG),
                  _hist_spec(hist, layer, CONF_W - 1, BRANCH),
                  pl.BlockSpec((CONF_W, BRANCH), lambda i, j: (0, 0)), row, row, row],
        out_specs=[pl.BlockSpec((None, tt, BRANCH), lambda i, j: (i, j, 0)),
                   pl.BlockSpec((None, CONF_W - 1, BRANCH), lambda i, j: (i, 0, 0))],
        out_shape=[jax.ShapeDtypeStruct((b, t, BRANCH), F32),
                   jax.ShapeDtypeStruct((b, CONF_W - 1, BRANCH), F32)],
        scratch_shapes=scratch,
        compiler_params=_cparams(2), name="conformer")(
            z3, z3, hist, w, cb.reshape(1, BRANCH), lg.reshape(1, BRANCH), lb.reshape(1, BRANCH))


def _gdn_pre_kernel(x_ref, hist_ref, w_ref, q_ref, k_ref, v_ref, st_ref, ext_ref, *, tt):
    _conv_load_tile(x_ref[...].astype(F32), hist_ref, ext_ref, tt, GDN_W)
    rb = min(tt, 128)
    for r0 in range(0, tt, rb):
        for part, dst in enumerate((q_ref, k_ref, v_ref)):
            y = _silu(_conv_rows(ext_ref, w_ref, r0, rb, GDN_W, c0=part * BRANCH, ncols=BRANCH))
            if part == 2:
                dst[r0:r0 + rb, :] = y
                continue
            scale = GDN_DK ** -0.5 if part == 0 else 1.0
            for h in range(GDN_HEADS):
                blk = y[:, h * GDN_DK:(h + 1) * GDN_DK]
                nrm = lax.rsqrt(jnp.sum(blk * blk, axis=-1, keepdims=True) + 1e-6)
                dst[r0:r0 + rb, h * GDN_DK:(h + 1) * GDN_DK] = blk * nrm * scale
    _conv_state(ext_ref, st_ref, tt, GDN_W)


def _gdn_pre(z3, hist, w, tt, layer=0):
    b, t, _ = z3.shape
    wq = 3 * BRANCH
    out = pl.BlockSpec((None, tt, BRANCH), lambda i, j: (i, j, 0))
    return pl.pallas_call(
        functools.partial(_gdn_pre_kernel, tt=tt), grid=(b, t // tt),
        in_specs=[pl.BlockSpec((None, tt, wq), lambda i, j: (i, j, C_GQKV // wq)),
                  _hist_spec(hist, layer, GDN_W - 1, wq),
                  pl.BlockSpec((GDN_W, wq), lambda i, j: (0, 0))],
        out_specs=[out, out, out, pl.BlockSpec((None, GDN_W - 1, wq), lambda i, j: (i, 0, 0))],
        out_shape=[jax.ShapeDtypeStruct((b, t, BRANCH), F32)] * 3 + [jax.ShapeDtypeStruct((b, GDN_W - 1, wq), F32)],
        scratch_shapes=[pltpu.VMEM((_halo_rows(GDN_W) + tt, wq), F32)],
        compiler_params=_cparams(2), name="gdn_pre")(z3, hist, w)


def _fox_bias_placement():
    pq = np.zeros((SMALL, BRANCH), np.float32)
    pk = np.zeros((SMALL, BRANCH), np.float32)
    for h in range(FOX_HEADS):
        base = (h // 2) * LANES + (FOX_DH if h % 2 == 0 else 0)
        for part in range(3):
            pq[8 * (part + 1) + h, base + part] = 1.0
            pq[0, base + 3 + part] = 1.0
            pk[0, base + part] = 1.0
            pk[8 * (part + 1) + h, base + 3 + part] = -1.0
    return jnp.asarray(pq, BF16), jnp.asarray(pk, BF16)


def _gates_kernel(zs_ref, bias_ref, alog_ref, *rest, tt, with_t):
    if with_t:
        pq_ref, pk_ref, p_ref, pt_ref, qa_ref, ka_ref, carry_ref = rest
    else:
        p_ref, carry_ref = rest
    zs = zs_ref[...]
    lane = lax.broadcasted_iota(jnp.int32, zs.shape, 1)
    tb = zs + bias_ref[...]
    g_log = -jnp.exp(alog_ref[...]) * _softplus(tb)
    beta = jax.nn.sigmoid(zs)
    flog = -_softplus(-tb)
    is_f = (lane >= 8) & (lane < 16)
    f_only = jnp.where(is_f, flog, 0.0)
    if tt > 1:
        r = lax.broadcasted_iota(jnp.int32, (tt, tt), 0)
        c = lax.broadcasted_iota(jnp.int32, (tt, tt), 1)
        csum = _dot(jnp.where(r >= c, 1.0, 0.0).astype(F32), f_only, precision=HIGHEST)
    else:
        csum = f_only

    @pl.when(pl.program_id(1) == 0)
    def _():
        carry_ref[...] = jnp.zeros_like(carry_ref)

    csum = csum + carry_ref[...]
    carry_ref[...] = csum[tt - 1:tt, :]
    p = jnp.where(lane < 4, g_log, jnp.where(lane < 8, beta, f_only)) + pltpu.roll(csum, 8, axis=1)
    p_ref[...] = p
    if with_t:
        pt_ref[...] = p.T
        hi = csum.astype(BF16).astype(F32)
        r1 = csum - hi
        mid = r1.astype(BF16).astype(F32)
        lo = (r1 - mid).astype(BF16).astype(F32)
        c3 = (hi + pltpu.roll(mid, 8, axis=1) + pltpu.roll(lo, 16, axis=1) + jnp.where(lane == 0, 1.0, 0.0)).astype(BF16)
        qa_ref[...] = _dot(c3, pq_ref[...]).astype(BF16)
        ka_ref[...] = _dot(c3, pk_ref[...]).astype(BF16)


def _gates(zs3, bias_row, alog_row, tt):
    b, t, _ = zs3.shape
    with_t = tt % LANES == 0
    row = pl.BlockSpec((1, SMALL), lambda i, j: (0, 0))
    in_specs = [pl.BlockSpec((None, tt, SMALL), lambda i, j: (i, j, 0)), row, row]
    args = [zs3, bias_row, alog_row]
    out_specs = [pl.BlockSpec((None, tt, SMALL), lambda i, j: (i, j, 0))]
    out_shape = [jax.ShapeDtypeStruct((b, t, SMALL), F32)]
    if with_t:
        place = pl.BlockSpec((SMALL, BRANCH), lambda i, j: (0, 0))
        aug = pl.BlockSpec((None, tt, BRANCH), lambda i, j: (i, j, 0))
        in_specs += [place, place]
        args += list(_fox_bias_placement())
        out_specs += [pl.BlockSpec((None, SMALL, tt), lambda i, j: (i, 0, j)), aug, aug]
        out_shape += [jax.ShapeDtypeStruct((b, SMALL, t), F32)] + [jax.ShapeDtypeStruct((b, t, BRANCH), BF16)] * 2
    res = pl.pallas_call(
        functools.partial(_gates_kernel, tt=tt, with_t=with_t), grid=(b, t // tt),
        in_specs=in_specs, out_specs=out_specs, out_shape=out_shape,
        scratch_shapes=[pltpu.VMEM((1, SMALL), F32)],
        compiler_params=_cparams(2), name="gates")(*args)
    return res if with_t else (res[0], None, None, None)


def _unit_lower_inverse_many(ns, c):
    r = lax.broadcasted_iota(jnp.int32, (c, c), 0)
    q = lax.broadcasted_iota(jnp.int32, (c, c), 1)
    mm = lambda a, b: _dot(a.astype(BF16), b.astype(BF16))
    blk8 = (r >> 3) == (q >> 3)
    eye = jnp.where(r == q, 1.0, 0.0)
    n0s = [jnp.where(blk8, n, 0.0) for n in ns]
    xs = [eye - n0 for n0 in n0s]
    ms = [mm(n0, n0) for n0 in n0s]
    xs = [x + mm(x, m) for x, m in zip(xs, ms)]
    ms = [mm(m, m) for m in ms]
    xs = [x + mm(x, m) for x, m in zip(xs, ms)]
    s = 3
    while (1 << s) < c:
        lower_left = ((r >> (s + 1)) == (q >> (s + 1))) & ((r >> s) != (q >> s))
        ts = [mm(x, jnp.where(lower_left, n, 0.0)) for x, n in zip(xs, ns)]
        xs = [x - mm(t, x) for x, t in zip(xs, ts)]
        s += 1
    return xs


def _gdn_chunk_kernel(q_ref, k_ref, v_ref, p_ref, pt_ref, gz_ref, nw_ref, s0_ref, o_ref, so_ref,
                      s_ref, u_ref, w_ref, qd_ref, kdt_ref, qk_ref, *, tt):
    c = GDN_CHUNK
    nc = tt // c

    @pl.when(pl.program_id(1) == 0)
    def _():
        s_ref[...] = s0_ref[...]

    r = lax.broadcasted_iota(jnp.int32, (c, c), 0)
    q_i = lax.broadcasted_iota(jnp.int32, (c, c), 1)
    tril = r >= q_i
    strict = r > q_i
    tri_f = jnp.where(tril, 1.0, 0.0).astype(F32)
    triu_f = jnp.where(r <= q_i, 1.0, 0.0).astype(F32)
    probs = [(ci, h) for ci in range(nc) for h in range(GDN_HEADS)]
    rows_of = lambda ci: slice(ci * c, (ci + 1) * c)
    cols_of = lambda h: slice(h * GDN_DK, (h + 1) * GDN_DK)

    d_cols = [_dot(tri_f, p_ref[rows_of(ci), :], precision=HIGHEST) for ci in range(nc)]
    d_rows = [_dot(pt_ref[:, rows_of(ci)], triu_f, precision=HIGHEST) for ci in range(nc)]
    ns, rhss, dls = [], [], []
    for ci, h in probs:
        rows, hs = rows_of(ci), cols_of(h)
        qc, kc, vc = q_ref[rows, hs], k_ref[rows, hs], v_ref[rows, hs]
        beta = p_ref[rows, 4 + h:5 + h]
        dcol = d_cols[ci][:, h:h + 1]
        drow = d_rows[ci][h:h + 1, :]
        decay = jnp.where(tril, jnp.exp(jnp.where(tril, dcol - drow, 0.0)), 0.0)
        kb = kc * beta
        kbf = kc.astype(BF16)
        ns.append(jnp.where(strict, _dot_nt(kb.astype(BF16), kbf) * decay, 0.0))
        qk_ref[ci * GDN_HEADS + h] = jnp.where(tril, _dot_nt(qc.astype(BF16), kbf) * decay, 0.0).astype(BF16)
        ed = jnp.exp(dcol)
        dlast = dcol[c - 1:c, :]
        qd_ref[rows, hs] = (qc * ed).astype(BF16)
        kdt_ref[ci * GDN_HEADS + h] = (kc * jnp.exp(dlast - dcol)).T.astype(BF16)
        rhss.append(jnp.concatenate([vc * beta, kb * ed], axis=1).astype(BF16))
        dls.append(jnp.exp(dlast))
    t_invs = _unit_lower_inverse_many(ns, c)
    for (ci, h), t_inv, rhs in zip(probs, t_invs, rhss):
        sol = _dot(t_inv.astype(BF16), rhs)
        u_ref[rows_of(ci), cols_of(h)] = sol[:, :GDN_DK]
        w_ref[rows_of(ci), cols_of(h)] = sol[:, GDN_DK:].astype(BF16)

    for ci in range(nc):
        rows = rows_of(ci)
        s_olds = [s_ref[h] for h in range(GDN_HEADS)]
        s_bfs = [s.astype(BF16) for s in s_olds]
        vns = [(u_ref[rows, cols_of(h)] - _dot(w_ref[rows, cols_of(h)], s_bfs[h])).astype(BF16)
               for h in range(GDN_HEADS)]
        for h in range(GDN_HEADS):
            hs = cols_of(h)
            idx = ci * GDN_HEADS + h
            o = _dot(qd_ref[rows, hs], s_bfs[h]) + _dot(qk_ref[idx], vns[h])
            s_ref[h] = s_olds[h] * dls[idx] + _dot(kdt_ref[idx], vns[h])
            o_ref[rows, hs] = _rms(o, nw_ref[...]) * _silu(gz_ref[rows, hs].astype(F32))
    so_ref[...] = s_ref[...]


def _gdn_chunked(q, k, v, p, pt, z3, nw, s0, tt):
    b, t, _ = q.shape
    blk = pl.BlockSpec((None, tt, BRANCH), lambda i, j: (i, j, 0))
    st = pl.BlockSpec((None, GDN_HEADS, GDN_DK, GDN_DK), lambda i, j: (i, 0, 0, 0))
    return pl.pallas_call(
        functools.partial(_gdn_chunk_kernel, tt=tt), grid=(b, t // tt),
        in_specs=[blk, blk, blk,
                  pl.BlockSpec((None, tt, SMALL), lambda i, j: (i, j, 0)),
                  pl.BlockSpec((None, SMALL, tt), lambda i, j: (i, 0, j)),
                  pl.BlockSpec((None, tt, BRANCH), lambda i, j: (i, j, C_GZ // BRANCH)),
                  pl.BlockSpec((1, GDN_DK), lambda i, j: (0, 0)), st],
        out_specs=[blk, st],
        out_shape=[jax.ShapeDtypeStruct((b, t, BRANCH), F32),
                   jax.ShapeDtypeStruct((b, GDN_HEADS, GDN_DK, GDN_DK), F32)],
        scratch_shapes=[pltpu.VMEM((GDN_HEADS, GDN_DK, GDN_DK), F32),
                        pltpu.VMEM((tt, BRANCH), F32), pltpu.VMEM((tt, BRANCH), BF16), pltpu.VMEM((tt, BRANCH), BF16),
                        pltpu.VMEM((tt // GDN_CHUNK * GDN_HEADS, GDN_DK, GDN_CHUNK), BF16),
                        pltpu.VMEM((tt // GDN_CHUNK * GDN_HEADS, GDN_CHUNK, GDN_CHUNK), BF16)],
        compiler_params=_cparams(2), name="gdn_chunked")(q, k, v, p, pt, z3, nw.reshape(1, GDN_DK), s0)


def _gdn_step_kernel(q_ref, k_ref, v_ref, p_ref, gz_ref, nw_ref, s0_ref, o_ref, so_ref):
    r8 = lax.broadcasted_iota(jnp.int32, (8, GDN_DK), 0)
    rr = lax.broadcasted_iota(jnp.int32, (GDN_DK, GDN_DK), 0)
    cc = lax.broadcasted_iota(jnp.int32, (GDN_DK, GDN_DK), 1)
    p = p_ref[...]
    for h in range(GDN_HEADS):
        hs = slice(h * GDN_DK, (h + 1) * GDN_DK)
        q, k, v = q_ref[:, hs], k_ref[:, hs], v_ref[:, hs]
        g = p[:, h:h + 1]
        beta = p[:, 4 + h:5 + h]
        a = jnp.exp(g)
        w = k * beta * a
        s_old = s0_ref[h]
        s_bf = s_old.astype(BF16)
        lhs = jnp.where(r8 == 0, jnp.broadcast_to(w, (8, GDN_DK)),
                        jnp.where(r8 == 1, jnp.broadcast_to(q * a, (8, GDN_DK)), 0.0))
        ws = _dot(lhs.astype(BF16), s_bf)
        v_new = v * beta - ws[0:1, :]
        qk = jnp.sum(q.astype(BF16).astype(F32) * k.astype(BF16).astype(F32), axis=-1, keepdims=True)
        o = ws[1:2, :] + qk.astype(BF16).astype(F32) * v_new.astype(BF16).astype(F32)
        kdiag = jnp.where(rr == cc, jnp.broadcast_to(k, (GDN_DK, GDN_DK)), 0.0)
        outer = _dot(kdiag.astype(BF16), jnp.broadcast_to(v_new, (GDN_DK, GDN_DK)).astype(BF16))
        so_ref[h] = s_old * a + outer
        on = _rms(o, nw_ref[...])
        o_ref[:, hs] = on * _silu(gz_ref[:, hs])


def _gdn_step(q, k, v, p, z3, nw, state, layer):
    b = q.shape[0]
    blk = pl.BlockSpec((None, 1, BRANCH), lambda i: (i, 0, 0))
    st = pl.BlockSpec((None, GDN_HEADS, GDN_DK, GDN_DK), lambda i: (i, 0, 0, 0))
    st_in = pl.BlockSpec((None, None, GDN_HEADS, GDN_DK, GDN_DK), lambda i: (i, layer, 0, 0, 0))
    s0 = state
    return pl.pallas_call(
        _gdn_step_kernel, grid=(b,),
        in_specs=[blk, blk, blk, pl.BlockSpec((None, 1, SMALL), lambda i: (i, 0, 0)),
                  pl.BlockSpec((None, 1, BRANCH), lambda i: (i, 0, C_GZ // BRANCH)),
                  pl.BlockSpec((1, GDN_DK), lambda i: (0, 0)), st_in],
        out_specs=[blk, st],
        out_shape=[jax.ShapeDtypeStruct((b, 1, BRANCH), F32),
                   jax.ShapeDtypeStruct((b, GDN_HEADS, GDN_DK, GDN_DK), F32)],
        compiler_params=_cparams(1), name="gdn_step")(q, k, v, p, z3, nw.reshape(1, GDN_DK), s0)


def _fox_kernel(q_ref, k_ref, v_ref, qa_ref, ka_ref, o_ref, m_ref, acc_ref, *, tq, tk):
    qi, kj = pl.program_id(1), pl.program_id(2)

    @pl.when(kj == 0)
    def _():
        m_ref[...] = jnp.full(m_ref.shape, NEG_INF, F32)
        acc_ref[...] = jnp.zeros(acc_ref.shape, F32)

    def step(on_diagonal):
        lane = lax.broadcasted_iota(jnp.int32, (1, LANES), 1)
        if on_diagonal:
            causal = (lax.broadcasted_iota(jnp.int32, (tq, tk), 1) <= lax.broadcasted_iota(jnp.int32, (tq, tk), 0))
        for h in range(FOX_HEADS):
            ps = slice((h // 2) * LANES, (h // 2 + 1) * LANES)
            own = (lane >= (h % 2) * FOX_DH) & (lane < (h % 2 + 1) * FOX_DH)
            q2 = (q_ref[:, ps] * (FOX_DH ** -0.5)).astype(BF16)
            s = _dot_nt(jnp.where(own, q2, qa_ref[:, ps]), jnp.where(own, k_ref[:, ps].astype(BF16), ka_ref[:, ps]))
            if on_diagonal:
                s = jnp.where(causal, s, NEG_INF)
            m_old = m_ref[h]
            m_new = jnp.maximum(m_old, jnp.max(s, axis=-1, keepdims=True))
            pr = jnp.exp(s - pltpu.repeat(m_new, tk // LANES, axis=1))
            v1 = jnp.where(own, v_ref[:, ps].astype(BF16), jnp.ones((), BF16))
            acc_ref[h] = acc_ref[h] * jnp.exp(m_old - m_new) + _dot(pr.astype(BF16), v1)
            m_ref[h] = m_new

    @pl.when(kj < qi)
    def _():
        step(False)

    @pl.when(kj == qi)
    def _():
        step(True)
        first = lax.broadcasted_iota(jnp.int32, (1, LANES), 1) < FOX_DH
        for pair in range(FOX_HEADS // 2):
            a0, a1 = acc_ref[2 * pair], acc_ref[2 * pair + 1]
            o0 = a0 / pltpu.roll(a0, FOX_DH, axis=1)
            o1 = a1 / pltpu.roll(a1, FOX_DH, axis=1)
            o_ref[:, pair * LANES:(pair + 1) * LANES] = jnp.where(first, o0, o1)


def _fox_prompt(z3, qa, ka, tq):
    b, t, _ = z3.shape
    tk = tq
    nq = t // tq
    kv = lambda c: pl.BlockSpec((None, tk, BRANCH), lambda i, qi, kj, c=c: (i, jnp.minimum(kj, qi), c // BRANCH))
    return pl.pallas_call(
        functools.partial(_fox_kernel, tq=tq, tk=tk), grid=(b, nq, nq),
        in_specs=[pl.BlockSpec((None, tq, BRANCH), lambda i, qi, kj: (i, qi, C_FQ // BRANCH)),
                  kv(C_FK), kv(C_FV),
                  pl.BlockSpec((None, tq, BRANCH), lambda i, qi, kj: (i, qi, 0)),
                  pl.BlockSpec((None, tk, BRANCH), lambda i, qi, kj: (i, jnp.minimum(kj, qi), 0))],
        out_specs=pl.BlockSpec((None, tq, BRANCH), lambda i, qi, kj: (i, qi, 0)),
        out_shape=jax.ShapeDtypeStruct((b, t, BRANCH), F32),
        scratch_shapes=[pltpu.VMEM((FOX_HEADS, tq, LANES), F32), pltpu.VMEM((FOX_HEADS, tq, LANES), F32)],
        compiler_params=_cparams(3), name="fox_prompt")(z3, z3, z3, qa, ka)


def _fox_decode_kernel(pt_ref, q_ref, kc_ref, vc_ref, fc_ref, *refs):
    del pt_ref
    n = PAGES_PER_STEP
    k_refs, v_refs, lf_refs = refs[:n], refs[n:2 * n], refs[2 * n:3 * n]
    o_ref, qb_ref, m_ref, l_ref, acc_ref, run_ref = refs[3 * n:]
    j = pl.program_id(1)
    nj = pl.num_programs(1)
    lane = lax.broadcasted_iota(jnp.int32, (1, LANES), 1)
    row8 = lax.broadcasted_iota(jnp.int32, (FOX_HEADS, LANES), 0)

    def to_cols(row):
        return jnp.broadcast_to(row, (LANES, BRANCH)).T

    def head_sum(x):
        out = jnp.zeros((FOX_HEADS, LANES), F32)
        for h in range(FOX_HEADS):
            sh = jnp.sum(x[h * FOX_DH:(h + 1) * FOX_DH, :], axis=0, keepdims=True)
            out = jnp.where(row8 == h, jnp.broadcast_to(sh, (FOX_HEADS, LANES)), out)
        return out

    @pl.when(j == 0)
    def _():
        qb_ref[...] = to_cols(q_ref[...] * (FOX_DH ** -0.5))
        m_ref[...] = head_sum(qb_ref[...] * to_cols(kc_ref[...]))
        l_ref[...] = jnp.ones(l_ref.shape, F32)
        acc_ref[...] = jnp.where(lane == 0, to_cols(vc_ref[...]), 0.0)
        run_ref[...] = fc_ref[...]

    qb = qb_ref[...]
    lf = jnp.concatenate([lf_refs[i][...] for i in range(n)], axis=0)
    pr_ = lax.broadcasted_iota(jnp.int32, (LANES, LANES), 0)
    pc_ = lax.broadcasted_iota(jnp.int32, (LANES, LANES), 1)
    in_page = _dot(lf, jnp.where(pr_ > pc_, 1.0, 0.0).astype(F32), precision=HIGHEST)
    tot = jnp.sum(lf, axis=-1, keepdims=True)
    run = run_ref[...]
    logits = [None] * n
    for i in reversed(range(n)):
        s_i = head_sum(k_refs[i][...].reshape(BRANCH, LANES) * qb)
        logits[i] = s_i + in_page[8 * i:8 * (i + 1), :] + run
        run = run + tot[8 * i:8 * (i + 1), :]
    run_ref[...] = run
    mx = logits[0]
    for i in range(1, n):
        mx = jnp.maximum(mx, logits[i])
    m_old = m_ref[...]
    m_new = jnp.maximum(m_old, jnp.max(mx, axis=-1, keepdims=True))
    alpha = jnp.exp(m_old - m_new)
    ps = [jnp.exp(logits[i] - m_new) for i in range(n)]
    psum = ps[0]
    for i in range(1, n):
        psum = psum + ps[i]
    l_ref[...] = alpha * l_ref[...] + jnp.sum(psum, axis=-1, keepdims=True)
    m_ref[...] = m_new
    for h in range(FOX_HEADS):
        hs = slice(h * FOX_DH, (h + 1) * FOX_DH)
        a = acc_ref[hs, :] * alpha[h:h + 1, :]
        for i in range(n):
            a = a + v_refs[i][h] * ps[i][h:h + 1, :]
        acc_ref[hs, :] = a

    @pl.when(j == nj - 1)
    def _():
        l_all = l_ref[...]
        l_cols = jnp.concatenate([jnp.broadcast_to(l_all[h:h + 1, :], (FOX_DH, LANES)) for h in range(FOX_HEADS)],
                                 axis=0)
        o_cols = jnp.broadcast_to(jnp.sum(acc_ref[...], axis=-1, keepdims=True), (BRANCH, LANES)) / l_cols
        o_ref[...] = o_cols.T[0:1, :]


def _fox_decode(page_table, q, kc, vc, fc, cache_kt, cache_vt, cache_lft, layer):
    b, n_pages = page_table.shape
    nj = n_pages // PAGES_PER_STEP
    tok = pl.BlockSpec((None, 1, BRANCH), lambda i, j, pt: (i, 0, 0))
    pidx = lambda i, j, pt, k: pt[i, (nj - 1 - j) * PAGES_PER_STEP + k]
    page = lambda k: pl.BlockSpec((None, None, FOX_HEADS, FOX_DH, PAGE),
                                  lambda i, j, pt, k=k: (pidx(i, j, pt, k), layer, 0, 0, 0))
    lpage = lambda k: pl.BlockSpec((None, None, FOX_HEADS, PAGE),
                                   lambda i, j, pt, k=k: (pidx(i, j, pt, k), layer, 0, 0))
    rng = range(PAGES_PER_STEP)
    in_specs = ([tok, tok, tok, pl.BlockSpec((None, FOX_HEADS, LANES), lambda i, j, pt: (i, 0, 0))]
                + [page(k) for k in rng] * 2 + [lpage(k) for k in rng])
    return pl.pallas_call(
        _fox_decode_kernel,
        grid_spec=pltpu.PrefetchScalarGridSpec(
            num_scalar_prefetch=1, grid=(b, nj), in_specs=in_specs, out_specs=tok,
            scratch_shapes=[pltpu.VMEM((BRANCH, LANES), F32), pltpu.VMEM((FOX_HEADS, LANES), F32),
                            pltpu.VMEM((FOX_HEADS, LANES), F32), pltpu.VMEM((BRANCH, LANES), F32),
                            pltpu.VMEM((FOX_HEADS, LANES), F32)]),
        out_shape=jax.ShapeDtypeStruct((b, 1, BRANCH), F32),
        compiler_params=_cparams(2), name="fox_decode")(
            page_table, q, kc, vc, fc, *([cache_kt] * PAGES_PER_STEP), *([cache_vt] * PAGES_PER_STEP),
            *([cache_lft] * PAGES_PER_STEP))


def _merge_kernel(oa, ob, oc, od, g0, g1, g2, g3, x_ref, wb_ref, wo_ref, out_ref):
    merged = None
    for i, (o_ref, g_ref) in enumerate(((oa, g0), (ob, g1), (oc, g2), (od, g3))):
        term = jax.nn.sigmoid(g_ref[...].astype(F32)) * _dot(o_ref[...].astype(BF16), wb_ref[i])
        merged = term if merged is None else merged + term
    out_ref[...] = x_ref[...] + _dot(merged.astype(BF16), wo_ref[...])


def _merge(o_list, z2, x2, wb, wo, tm):
    m = x2.shape[0]
    o_spec = pl.BlockSpec((tm, BRANCH), lambda i: (i, 0))
    gate = lambda g: pl.BlockSpec((tm, D_MODEL), lambda i, g=g: (i, C_GATE // D_MODEL + g))
    xs = pl.BlockSpec((tm, D_MODEL), lambda i: (i, 0))
    return pl.pallas_call(
        _merge_kernel, grid=(m // tm,),
        in_specs=[o_spec] * 4 + [gate(g) for g in range(4)] + [
            xs, pl.BlockSpec((4, BRANCH, D_MODEL), lambda i: (0, 0, 0)),
            pl.BlockSpec((D_MODEL, D_MODEL), lambda i: (0, 0))],
        out_specs=xs, out_shape=jax.ShapeDtypeStruct((m, D_MODEL), F32),
        compiler_params=_cparams(1), name="merge")(*o_list, z2, z2, z2, z2, x2, wb, wo)


def _cross_kernel(x_ref, g_ref, wq_ref, mk_ref, mv_ref, wo_ref, out_ref):
    x = x_ref[...]
    q = _dot(_rms(x, g_ref[...]).astype(BF16), wq_ref[...])
    mk = mk_ref[...].astype(BF16)
    mv = mv_ref[...].astype(BF16)
    outs = []
    for h in range(X_HEADS):
        hs = slice(h * X_DH, (h + 1) * X_DH)
        s = _dot_nt(q[:, hs].astype(BF16), mk[:, hs]) * (X_DH ** -0.5)
        e = jnp.exp(s - jnp.max(s, axis=-1, keepdims=True))
        pr = e / jnp.sum(e, axis=-1, keepdims=True)
        outs.append(_dot(pr.astype(BF16), mv[:, hs]))
    out_ref[...] = x + _dot(jnp.concatenate(outs, axis=1).astype(BF16), wo_ref[...])


def _cross(x3, g, wq, mk_arr, mk_spec, mv_arr, mv_spec, wo, tt):
    b, t, _ = x3.shape
    xw = X_HEADS * X_DH
    xs = pl.BlockSpec((None, tt, D_MODEL), lambda i, j: (i, j, 0))
    return pl.pallas_call(
        _cross_kernel, grid=(b, t // tt),
        in_specs=[xs, pl.BlockSpec((1, D_MODEL), lambda i, j: (0, 0)),
                  pl.BlockSpec((D_MODEL, xw), lambda i, j: (0, 0)), mk_spec, mv_spec,
                  pl.BlockSpec((xw, D_MODEL), lambda i, j: (0, 0))],
        out_specs=xs, out_shape=jax.ShapeDtypeStruct((b, t, D_MODEL), F32),
        compiler_params=_cparams(2), name="cross_attn")(x3, g.reshape(1, D_MODEL), wq, mk_arr, mv_arr, wo)


def _ffn_kernel(x_ref, g_ref, wg_ref, wu_ref, cw_ref, wd_ref, hist_ref, out_ref, st_ref,
                xn_ref, acc_ref, halo_ref, ext_ref, *, tt, tf, nf):
    t, f = pl.program_id(1), pl.program_id(2)
    hp = _halo_rows(FFN_W)

    @pl.when(f == 0)
    def _():
        xn_ref[...] = _rms(x_ref[...], g_ref[...]).astype(BF16)
        acc_ref[...] = jnp.zeros(acc_ref.shape, F32)

    gcol = _dot(xn_ref[...], wg_ref[...])
    u = _dot(xn_ref[...], wu_ref[...])

    @pl.when(t == 0)
    def _():
        ext_ref[0:hp, :] = jnp.zeros((hp, ext_ref.shape[1]), F32)
        ext_ref[hp - (FFN_W - 1):hp, :] = hist_ref[...]

    @pl.when(t > 0)
    def _():
        ext_ref[0:hp, :] = halo_ref[f]

    ext_ref[hp:hp + tt, :] = gcol
    halo_ref[f] = ext_ref[tt:tt + hp, :]
    rb = min(tt, 256)
    for r0 in range(0, tt, rb):
        gc = _conv_rows(ext_ref, cw_ref, r0, rb, FFN_W)
        act = _silu(gc) * u[r0:r0 + rb, :]
        acc_ref[r0:r0 + rb, :] += _dot(act.astype(BF16), wd_ref[...])

    @pl.when(f == nf - 1)
    def _():
        out_ref[...] = x_ref[...] + acc_ref[...]

    @pl.when((f == nf - 1) & (t == pl.num_programs(1) - 1))
    def _():
        for ff in range(nf):
            st_ref[:, ff * tf:(ff + 1) * tf] = halo_ref[ff][hp - (FFN_W - 1):hp, :]


def _ffn(x3, g, w_in, cw, w_out, hist, tt, tf):
    b, t, _ = x3.shape
    nf = D_FF // tf
    xs = pl.BlockSpec((None, tt, D_MODEL), lambda i, j, f: (i, j, 0))
    st = pl.BlockSpec((None, FFN_W - 1, tf), lambda i, j, f: (i, 0, f))
    st_out = pl.BlockSpec((None, FFN_W - 1, D_FF), lambda i, j, f: (i, 0, 0))
    return pl.pallas_call(
        functools.partial(_ffn_kernel, tt=tt, tf=tf, nf=nf), grid=(b, t // tt, nf),
        in_specs=[xs, pl.BlockSpec((1, D_MODEL), lambda i, j, f: (0, 0)),
                  pl.BlockSpec((D_MODEL, tf), lambda i, j, f: (0, f)),
                  pl.BlockSpec((D_MODEL, tf), lambda i, j, f: (0, nf + f)),
                  pl.BlockSpec((FFN_W, tf), lambda i, j, f: (0, f)),
                  pl.BlockSpec((tf, D_MODEL), lambda i, j, f: (f, 0)), st],
        out_specs=[xs, st_out],
        out_shape=[jax.ShapeDtypeStruct((b, t, D_MODEL), F32), jax.ShapeDtypeStruct((b, FFN_W - 1, D_FF), F32)],
        scratch_shapes=[pltpu.VMEM((tt, D_MODEL), BF16), pltpu.VMEM((tt, D_MODEL), F32),
                        pltpu.VMEM((nf, _halo_rows(FFN_W), tf), F32),
                        pltpu.VMEM((_halo_rows(FFN_W) + tt, tf), F32)],
        compiler_params=_cparams(3), name="conv_ffn")(x3, g.reshape(1, D_MODEL), w_in, w_in, cw, w_out, hist)


def _ffn_step_kernel(x_ref, g_ref, wg_ref, wu_ref, cw_ref, wd_ref, h0_ref, h1_ref, out_ref, s0_ref, s1_ref,
                     xn_ref, acc_ref):
    f = pl.program_id(0)

    @pl.when(f == 0)
    def _():
        xn_ref[...] = _rms(x_ref[...], g_ref[...]).astype(BF16)
        acc_ref[...] = jnp.zeros(acc_ref.shape, F32)

    gcol = _dot(xn_ref[...], wg_ref[...])
    u = _dot(xn_ref[...], wu_ref[...])
    gc = cw_ref[0:1, :] * h0_ref[...] + cw_ref[1:2, :] * h1_ref[...] + cw_ref[2:3, :] * gcol
    s0_ref[...] = h1_ref[...]
    s1_ref[...] = gcol
    acc_ref[...] += _dot((_silu(gc) * u).astype(BF16), wd_ref[...])

    @pl.when(f == pl.num_programs(0) - 1)
    def _():
        out_ref[...] = x_ref[...] + acc_ref[...]


def _ffn_step(x2, g, w_in, cw, w_out, h0, h1, tf):
    b = x2.shape[0]
    nf = D_FF // tf
    xs = pl.BlockSpec((b, D_MODEL), lambda f: (0, 0))
    hs = pl.BlockSpec((b, tf), lambda f: (0, f))
    return pl.pallas_call(
        _ffn_step_kernel, grid=(nf,),
        in_specs=[xs, pl.BlockSpec((1, D_MODEL), lambda f: (0, 0)),
                  pl.BlockSpec((D_MODEL, tf), lambda f: (0, f)),
                  pl.BlockSpec((D_MODEL, tf), lambda f: (0, nf + f)),
                  pl.BlockSpec((FFN_W, tf), lambda f: (0, f)),
                  pl.BlockSpec((tf, D_MODEL), lambda f: (f, 0)), hs, hs],
        out_specs=[xs, hs, hs],
        out_shape=[jax.ShapeDtypeStruct((b, D_MODEL), F32), jax.ShapeDtypeStruct((b, D_FF), F32),
                   jax.ShapeDtypeStruct((b, D_FF), F32)],
        scratch_shapes=[pltpu.VMEM((b, D_MODEL), BF16), pltpu.VMEM((b, D_MODEL), F32)],
        compiler_params=_cparams(1), name="conv_ffn_step")(x2, g.reshape(1, D_MODEL), w_in, w_in, cw, w_out, h0, h1)


def _kv_out_kernel(k_ref, v_ref, *refs):
    kt_ref, vt_ref = refs[-2:]
    kt_ref[...] = k_ref[...].astype(F32).T
    vt_ref[...] = v_ref[...].astype(F32).T


def _kv_out(z3, layer, depth, prev, tt):
    b, t, _ = z3.shape
    col = lambda c: pl.BlockSpec((None, tt, BRANCH), lambda i, j, c=c: (i, j, c // BRANCH))
    out = pl.BlockSpec((None, None, BRANCH, tt), lambda i, j: (i, layer, 0, j))
    shape = jax.ShapeDtypeStruct((b, depth, BRANCH, t), F32)
    in_specs, args, aliases = [col(C_FK), col(C_FV)], [z3, z3], {}
    if prev is not None:
        in_specs += [pl.BlockSpec(memory_space=pl.ANY)] * 2
        args += list(prev)
        aliases = {2: 0, 3: 1}
    return pl.pallas_call(
        _kv_out_kernel, grid=(b, t // tt), in_specs=in_specs, out_specs=[out, out], out_shape=[shape, shape],
        input_output_aliases=aliases, compiler_params=_cparams(2), name="kv_out")(*args)


def _final_norm_kernel(x_ref, g_ref, o_ref):
    o_ref[...] = _rms(x_ref[...], g_ref[...])


def _final_norm(x2, g, tm):
    m = x2.shape[0]
    xs = pl.BlockSpec((tm, D_MODEL), lambda i: (i, 0))
    return pl.pallas_call(
        _final_norm_kernel, grid=(m // tm,),
        in_specs=[xs, pl.BlockSpec((1, D_MODEL), lambda i: (0, 0))], out_specs=xs,
        out_shape=jax.ShapeDtypeStruct((m, D_MODEL), F32),
        compiler_params=_cparams(1), name="final_norm")(x2, g.reshape(1, D_MODEL))


def _prep_layer(l, w_in, w_branch, w_out, w_cq, w_co, w_ffn_in, w_ffn_out, gdn_a_log, gdn_dt_bias, fox_fbias):
    w = w_in[l]
    w_main = jnp.concatenate([w[:, :3584], w[:, 3592:6152], w[:, 6160:]], axis=1).astype(BF16)
    w_small = jnp.concatenate([w[:, 3584:3592], w[:, 6152:6160],
                               jnp.zeros((D_MODEL, SMALL - 16), F32)], axis=1).astype(BF16)
    zeros4 = jnp.zeros((4,), F32)
    bias_row = jnp.concatenate([gdn_dt_bias[l], zeros4, fox_fbias[l], jnp.zeros((SMALL - 16,), F32)]).reshape(1, SMALL)
    alog_row = jnp.concatenate([gdn_a_log[l], jnp.zeros((SMALL - 4,), F32)]).reshape(1, SMALL)
    return dict(w_main=w_main, w_small=w_small, bias_row=bias_row, alog_row=alog_row,
                wb=w_branch[l].astype(BF16), wo=w_out[l].astype(BF16), wcq=w_cq[l].astype(BF16),
                wco=w_co[l].astype(BF16), wfi=w_ffn_in[l].astype(BF16), wfo=w_ffn_out[l].astype(BF16))


def kernel(x_prompt, x_sample, cache_fox_k, cache_fox_v, cache_fox_logf, cache_mem_k, cache_mem_v, state_sconv, state_gdn_conv, state_gdn, state_conf_conv, state_ffn_conv, page_table, mem_prompt, norm_mix, w_in, w_branch, w_out, sconv_w, gdn_conv_w, gdn_a_log, gdn_dt_bias, gdn_norm, conf_dw_w, conf_dw_b, conf_ln_g, conf_ln_b, fox_fbias, norm_cross, norm_mem, w_cq, w_ckv, w_co, norm_ffn, w_ffn_in, ffn_conv_w, w_ffn_out, norm_final):
    depth = w_in.shape[0]
    bp, t, _ = x_prompt.shape
    bs = x_sample.shape[0]
    n_pool = cache_fox_k.shape[0]
    xw = X_HEADS * X_DH
    prep = [_prep_layer(l, w_in, w_branch, w_out, w_cq, w_co, w_ffn_in, w_ffn_out, gdn_a_log, gdn_dt_bias, fox_fbias)
            for l in range(depth)]

    tt = min(t, 512)
    mp = bp * t
    x = x_prompt.reshape(mp, D_MODEL)
    mem2 = mem_prompt.reshape(bp * MEM_LEN, D_MODEL)
    p_out = {k: [] for k in ("fk", "fv", "fl", "mk", "mv", "sc", "gc", "gs", "cc", "ff")}
    zero = lambda *s: jnp.zeros(s, F32)
    kt_vt = None
    for l in range(depth):
        w = prep[l]
        z, zs = _rms_matmul(x, norm_mix[l], w["w_main"], w["w_small"], tm=min(mp, 2048), tn=1024, out_dtype=BF16)
        z3 = z.reshape(bp, t, NZ)
        o_a, st_sc = _sconv(z3, zero(bp, SC_W - 1, BRANCH), sconv_w[l], tt)
        q, k, v, st_gc = _gdn_pre(z3, zero(bp, GDN_W - 1, 3 * BRANCH), gdn_conv_w[l], tt)
        p, pt, qa, ka = _gates(zs.reshape(bp, t, SMALL), w["bias_row"], w["alog_row"], tt)
        o_b, st_gs = _gdn_chunked(q, k, v, p, pt, z3, gdn_norm[l], zero(bp, GDN_HEADS, GDN_DK, GDN_DK), tt)
        o_c, st_cc = _conformer(z3, zero(bp, CONF_W - 1, BRANCH), conf_dw_w[l], conf_dw_b[l],
                                conf_ln_g[l], conf_ln_b[l], tt)
        o_d = _fox_prompt(z3, qa, ka, tt)
        o_list = [o.reshape(mp, BRANCH) for o in (o_a, o_b, o_c, o_d)]
        x = _merge(o_list, z, x, w["wb"], w["wo"], tm=min(mp, 256))
        kv = _rms_matmul(mem2, norm_mem[l], w_ckv[l].astype(BF16), None, tm=min(bp * MEM_LEN, 1024), tn=512)
        kv3 = kv.reshape(bp, MEM_LEN, 2 * xw)
        x = _cross(x.reshape(bp, t, D_MODEL), norm_cross[l], w["wcq"],
                   kv3, pl.BlockSpec((None, MEM_LEN, xw), lambda i, j: (i, 0, 0)),
                   kv3, pl.BlockSpec((None, MEM_LEN, xw), lambda i, j: (i, 0, 1)), w["wco"], tt)
        x, st_ff = _ffn(x, norm_ffn[l], w["wfi"], ffn_conv_w[l], w["wfo"], zero(bp, FFN_W - 1, D_FF),
                        tt=min(t, 512), tf=D_FF // 2)
        x = x.reshape(mp, D_MODEL)
        kt_vt = _kv_out(z3, l, depth, kt_vt, tt)
        p_out["fl"].append(pt[:, 8:16, :])
        p_out["mk"].append(kv3[:, :, :xw])
        p_out["mv"].append(kv3[:, :, xw:])
        for key, val in (("sc", st_sc), ("gc", st_gc), ("gs", st_gs), ("cc", st_cc), ("ff", st_ff)):
            p_out[key].append(val)
    y_prompt = _final_norm(x, norm_final, tm=min(mp, 1024)).reshape(bp, t, D_MODEL)

    ckt = jnp.transpose(cache_fox_k, (0, 2, 3, 4, 1))
    cvt = jnp.transpose(cache_fox_v, (0, 2, 3, 4, 1))
    clt = jnp.transpose(cache_fox_logf, (0, 2, 3, 1))
    cmk = cache_mem_k.reshape(bs, depth, MEM_LEN, xw)
    cmv = cache_mem_v.reshape(bs, depth, MEM_LEN, xw)
    xs_ = x_sample.reshape(bs, D_MODEL)
    s_out = {k: [] for k in ("fk", "fv", "fl", "sc", "gc", "gs", "cc", "ff")}
    for l in range(depth):
        w = prep[l]
        z, zs = _rms_matmul(xs_, norm_mix[l], w["w_main"], w["w_small"], tm=bs, tn=2048)
        z3 = z.reshape(bs, 1, NZ)
        o_a, st_sc = _sconv(z3, state_sconv, sconv_w[l], 1, layer=l)
        q, k, v, st_gc = _gdn_pre(z3, state_gdn_conv, gdn_conv_w[l], 1, layer=l)
        p = _gates(zs.reshape(bs, 1, SMALL), w["bias_row"], w["alog_row"], 1)[0]
        o_b, st_gs = _gdn_step(q, k, v, p, z3, gdn_norm[l], state_gdn, l)
        o_c, st_cc = _conformer(z3, state_conf_conv, conf_dw_w[l], conf_dw_b[l], conf_ln_g[l], conf_ln_b[l], 1,
                                layer=l)
        fq, fk, fv = (z3[:, :, c:c + BRANCH] for c in (C_FQ, C_FK, C_FV))
        flog = p[:, :, 8:16]
        fc = jnp.broadcast_to(jnp.swapaxes(flog, 1, 2), (bs, FOX_HEADS, LANES))
        o_d = _fox_decode(page_table, fq, fk, fv, fc, ckt, cvt, clt, l)
        o_list = [o.reshape(bs, BRANCH) for o in (o_a, o_b, o_c, o_d)]
        xs_ = _merge(o_list, z, xs_, w["wb"], w["wo"], tm=bs)
        xs_ = _cross(xs_.reshape(bs, 1, D_MODEL), norm_cross[l], w["wcq"],
                     cmk, pl.BlockSpec((None, None, MEM_LEN, xw), lambda i, j, l=l: (i, l, 0, 0)),
                     cmv, pl.BlockSpec((None, None, MEM_LEN, xw), lambda i, j, l=l: (i, l, 0, 0)), w["wco"], 1)
        xs_, h_a, h_b = _ffn_step(xs_.reshape(bs, D_MODEL), norm_ffn[l], w["wfi"], ffn_conv_w[l], w["wfo"],
                                  state_ffn_conv[:, l, 0], state_ffn_conv[:, l, 1], tf=256)
        s_out["fk"].append(fk)
        s_out["fv"].append(fv)
        s_out["fl"].append(flog)
        for key, val in (("sc", st_sc), ("gc", st_gc), ("gs", st_gs), ("cc", st_cc),
                         ("ff", jnp.stack([h_a, h_b], axis=1))):
            s_out[key].append(val)
    y_sample = _final_norm(xs_, norm_final, tm=bs).reshape(bs, 1, D_MODEL)

    heads = lambda a: a.reshape(a.shape[0], a.shape[1], depth, FOX_HEADS, FOX_DH)
    memh = lambda a: a.reshape(bp, depth, MEM_LEN, X_HEADS, X_DH)
    from_pos_minor = lambda a: jnp.transpose(a.reshape(bp, depth, FOX_HEADS, FOX_DH, t), (0, 4, 1, 2, 3))
    return (y_prompt, y_sample,
            from_pos_minor(kt_vt[0]), from_pos_minor(kt_vt[1]),
            jnp.transpose(jnp.stack(p_out["fl"], axis=1), (0, 3, 1, 2)),
            memh(jnp.stack(p_out["mk"], axis=1)), memh(jnp.stack(p_out["mv"], axis=1)),
            jnp.stack(p_out["sc"], axis=1), jnp.stack(p_out["gc"], axis=1), jnp.stack(p_out["gs"], axis=1),
            jnp.stack(p_out["cc"], axis=1), jnp.stack(p_out["ff"], axis=1),
            heads(jnp.stack(s_out["fk"], axis=2)), heads(jnp.stack(s_out["fv"], axis=2)),
            jnp.stack(s_out["fl"], axis=2),
            jnp.stack(s_out["sc"], axis=1), jnp.stack(s_out["gc"], axis=1), jnp.stack(s_out["gs"], axis=1),
            jnp.stack(s_out["cc"], axis=1), jnp.stack(s_out["ff"], axis=1))
```

```python
import functools

import jax
import jax.numpy as jnp
import numpy as np
from jax import lax
from jax.experimental import pallas as pl
from jax.experimental.pallas import tpu as pltpu

F32 = jnp.float32
BF16 = jnp.bfloat16

D_MODEL = 1024
BRANCH = 512
GDN_HEADS = 4
GDN_DK = 128
FOX_HEADS = 8
FOX_DH = 64
X_HEADS = 4
X_DH = 128
D_FF = 2816
MEM_LEN = 256
PAGE = 128
SC_W, GDN_W, CONF_W, FFN_W = 3, 4, 31, 3
NEG_INF = -1e30

C_AH, C_AB, C_AC = 0, 512, 1024
C_GQKV = 1536
C_GZ = 3072
C_CV, C_CG = 3584, 4096
C_FQ, C_FK, C_FV = 4608, 5120, 5632
C_GATE = 6144
NZ = 10240
SMALL = 128
LANES = 128
GDN_CHUNK = 128
PAGES_PER_STEP = 16


def _cparams(n_axes, vmem_mb=48):
    return pltpu.CompilerParams(dimension_semantics=("arbitrary",) * n_axes,
                                vmem_limit_bytes=vmem_mb * 1024 * 1024)


def _dot(a, b, precision=None):
    return jnp.dot(a, b, preferred_element_type=F32, precision=precision)


def _dot_nt(a, b, precision=None):
    return lax.dot_general(a, b, (((1,), (1,)), ((), ())), preferred_element_type=F32, precision=precision)


def _split3(x):
    hi = x.astype(BF16)
    r1 = x - hi.astype(F32)
    mid = r1.astype(BF16)
    return hi, mid, (r1 - mid.astype(F32)).astype(BF16)


def _dot_sel(sel01, x):
    s = sel01.astype(BF16)
    hi, mid, lo = _split3(x)
    return _dot(s, hi) + _dot(s, mid) + _dot(s, lo)


def _dot_sel_rhs(x, sel01):
    s = sel01.astype(BF16)
    hi, mid, lo = _split3(x)
    return _dot(hi, s) + _dot(mid, s) + _dot(lo, s)


def _rms(x, g, eps=1e-6):
    return (x * lax.rsqrt(jnp.mean(x * x, axis=-1, keepdims=True) + eps)) * g


def _softplus(t):
    return jnp.maximum(t, 0.0) + jnp.log1p(jnp.exp(-jnp.abs(t)))


def _silu(t):
    return t * jax.nn.sigmoid(t)


def _rms_mm_kernel(*refs, has_small):
    if has_small:
        x_ref, g_ref, w_ref, ws_ref, o_ref, os_ref, xn_ref = refs
    else:
        x_ref, g_ref, w_ref, o_ref, xn_ref = refs

    @pl.when(pl.program_id(1) == 0)
    def _():
        xn_ref[...] = _rms(x_ref[...], g_ref[...]).astype(BF16)
        if has_small:
            os_ref[...] = _dot(xn_ref[...], ws_ref[...])

    o_ref[...] = _dot(xn_ref[...], w_ref[...]).astype(o_ref.dtype)


def _rms_matmul(x, g, w, w_small, tm, tn, out_dtype=F32):
    m, k = x.shape
    n = w.shape[1]
    has_small = w_small is not None
    in_specs = [pl.BlockSpec((tm, k), lambda i, j: (i, 0)),
                pl.BlockSpec((1, k), lambda i, j: (0, 0)),
                pl.BlockSpec((k, tn), lambda i, j: (0, j))]
    out_specs = [pl.BlockSpec((tm, tn), lambda i, j: (i, j))]
    out_shape = [jax.ShapeDtypeStruct((m, n), out_dtype)]
    args = [x, g.reshape(1, k), w]
    if has_small:
        in_specs.append(pl.BlockSpec((k, SMALL), lambda i, j: (0, 0)))
        out_specs.append(pl.BlockSpec((tm, SMALL), lambda i, j: (i, 0)))
        out_shape.append(jax.ShapeDtypeStruct((m, SMALL), F32))
        args.append(w_small)
    res = pl.pallas_call(
        functools.partial(_rms_mm_kernel, has_small=has_small),
        grid=(m // tm, n // tn), in_specs=in_specs, out_specs=out_specs, out_shape=out_shape,
        scratch_shapes=[pltpu.VMEM((tm, k), BF16)], compiler_params=_cparams(2), name="rms_matmul")(*args)
    return res if has_small else res[0]


def _halo_rows(width):
    return 8 * ((width - 1 + 7) // 8)


def _conv_load_tile(x, hist_ref, ext_ref, tt, width):
    hp = _halo_rows(width)
    t = pl.program_id(1)

    @pl.when(t == 0)
    def _():
        ext_ref[0:hp, :] = jnp.zeros((hp, ext_ref.shape[1]), F32)
        ext_ref[hp - (width - 1):hp, :] = hist_ref[...]

    @pl.when(t > 0)
    def _():
        ext_ref[0:hp, :] = ext_ref[tt:tt + hp, :]

    ext_ref[hp:hp + tt, :] = x


def _conv_rows(ext_ref, w_ref, r0, nrows, width, c0=0, ncols=None):
    hp = _halo_rows(width)
    ncols = ext_ref.shape[1] if ncols is None else ncols
    acc = None
    for i in range(width):
        off = hp - (width - 1) + i + r0
        term = w_ref[i:i + 1, c0:c0 + ncols] * ext_ref[off:off + nrows, c0:c0 + ncols]
        acc = term if acc is None else acc + term
    return acc


def _conv_state(ext_ref, st_ref, tt, width):
    hp = _halo_rows(width)
    st_ref[...] = ext_ref[hp + tt - (width - 1):hp + tt, :]


def _sconv_kernel(h_ref, b_ref, c_ref, hist_ref, w_ref, o_ref, st_ref, ext_ref, *, tt):
    _conv_load_tile(c_ref[...].astype(F32) * h_ref[...].astype(F32), hist_ref, ext_ref, tt, SC_W)
    rb = min(tt, 128)
    for r0 in range(0, tt, rb):
        o_ref[r0:r0 + rb, :] = (b_ref[r0:r0 + rb, :].astype(F32)
                                * _conv_rows(ext_ref, w_ref, r0, rb, SC_W)).astype(o_ref.dtype)
    _conv_state(ext_ref, st_ref, tt, SC_W)


def _hist_spec(hist, layer, rows, cols):
    if hist.ndim == 4:
        return pl.BlockSpec((None, None, rows, cols), lambda i, j: (i, layer, 0, 0))
    return pl.BlockSpec((None, rows, cols), lambda i, j: (i, 0, 0))


def _sconv(z3, hist, w, tt, layer=0):
    b, t, _ = z3.shape
    col = lambda c: pl.BlockSpec((None, tt, BRANCH), lambda i, j, c=c: (i, j, c // BRANCH))
    return pl.pallas_call(
        functools.partial(_sconv_kernel, tt=tt), grid=(b, t // tt),
        in_specs=[col(C_AH), col(C_AB), col(C_AC),
                  _hist_spec(hist, layer, SC_W - 1, BRANCH),
                  pl.BlockSpec((SC_W, BRANCH), lambda i, j: (0, 0))],
        out_specs=[pl.BlockSpec((None, tt, BRANCH), lambda i, j: (i, j, 0)),
                   pl.BlockSpec((None, SC_W - 1, BRANCH), lambda i, j: (i, 0, 0))],
        out_shape=[jax.ShapeDtypeStruct((b, t, BRANCH), z3.dtype),
                   jax.ShapeDtypeStruct((b, SC_W - 1, BRANCH), F32)],
        scratch_shapes=[pltpu.VMEM((_halo_rows(SC_W) + tt, BRANCH), F32)],
        compiler_params=_cparams(2), name="sconv")(z3, z3, z3, hist, w)


def _conf_kernel(v_ref, g_ref, hist_ref, w_ref, cb_ref, lg_ref, lb_ref, o_ref, st_ref, ext_ref, *phase, tt):
    _conv_load_tile(v_ref[...].astype(F32) * jax.nn.sigmoid(g_ref[...].astype(F32)), hist_ref, ext_ref, tt, CONF_W)
    hp = _halo_rows(CONF_W)
    if phase:
        (ph_ref,) = phase
        n = hp + tt - 8
        for r in range(1, 8):
            ph_ref[r - 1, 0:n, :] = ext_ref[r:r + n, :]
    rb = min(tt, 64)
    for r0 in range(0, tt, rb):
        if phase:
            y = None
            for i in range(CONF_W):
                a, r = divmod(hp - (CONF_W - 1) + i, 8)
                rows = slice(8 * a + r0, 8 * a + r0 + rb)
                term = w_ref[i:i + 1, :] * (ext_ref[rows, :] if r == 0 else ph_ref[r - 1, rows, :])
                y = term if y is None else y + term
            y = y + cb_ref[...]
        else:
            y = _conv_rows(ext_ref, w_ref, r0, rb, CONF_W) + cb_ref[...]
        mu = jnp.mean(y, axis=-1, keepdims=True)
        yc = y - mu
        var = jnp.mean(yc * yc, axis=-1, keepdims=True)
        o_ref[r0:r0 + rb, :] = _silu((yc * lax.rsqrt(var + 1e-5)) * lg_ref[...] + lb_ref[...]).astype(o_ref.dtype)
    _conv_state(ext_ref, st_ref, tt, CONF_W)


def _conformer(z3, hist, w, cb, lg, lb, tt, layer=0):
    b, t, _ = z3.shape
    col = lambda c: pl.BlockSpec((None, tt, BRANCH), lambda i, j, c=c: (i, j, c // BRANCH))
    row = pl.BlockSpec((1, BRANCH), lambda i, j: (0, 0))
    scratch = [pltpu.VMEM((_halo_rows(CONF_W) + tt, BRANCH), F32)]
    if tt % 8 == 0:
        scratch.append(pltpu.VMEM((7, _halo_rows(CONF_W) + tt, BRANCH), F32))
    return pl.pallas_call(
        functools.partial(_conf_kernel, tt=tt), grid=(b, t // tt),
        in_specs=[col(C_CV), col(C_CG),
                  _hist_spec(hist, layer, CONF_W - 1, BRANCH),
                  pl.BlockSpec((CONF_W, BRANCH), lambda i, j: (0, 0)), row, row, row],
        out_specs=[pl.BlockSpec((None, tt, BRANCH), lambda i, j: (i, j, 0)),
                   pl.BlockSpec((None, CONF_W - 1, BRANCH), lambda i, j: (i, 0, 0))],
        out_shape=[jax.ShapeDtypeStruct((b, t, BRANCH), z3.dtype),
                   jax.ShapeDtypeStruct((b, CONF_W - 1, BRANCH), F32)],
        scratch_shapes=scratch,
        compiler_params=_cparams(2), name="conformer")(
            z3, z3, hist, w, cb.reshape(1, BRANCH), lg.reshape(1, BRANCH), lb.reshape(1, BRANCH))


def _gdn_pre_kernel(x_ref, hist_ref, w_ref, q_ref, k_ref, v_ref, st_ref, ext_ref, *, tt):
    _conv_load_tile(x_ref[...].astype(F32), hist_ref, ext_ref, tt, GDN_W)
    rb = min(tt, 128)
    for r0 in range(0, tt, rb):
        for part, dst in enumerate((q_ref, k_ref, v_ref)):
            y = _silu(_conv_rows(ext_ref, w_ref, r0, rb, GDN_W, c0=part * BRANCH, ncols=BRANCH))
            if part == 2:
                dst[r0:r0 + rb, :] = y
                continue
            scale = GDN_DK ** -0.5 if part == 0 else 1.0
            for h in range(GDN_HEADS):
                blk = y[:, h * GDN_DK:(h + 1) * GDN_DK]
                nrm = lax.rsqrt(jnp.sum(blk * blk, axis=-1, keepdims=True) + 1e-6)
                dst[r0:r0 + rb, h * GDN_DK:(h + 1) * GDN_DK] = blk * nrm * scale
    _conv_state(ext_ref, st_ref, tt, GDN_W)


def _gdn_pre(z3, hist, w, tt, layer=0):
    b, t, _ = z3.shape
    wq = 3 * BRANCH
    out = pl.BlockSpec((None, tt, BRANCH), lambda i, j: (i, j, 0))
    return pl.pallas_call(
        functools.partial(_gdn_pre_kernel, tt=tt), grid=(b, t // tt),
        in_specs=[pl.BlockSpec((None, tt, wq), lambda i, j: (i, j, C_GQKV // wq)),
                  _hist_spec(hist, layer, GDN_W - 1, wq),
                  pl.BlockSpec((GDN_W, wq), lambda i, j: (0, 0))],
        out_specs=[out, out, out, pl.BlockSpec((None, GDN_W - 1, wq), lambda i, j: (i, 0, 0))],
        out_shape=[jax.ShapeDtypeStruct((b, t, BRANCH), F32)] * 3 + [jax.ShapeDtypeStruct((b, GDN_W - 1, wq), F32)],
        scratch_shapes=[pltpu.VMEM((_halo_rows(GDN_W) + tt, wq), F32)],
        compiler_params=_cparams(2), name="gdn_pre")(z3, hist, w)


def _fox_bias_placement():
    pq = np.zeros((SMALL, BRANCH), np.float32)
    pk = np.zeros((SMALL, BRANCH), np.float32)
    for h in range(FOX_HEADS):
        base = (h // 2) * LANES + (FOX_DH if h % 2 == 0 else 0)
        for part in range(3):
            pq[8 * (part + 1) + h, base + part] = 1.0
            pq[0, base + 3 + part] = 1.0
            pk[0, base + part] = 1.0
            pk[8 * (part + 1) + h, base + 3 + part] = -1.0
    return jnp.asarray(pq, BF16), jnp.asarray(pk, BF16)


def _gates_kernel(zs_ref, bias_ref, alog_ref, *rest, tt, with_t):
    if with_t:
        pq_ref, pk_ref, p_ref, pt_ref, qa_ref, ka_ref, carry_ref = rest
    else:
        p_ref, carry_ref = rest
    zs = zs_ref[...]
    lane = lax.broadcasted_iota(jnp.int32, zs.shape, 1)
    tb = zs + bias_ref[...]
    g_log = -jnp.exp(alog_ref[...]) * _softplus(tb)
    beta = jax.nn.sigmoid(zs)
    flog = -_softplus(-tb)
    is_f = (lane >= 8) & (lane < 16)
    f_only = jnp.where(is_f, flog, 0.0)
    if tt > 1:
        r = lax.broadcasted_iota(jnp.int32, (tt, tt), 0)
        c = lax.broadcasted_iota(jnp.int32, (tt, tt), 1)
        csum = _dot_sel(jnp.where(r >= c, 1.0, 0.0), f_only)
    else:
        csum = f_only

    @pl.when(pl.program_id(1) == 0)
    def _():
        carry_ref[...] = jnp.zeros_like(carry_ref)

    csum = csum + carry_ref[...]
    carry_ref[...] = csum[tt - 1:tt, :]
    p = jnp.where(lane < 4, g_log, jnp.where(lane < 8, beta, f_only)) + pltpu.roll(csum, 8, axis=1)
    p_ref[...] = p
    if with_t:
        pt_ref[...] = p.T
        hi = csum.astype(BF16).astype(F32)
        r1 = csum - hi
        mid = r1.astype(BF16).astype(F32)
        lo = (r1 - mid).astype(BF16).astype(F32)
        c3 = (hi + pltpu.roll(mid, 8, axis=1) + pltpu.roll(lo, 16, axis=1) + jnp.where(lane == 0, 1.0, 0.0)).astype(BF16)
        qa_ref[...] = _dot(c3, pq_ref[...]).astype(BF16)
        ka_ref[...] = _dot(c3, pk_ref[...]).astype(BF16)


def _gates(zs3, bias_row, alog_row, tt):
    b, t, _ = zs3.shape
    with_t = tt % LANES == 0
    row = pl.BlockSpec((1, SMALL), lambda i, j: (0, 0))
    in_specs = [pl.BlockSpec((None, tt, SMALL), lambda i, j: (i, j, 0)), row, row]
    args = [zs3, bias_row, alog_row]
    out_specs = [pl.BlockSpec((None, tt, SMALL), lambda i, j: (i, j, 0))]
    out_shape = [jax.ShapeDtypeStruct((b, t, SMALL), F32)]
    if with_t:
        place = pl.BlockSpec((SMALL, BRANCH), lambda i, j: (0, 0))
        aug = pl.BlockSpec((None, tt, BRANCH), lambda i, j: (i, j, 0))
        in_specs += [place, place]
        args += list(_fox_bias_placement())
        out_specs += [pl.BlockSpec((None, SMALL, tt), lambda i, j: (i, 0, j)), aug, aug]
        out_shape += [jax.ShapeDtypeStruct((b, SMALL, t), F32)] + [jax.ShapeDtypeStruct((b, t, BRANCH), BF16)] * 2
    res = pl.pallas_call(
        functools.partial(_gates_kernel, tt=tt, with_t=with_t), grid=(b, t // tt),
        in_specs=in_specs, out_specs=out_specs, out_shape=out_shape,
        scratch_shapes=[pltpu.VMEM((1, SMALL), F32)],
        compiler_params=_cparams(2), name="gates")(*args)
    return res if with_t else (res[0], None, None, None)


def _unit_lower_inverse_many(ns, c):
    r = lax.broadcasted_iota(jnp.int32, (c, c), 0)
    q = lax.broadcasted_iota(jnp.int32, (c, c), 1)
    mm = lambda a, b: _dot(a.astype(BF16), b.astype(BF16))
    blk8 = (r >> 3) == (q >> 3)
    eye = jnp.where(r == q, 1.0, 0.0)
    n0s = [jnp.where(blk8, n, 0.0) for n in ns]
    xs = [eye - n0 for n0 in n0s]
    ms = [mm(n0, n0) for n0 in n0s]
    xs = [x + mm(x, m) for x, m in zip(xs, ms)]
    ms = [mm(m, m) for m in ms]
    xs = [x + mm(x, m) for x, m in zip(xs, ms)]
    s = 3
    while (1 << s) < c:
        lower_left = ((r >> (s + 1)) == (q >> (s + 1))) & ((r >> s) != (q >> s))
        ts = [mm(x, jnp.where(lower_left, n, 0.0)) for x, n in zip(xs, ns)]
        xs = [x - mm(t, x) for x, t in zip(xs, ts)]
        s += 1
    return xs


def _gdn_chunk_kernel(q_ref, k_ref, v_ref, p_ref, pt_ref, gz_ref, nw_ref, s0_ref, o_ref, so_ref,
                      s_ref, u_ref, w_ref, qd_ref, kdt_ref, qk_ref, *, tt):
    c = GDN_CHUNK
    nc = tt // c

    @pl.when(pl.program_id(1) == 0)
    def _():
        s_ref[...] = s0_ref[...]

    r = lax.broadcasted_iota(jnp.int32, (c, c), 0)
    q_i = lax.broadcasted_iota(jnp.int32, (c, c), 1)
    tril = r >= q_i
    strict = r > q_i
    tri_f = jnp.where(tril, 1.0, 0.0).astype(F32)
    triu_f = jnp.where(r <= q_i, 1.0, 0.0).astype(F32)
    probs = [(ci, h) for ci in range(nc) for h in range(GDN_HEADS)]
    rows_of = lambda ci: slice(ci * c, (ci + 1) * c)
    cols_of = lambda h: slice(h * GDN_DK, (h + 1) * GDN_DK)

    d_cols = [_dot_sel(tri_f, p_ref[rows_of(ci), :]) for ci in range(nc)]
    d_rows = [_dot_sel_rhs(pt_ref[:, rows_of(ci)], triu_f) for ci in range(nc)]
    ns, rhss, dls = [], [], []
    for ci, h in probs:
        rows, hs = rows_of(ci), cols_of(h)
        qc, kc, vc = q_ref[rows, hs], k_ref[rows, hs], v_ref[rows, hs]
        beta = p_ref[rows, 4 + h:5 + h]
        dcol = d_cols[ci][:, h:h + 1]
        drow = d_rows[ci][h:h + 1, :]
        decay = jnp.where(tril, jnp.exp(jnp.where(tril, dcol - drow, 0.0)), 0.0)
        kb = kc * beta
        kbf = kc.astype(BF16)
        ns.append(jnp.where(strict, _dot_nt(kb.astype(BF16), kbf) * decay, 0.0))
        qk_ref[ci * GDN_HEADS + h] = jnp.where(tril, _dot_nt(qc.astype(BF16), kbf) * decay, 0.0).astype(BF16)
        ed = jnp.exp(dcol)
        dlast = dcol[c - 1:c, :]
        qd_ref[rows, hs] = (qc * ed).astype(BF16)
        kdt_ref[ci * GDN_HEADS + h] = (kc * jnp.exp(dlast - dcol)).T.astype(BF16)
        rhss.append(jnp.concatenate([vc * beta, kb * ed], axis=1).astype(BF16))
        dls.append(jnp.exp(dlast))
    t_invs = _unit_lower_inverse_many(ns, c)
    for (ci, h), t_inv, rhs in zip(probs, t_invs, rhss):
        sol = _dot(t_inv.astype(BF16), rhs)
        u_ref[rows_of(ci), cols_of(h)] = sol[:, :GDN_DK]
        w_ref[rows_of(ci), cols_of(h)] = sol[:, GDN_DK:].astype(BF16)

    for ci in range(nc):
        rows = rows_of(ci)
        s_olds = [s_ref[h] for h in range(GDN_HEADS)]
        s_bfs = [s.astype(BF16) for s in s_olds]
        vns = [(u_ref[rows, cols_of(h)] - _dot(w_ref[rows, cols_of(h)], s_bfs[h])).astype(BF16)
               for h in range(GDN_HEADS)]
        for h in range(GDN_HEADS):
            hs = cols_of(h)
            idx = ci * GDN_HEADS + h
            o = _dot(qd_ref[rows, hs], s_bfs[h]) + _dot(qk_ref[idx], vns[h])
            s_ref[h] = s_olds[h] * dls[idx] + _dot(kdt_ref[idx], vns[h])
            o_ref[rows, hs] = (_rms(o, nw_ref[...]) * _silu(gz_ref[rows, hs].astype(F32))).astype(o_ref.dtype)
    so_ref[...] = s_ref[...]


def _gdn_chunked(q, k, v, p, pt, z3, nw, s0, tt):
    b, t, _ = q.shape
    blk = pl.BlockSpec((None, tt, BRANCH), lambda i, j: (i, j, 0))
    st = pl.BlockSpec((None, GDN_HEADS, GDN_DK, GDN_DK), lambda i, j: (i, 0, 0, 0))
    return pl.pallas_call(
        functools.partial(_gdn_chunk_kernel, tt=tt), grid=(b, t // tt),
        in_specs=[blk, blk, blk,
                  pl.BlockSpec((None, tt, SMALL), lambda i, j: (i, j, 0)),
                  pl.BlockSpec((None, SMALL, tt), lambda i, j: (i, 0, j)),
                  pl.BlockSpec((None, tt, BRANCH), lambda i, j: (i, j, C_GZ // BRANCH)),
                  pl.BlockSpec((1, GDN_DK), lambda i, j: (0, 0)), st],
        out_specs=[blk, st],
        out_shape=[jax.ShapeDtypeStruct((b, t, BRANCH), z3.dtype),
                   jax.ShapeDtypeStruct((b, GDN_HEADS, GDN_DK, GDN_DK), F32)],
        scratch_shapes=[pltpu.VMEM((GDN_HEADS, GDN_DK, GDN_DK), F32),
                        pltpu.VMEM((tt, BRANCH), F32), pltpu.VMEM((tt, BRANCH), BF16), pltpu.VMEM((tt, BRANCH), BF16),
                        pltpu.VMEM((tt // GDN_CHUNK * GDN_HEADS, GDN_DK, GDN_CHUNK), BF16),
                        pltpu.VMEM((tt // GDN_CHUNK * GDN_HEADS, GDN_CHUNK, GDN_CHUNK), BF16)],
        compiler_params=_cparams(2), name="gdn_chunked")(q, k, v, p, pt, z3, nw.reshape(1, GDN_DK), s0)


def _gdn_step_kernel(q_ref, k_ref, v_ref, p_ref, gz_ref, nw_ref, s0_ref, o_ref, so_ref):
    r8 = lax.broadcasted_iota(jnp.int32, (8, GDN_DK), 0)
    rr = lax.broadcasted_iota(jnp.int32, (GDN_DK, GDN_DK), 0)
    cc = lax.broadcasted_iota(jnp.int32, (GDN_DK, GDN_DK), 1)
    p = p_ref[...]
    for h in range(GDN_HEADS):
        hs = slice(h * GDN_DK, (h + 1) * GDN_DK)
        q, k, v = q_ref[:, hs], k_ref[:, hs], v_ref[:, hs]
        g = p[:, h:h + 1]
        beta = p[:, 4 + h:5 + h]
        a = jnp.exp(g)
        w = k * beta * a
        s_old = s0_ref[h]
        s_bf = s_old.astype(BF16)
        lhs = jnp.where(r8 == 0, jnp.broadcast_to(w, (8, GDN_DK)),
                        jnp.where(r8 == 1, jnp.broadcast_to(q * a, (8, GDN_DK)), 0.0))
        ws = _dot(lhs.astype(BF16), s_bf)
        v_new = v * beta - ws[0:1, :]
        qk = jnp.sum(q.astype(BF16).astype(F32) * k.astype(BF16).astype(F32), axis=-1, keepdims=True)
        o = ws[1:2, :] + qk.astype(BF16).astype(F32) * v_new.astype(BF16).astype(F32)
        kdiag = jnp.where(rr == cc, jnp.broadcast_to(k, (GDN_DK, GDN_DK)), 0.0)
        outer = _dot(kdiag.astype(BF16), jnp.broadcast_to(v_new, (GDN_DK, GDN_DK)).astype(BF16))
        so_ref[h] = s_old * a + outer
        on = _rms(o, nw_ref[...])
        o_ref[:, hs] = on * _silu(gz_ref[:, hs])


def _gdn_step(q, k, v, p, z3, nw, state, layer):
    b = q.shape[0]
    blk = pl.BlockSpec((None, 1, BRANCH), lambda i: (i, 0, 0))
    st = pl.BlockSpec((None, GDN_HEADS, GDN_DK, GDN_DK), lambda i: (i, 0, 0, 0))
    st_in = pl.BlockSpec((None, None, GDN_HEADS, GDN_DK, GDN_DK), lambda i: (i, layer, 0, 0, 0))
    s0 = state
    return pl.pallas_call(
        _gdn_step_kernel, grid=(b,),
        in_specs=[blk, blk, blk, pl.BlockSpec((None, 1, SMALL), lambda i: (i, 0, 0)),
                  pl.BlockSpec((None, 1, BRANCH), lambda i: (i, 0, C_GZ // BRANCH)),
                  pl.BlockSpec((1, GDN_DK), lambda i: (0, 0)), st_in],
        out_specs=[blk, st],
        out_shape=[jax.ShapeDtypeStruct((b, 1, BRANCH), F32),
                   jax.ShapeDtypeStruct((b, GDN_HEADS, GDN_DK, GDN_DK), F32)],
        compiler_params=_cparams(1), name="gdn_step")(q, k, v, p, z3, nw.reshape(1, GDN_DK), s0)


def _fox_kernel(q_ref, k_ref, v_ref, qa_ref, ka_ref, o_ref, m_ref, acc_ref, *, tq, tk):
    qi, kj = pl.program_id(1), pl.program_id(2)

    @pl.when(kj == 0)
    def _():
        m_ref[...] = jnp.full(m_ref.shape, NEG_INF, F32)
        acc_ref[...] = jnp.zeros(acc_ref.shape, F32)

    def step(on_diagonal):
        lane = lax.broadcasted_iota(jnp.int32, (1, LANES), 1)
        if on_diagonal:
            causal = (lax.broadcasted_iota(jnp.int32, (tq, tk), 1) <= lax.broadcasted_iota(jnp.int32, (tq, tk), 0))
        for h in range(FOX_HEADS):
            ps = slice((h // 2) * LANES, (h // 2 + 1) * LANES)
            own = (lane >= (h % 2) * FOX_DH) & (lane < (h % 2 + 1) * FOX_DH)
            q2 = (q_ref[:, ps] * (FOX_DH ** -0.5)).astype(BF16)
            s = _dot_nt(jnp.where(own, q2, qa_ref[:, ps]), jnp.where(own, k_ref[:, ps].astype(BF16), ka_ref[:, ps]))
            if on_diagonal:
                s = jnp.where(causal, s, NEG_INF)
            m_old = m_ref[h]
            m_new = jnp.maximum(m_old, jnp.max(s, axis=-1, keepdims=True))
            pr = jnp.exp(s - jnp.concatenate([m_new] * (tk // LANES), axis=1))
            v1 = jnp.where(own, v_ref[:, ps].astype(BF16), jnp.ones((), BF16))
            acc_ref[h] = acc_ref[h] * jnp.exp(m_old - m_new) + _dot(pr.astype(BF16), v1)
            m_ref[h] = m_new

    @pl.when(kj < qi)
    def _():
        step(False)

    @pl.when(kj == qi)
    def _():
        step(True)
        first = lax.broadcasted_iota(jnp.int32, (1, LANES), 1) < FOX_DH
        for pair in range(FOX_HEADS // 2):
            a0, a1 = acc_ref[2 * pair], acc_ref[2 * pair + 1]
            o0 = a0 / pltpu.roll(a0, FOX_DH, axis=1)
            o1 = a1 / pltpu.roll(a1, FOX_DH, axis=1)
            o_ref[:, pair * LANES:(pair + 1) * LANES] = jnp.where(first, o0, o1).astype(o_ref.dtype)


def _fox_prompt(z3, qa, ka, tq):
    b, t, _ = z3.shape
    tk = tq
    nq = t // tq
    kv = lambda c: pl.BlockSpec((None, tk, BRANCH), lambda i, qi, kj, c=c: (i, jnp.minimum(kj, qi), c // BRANCH))
    return pl.pallas_call(
        functools.partial(_fox_kernel, tq=tq, tk=tk), grid=(b, nq, nq),
        in_specs=[pl.BlockSpec((None, tq, BRANCH), lambda i, qi, kj: (i, qi, C_FQ // BRANCH)),
                  kv(C_FK), kv(C_FV),
                  pl.BlockSpec((None, tq, BRANCH), lambda i, qi, kj: (i, qi, 0)),
                  pl.BlockSpec((None, tk, BRANCH), lambda i, qi, kj: (i, jnp.minimum(kj, qi), 0))],
        out_specs=pl.BlockSpec((None, tq, BRANCH), lambda i, qi, kj: (i, qi, 0)),
        out_shape=jax.ShapeDtypeStruct((b, t, BRANCH), z3.dtype),
        scratch_shapes=[pltpu.VMEM((FOX_HEADS, tq, LANES), F32), pltpu.VMEM((FOX_HEADS, tq, LANES), F32)],
        compiler_params=_cparams(3), name="fox_prompt")(z3, z3, z3, qa, ka)


def _fox_decode_kernel(pt_ref, q_ref, kc_ref, vc_ref, fc_ref, *refs):
    del pt_ref
    n = PAGES_PER_STEP
    k_refs, v_refs, lf_refs = refs[:n], refs[n:2 * n], refs[2 * n:3 * n]
    o_ref, qb_ref, m_ref, l_ref, acc_ref, run_ref = refs[3 * n:]
    j = pl.program_id(1)
    nj = pl.num_programs(1)
    lane = lax.broadcasted_iota(jnp.int32, (1, LANES), 1)
    row8 = lax.broadcasted_iota(jnp.int32, (FOX_HEADS, LANES), 0)

    def to_cols(row):
        return jnp.broadcast_to(row, (LANES, BRANCH)).T

    def head_sum(x):
        out = jnp.zeros((FOX_HEADS, LANES), F32)
        for h in range(FOX_HEADS):
            sh = jnp.sum(x[h * FOX_DH:(h + 1) * FOX_DH, :], axis=0, keepdims=True)
            out = jnp.where(row8 == h, jnp.broadcast_to(sh, (FOX_HEADS, LANES)), out)
        return out

    @pl.when(j == 0)
    def _():
        qb_ref[...] = to_cols(q_ref[...] * (FOX_DH ** -0.5))
        m_ref[...] = head_sum(qb_ref[...] * to_cols(kc_ref[...]))
        l_ref[...] = jnp.ones(l_ref.shape, F32)
        acc_ref[...] = jnp.where(lane == 0, to_cols(vc_ref[...]), 0.0)
        run_ref[...] = fc_ref[...]

    qb = qb_ref[...]
    lf = jnp.concatenate([lf_refs[i][...] for i in range(n)], axis=0)
    pr_ = lax.broadcasted_iota(jnp.int32, (LANES, LANES), 0)
    pc_ = lax.broadcasted_iota(jnp.int32, (LANES, LANES), 1)
    in_page = _dot_sel_rhs(lf, jnp.where(pr_ > pc_, 1.0, 0.0))
    tot = jnp.sum(lf, axis=-1, keepdims=True)
    run = run_ref[...]
    logits = [None] * n
    for i in reversed(range(n)):
        s_i = head_sum(k_refs[i][...].reshape(BRANCH, LANES) * qb)
        logits[i] = s_i + in_page[8 * i:8 * (i + 1), :] + run
        run = run + tot[8 * i:8 * (i + 1), :]
    run_ref[...] = run
    mx = logits[0]
    for i in range(1, n):
        mx = jnp.maximum(mx, logits[i])
    m_old = m_ref[...]
    m_new = jnp.maximum(m_old, jnp.max(mx, axis=-1, keepdims=True))
    alpha = jnp.exp(m_old - m_new)
    ps = [jnp.exp(logits[i] - m_new) for i in range(n)]
    psum = ps[0]
    for i in range(1, n):
        psum = psum + ps[i]
    l_ref[...] = alpha * l_ref[...] + jnp.sum(psum, axis=-1, keepdims=True)
    m_ref[...] = m_new
    for h in range(FOX_HEADS):
        hs = slice(h * FOX_DH, (h + 1) * FOX_DH)
        a = acc_ref[hs, :] * alpha[h:h + 1, :]
        for i in range(n):
            a = a + v_refs[i][h] * ps[i][h:h + 1, :]
        acc_ref[hs, :] = a

    @pl.when(j == nj - 1)
    def _():
        l_all = l_ref[...]
        l_cols = jnp.concatenate([jnp.broadcast_to(l_all[h:h + 1, :], (FOX_DH, LANES)) for h in range(FOX_HEADS)],
                                 axis=0)
        o_cols = jnp.broadcast_to(jnp.sum(acc_ref[...], axis=-1, keepdims=True), (BRANCH, LANES)) / l_cols
        o_ref[...] = o_cols.T[0:1, :]


def _fox_decode(page_table, q, kc, vc, fc, cache_kt, cache_vt, cache_lft, layer):
    b, n_pages = page_table.shape
    nj = n_pages // PAGES_PER_STEP
    tok = pl.BlockSpec((None, 1, BRANCH), lambda i, j, pt: (i, 0, 0))
    pidx = lambda i, j, pt, k: pt[i, (nj - 1 - j) * PAGES_PER_STEP + k]
    page = lambda k: pl.BlockSpec((None, None, FOX_HEADS, FOX_DH, PAGE),
                                  lambda i, j, pt, k=k: (pidx(i, j, pt, k), layer, 0, 0, 0))
    lpage = lambda k: pl.BlockSpec((None, None, FOX_HEADS, PAGE),
                                   lambda i, j, pt, k=k: (pidx(i, j, pt, k), layer, 0, 0))
    rng = range(PAGES_PER_STEP)
    in_specs = ([tok, tok, tok, pl.BlockSpec((None, FOX_HEADS, LANES), lambda i, j, pt: (i, 0, 0))]
                + [page(k) for k in rng] * 2 + [lpage(k) for k in rng])
    return pl.pallas_call(
        _fox_decode_kernel,
        grid_spec=pltpu.PrefetchScalarGridSpec(
            num_scalar_prefetch=1, grid=(b, nj), in_specs=in_specs, out_specs=tok,
            scratch_shapes=[pltpu.VMEM((BRANCH, LANES), F32), pltpu.VMEM((FOX_HEADS, LANES), F32),
                            pltpu.VMEM((FOX_HEADS, LANES), F32), pltpu.VMEM((BRANCH, LANES), F32),
                            pltpu.VMEM((FOX_HEADS, LANES), F32)]),
        out_shape=jax.ShapeDtypeStruct((b, 1, BRANCH), F32),
        compiler_params=_cparams(2), name="fox_decode")(
            page_table, q, kc, vc, fc, *([cache_kt] * PAGES_PER_STEP), *([cache_vt] * PAGES_PER_STEP),
            *([cache_lft] * PAGES_PER_STEP))


def _merge_kernel(oa, ob, oc, od, g0, g1, g2, g3, x_ref, wb_ref, wo_ref, out_ref):
    merged = None
    for i, (o_ref, g_ref) in enumerate(((oa, g0), (ob, g1), (oc, g2), (od, g3))):
        term = jax.nn.sigmoid(g_ref[...].astype(F32)) * _dot(o_ref[...].astype(BF16), wb_ref[i])
        merged = term if merged is None else merged + term
    out_ref[...] = x_ref[...] + _dot(merged.astype(BF16), wo_ref[...])


def _merge(o_list, z2, x2, wb, wo, tm):
    m = x2.shape[0]
    o_spec = pl.BlockSpec((tm, BRANCH), lambda i: (i, 0))
    gate = lambda g: pl.BlockSpec((tm, D_MODEL), lambda i, g=g: (i, C_GATE // D_MODEL + g))
    xs = pl.BlockSpec((tm, D_MODEL), lambda i: (i, 0))
    return pl.pallas_call(
        _merge_kernel, grid=(m // tm,),
        in_specs=[o_spec] * 4 + [gate(g) for g in range(4)] + [
            xs, pl.BlockSpec((4, BRANCH, D_MODEL), lambda i: (0, 0, 0)),
            pl.BlockSpec((D_MODEL, D_MODEL), lambda i: (0, 0))],
        out_specs=xs, out_shape=jax.ShapeDtypeStruct((m, D_MODEL), F32),
        compiler_params=_cparams(1), name="merge")(*o_list, z2, z2, z2, z2, x2, wb, wo)


def _cross_kernel(x_ref, g_ref, wq_ref, mk_ref, mv_ref, wo_ref, out_ref):
    x = x_ref[...]
    q = _dot(_rms(x, g_ref[...]).astype(BF16), wq_ref[...])
    mk = mk_ref[...].astype(BF16)
    mv = mv_ref[...].astype(BF16)
    outs = []
    for h in range(X_HEADS):
        hs = slice(h * X_DH, (h + 1) * X_DH)
        s = _dot_nt(q[:, hs].astype(BF16), mk[:, hs]) * (X_DH ** -0.5)
        e = jnp.exp(s - jnp.max(s, axis=-1, keepdims=True))
        pr = e / jnp.sum(e, axis=-1, keepdims=True)
        outs.append(_dot(pr.astype(BF16), mv[:, hs]))
    out_ref[...] = x + _dot(jnp.concatenate(outs, axis=1).astype(BF16), wo_ref[...])


def _cross(x3, g, wq, mk_arr, mk_spec, mv_arr, mv_spec, wo, tt):
    b, t, _ = x3.shape
    xw = X_HEADS * X_DH
    xs = pl.BlockSpec((None, tt, D_MODEL), lambda i, j: (i, j, 0))
    return pl.pallas_call(
        _cross_kernel, grid=(b, t // tt),
        in_specs=[xs, pl.BlockSpec((1, D_MODEL), lambda i, j: (0, 0)),
                  pl.BlockSpec((D_MODEL, xw), lambda i, j: (0, 0)), mk_spec, mv_spec,
                  pl.BlockSpec((xw, D_MODEL), lambda i, j: (0, 0))],
        out_specs=xs, out_shape=jax.ShapeDtypeStruct((b, t, D_MODEL), F32),
        compiler_params=_cparams(2), name="cross_attn")(x3, g.reshape(1, D_MODEL), wq, mk_arr, mv_arr, wo)


def _mem_attn_step_kernel(q_ref, km_ref, vm_ref, o_ref):
    q = q_ref[...]
    row8 = lax.broadcasted_iota(jnp.int32, (8, X_DH), 0)
    q8 = jnp.zeros((8, X_DH), F32)
    for h in range(X_HEADS):
        q8 = jnp.where(row8 == h, jnp.broadcast_to(q[:, h * X_DH:(h + 1) * X_DH], (8, X_DH)), q8)
    n = MEM_LEN * X_HEADS
    s = _dot_nt(q8.astype(BF16), km_ref[...].astype(BF16)) * (X_DH ** -0.5)
    col = lax.broadcasted_iota(jnp.int32, (8, n), 1)
    row = lax.broadcasted_iota(jnp.int32, (8, n), 0)
    s = jnp.where((col & (X_HEADS - 1)) == row, s, NEG_INF)
    e = jnp.exp(s - jnp.max(s, axis=-1, keepdims=True))
    pr = e / jnp.sum(e, axis=-1, keepdims=True)
    o8 = _dot(pr.astype(BF16), vm_ref[...].astype(BF16))
    o_ref[...] = jnp.concatenate([o8[h:h + 1, :] for h in range(X_HEADS)], axis=1)


def _mem_attn_step(q3, km, vm, layer):
    b = q3.shape[0]
    xw = X_HEADS * X_DH
    tok = pl.BlockSpec((None, 1, xw), lambda i: (i, 0, 0))
    mem = pl.BlockSpec((None, None, MEM_LEN * X_HEADS, X_DH), lambda i: (i, layer, 0, 0))
    return pl.pallas_call(
        _mem_attn_step_kernel, grid=(b,), in_specs=[tok, mem, mem], out_specs=tok,
        out_shape=jax.ShapeDtypeStruct((b, 1, xw), F32),
        compiler_params=_cparams(1), name="mem_attn_step")(q3, km, vm)


def _proj_res_kernel(a_ref, w_ref, x_ref, o_ref):
    o_ref[...] = x_ref[...] + _dot(a_ref[...].astype(BF16), w_ref[...])


def _proj_res(a, w, x):
    m, k = a.shape
    n = w.shape[1]
    full = lambda r, c: pl.BlockSpec((r, c), lambda i: (0, 0))
    return pl.pallas_call(
        _proj_res_kernel, grid=(1,), in_specs=[full(m, k), full(k, n), full(m, n)], out_specs=full(m, n),
        out_shape=jax.ShapeDtypeStruct((m, n), F32), compiler_params=_cparams(1), name="proj_res")(a, w, x)


def _ffn_kernel(x_ref, g_ref, wg_ref, wu_ref, cw_ref, wd_ref, hist_ref, fg_ref, out_ref, st_ref,
                xn_ref, acc_ref, halo_ref, ext_ref, *, tt, tf, nf, final_norm):
    t, f = pl.program_id(1), pl.program_id(2)
    hp = _halo_rows(FFN_W)

    @pl.when(f == 0)
    def _():
        xn_ref[...] = _rms(x_ref[...], g_ref[...]).astype(BF16)
        acc_ref[...] = jnp.zeros(acc_ref.shape, F32)

    gcol = _dot(xn_ref[...], wg_ref[...])
    u = _dot(xn_ref[...], wu_ref[...])

    @pl.when(t == 0)
    def _():
        ext_ref[0:hp, :] = jnp.zeros((hp, ext_ref.shape[1]), F32)
        ext_ref[hp - (FFN_W - 1):hp, :] = hist_ref[...]

    @pl.when(t > 0)
    def _():
        ext_ref[0:hp, :] = halo_ref[f]

    ext_ref[hp:hp + tt, :] = gcol
    halo_ref[f] = ext_ref[tt:tt + hp, :]
    rb = min(tt, 256)
    for r0 in range(0, tt, rb):
        gc = _conv_rows(ext_ref, cw_ref, r0, rb, FFN_W)
        act = _silu(gc) * u[r0:r0 + rb, :]
        acc_ref[r0:r0 + rb, :] += _dot(act.astype(BF16), wd_ref[...])

    @pl.when(f == nf - 1)
    def _():
        y = x_ref[...] + acc_ref[...]
        out_ref[...] = _rms(y, fg_ref[...]) if final_norm else y

    @pl.when((f == nf - 1) & (t == pl.num_programs(1) - 1))
    def _():
        for ff in range(nf):
            st_ref[:, ff * tf:(ff + 1) * tf] = halo_ref[ff][hp - (FFN_W - 1):hp, :]


def _ffn(x3, g, w_in, cw, w_out, hist, final_g, final_norm, tt, tf):
    b, t, _ = x3.shape
    nf = D_FF // tf
    xs = pl.BlockSpec((None, tt, D_MODEL), lambda i, j, f: (i, j, 0))
    st = pl.BlockSpec((None, FFN_W - 1, tf), lambda i, j, f: (i, 0, f))
    st_out = pl.BlockSpec((None, FFN_W - 1, D_FF), lambda i, j, f: (i, 0, 0))
    return pl.pallas_call(
        functools.partial(_ffn_kernel, tt=tt, tf=tf, nf=nf, final_norm=final_norm), grid=(b, t // tt, nf),
        in_specs=[xs, pl.BlockSpec((1, D_MODEL), lambda i, j, f: (0, 0)),
                  pl.BlockSpec((D_MODEL, tf), lambda i, j, f: (0, f)),
                  pl.BlockSpec((D_MODEL, tf), lambda i, j, f: (0, nf + f)),
                  pl.BlockSpec((FFN_W, tf), lambda i, j, f: (0, f)),
                  pl.BlockSpec((tf, D_MODEL), lambda i, j, f: (f, 0)), st,
                  pl.BlockSpec((1, D_MODEL), lambda i, j, f: (0, 0))],
        out_specs=[xs, st_out],
        out_shape=[jax.ShapeDtypeStruct((b, t, D_MODEL), F32), jax.ShapeDtypeStruct((b, FFN_W - 1, D_FF), F32)],
        scratch_shapes=[pltpu.VMEM((tt, D_MODEL), BF16), pltpu.VMEM((tt, D_MODEL), F32),
                        pltpu.VMEM((nf, _halo_rows(FFN_W), tf), F32),
                        pltpu.VMEM((_halo_rows(FFN_W) + tt, tf), F32)],
        compiler_params=_cparams(3), name="conv_ffn")(
            x3, g.reshape(1, D_MODEL), w_in, w_in, cw, w_out, hist, final_g.reshape(1, D_MODEL))


def _ffn_step_kernel(x_ref, g_ref, wg_ref, wu_ref, cw_ref, wd_ref, h0_ref, h1_ref, out_ref, s0_ref, s1_ref,
                     xn_ref, acc_ref):
    f = pl.program_id(0)

    @pl.when(f == 0)
    def _():
        xn_ref[...] = _rms(x_ref[...], g_ref[...]).astype(BF16)
        acc_ref[...] = jnp.zeros(acc_ref.shape, F32)

    gcol = _dot(xn_ref[...], wg_ref[...])
    u = _dot(xn_ref[...], wu_ref[...])
    gc = cw_ref[0:1, :] * h0_ref[...] + cw_ref[1:2, :] * h1_ref[...] + cw_ref[2:3, :] * gcol
    s0_ref[...] = h1_ref[...]
    s1_ref[...] = gcol
    acc_ref[...] += _dot((_silu(gc) * u).astype(BF16), wd_ref[...])

    @pl.when(f == pl.num_programs(0) - 1)
    def _():
        out_ref[...] = x_ref[...] + acc_ref[...]


def _ffn_step(x2, g, w_in, cw, w_out, h0, h1, tf):
    b = x2.shape[0]
    nf = D_FF // tf
    xs = pl.BlockSpec((b, D_MODEL), lambda f: (0, 0))
    hs = pl.BlockSpec((b, tf), lambda f: (0, f))
    return pl.pallas_call(
        _ffn_step_kernel, grid=(nf,),
        in_specs=[xs, pl.BlockSpec((1, D_MODEL), lambda f: (0, 0)),
                  pl.BlockSpec((D_MODEL, tf), lambda f: (0, f)),
                  pl.BlockSpec((D_MODEL, tf), lambda f: (0, nf + f)),
                  pl.BlockSpec((FFN_W, tf), lambda f: (0, f)),
                  pl.BlockSpec((tf, D_MODEL), lambda f: (f, 0)), hs, hs],
        out_specs=[xs, hs, hs],
        out_shape=[jax.ShapeDtypeStruct((b, D_MODEL), F32), jax.ShapeDtypeStruct((b, D_FF), F32),
                   jax.ShapeDtypeStruct((b, D_FF), F32)],
        scratch_shapes=[pltpu.VMEM((b, D_MODEL), BF16), pltpu.VMEM((b, D_MODEL), F32)],
        compiler_params=_cparams(1), name="conv_ffn_step")(x2, g.reshape(1, D_MODEL), w_in, w_in, cw, w_out, h0, h1)


def _kv_out_kernel(k_ref, v_ref, *refs):
    kt_ref, vt_ref = refs[-2:]
    kt_ref[...] = k_ref[...].astype(F32).T
    vt_ref[...] = v_ref[...].astype(F32).T


def _kv_out(z3, layer, depth, prev, tt):
    b, t, _ = z3.shape
    col = lambda c: pl.BlockSpec((None, tt, BRANCH), lambda i, j, c=c: (i, j, c // BRANCH))
    out = pl.BlockSpec((None, None, BRANCH, tt), lambda i, j: (i, layer, 0, j))
    shape = jax.ShapeDtypeStruct((b, depth, BRANCH, t), F32)
    in_specs, args, aliases = [col(C_FK), col(C_FV)], [z3, z3], {}
    if prev is not None:
        in_specs += [pl.BlockSpec(memory_space=pl.ANY)] * 2
        args += list(prev)
        aliases = {2: 0, 3: 1}
    return pl.pallas_call(
        _kv_out_kernel, grid=(b, t // tt), in_specs=in_specs, out_specs=[out, out], out_shape=[shape, shape],
        input_output_aliases=aliases, compiler_params=_cparams(2), name="kv_out")(*args)


def _final_norm_kernel(x_ref, g_ref, o_ref):
    o_ref[...] = _rms(x_ref[...], g_ref[...])


def _final_norm(x2, g, tm):
    m = x2.shape[0]
    xs = pl.BlockSpec((tm, D_MODEL), lambda i: (i, 0))
    return pl.pallas_call(
        _final_norm_kernel, grid=(m // tm,),
        in_specs=[xs, pl.BlockSpec((1, D_MODEL), lambda i: (0, 0))], out_specs=xs,
        out_shape=jax.ShapeDtypeStruct((m, D_MODEL), F32),
        compiler_params=_cparams(1), name="final_norm")(x2, g.reshape(1, D_MODEL))


def _prep_layer(l, w_in, w_branch, w_out, w_cq, w_co, w_ffn_in, w_ffn_out, gdn_a_log, gdn_dt_bias, fox_fbias):
    w = w_in[l]
    w_main = jnp.concatenate([w[:, :3584], w[:, 3592:6152], w[:, 6160:]], axis=1).astype(BF16)
    w_small = jnp.concatenate([w[:, 3584:3592], w[:, 6152:6160],
                               jnp.zeros((D_MODEL, SMALL - 16), F32)], axis=1).astype(BF16)
    zeros4 = jnp.zeros((4,), F32)
    bias_row = jnp.concatenate([gdn_dt_bias[l], zeros4, fox_fbias[l], jnp.zeros((SMALL - 16,), F32)]).reshape(1, SMALL)
    alog_row = jnp.concatenate([gdn_a_log[l], jnp.zeros((SMALL - 4,), F32)]).reshape(1, SMALL)
    return dict(w_main=w_main, w_small=w_small, bias_row=bias_row, alog_row=alog_row,
                wb=w_branch[l].astype(BF16), wo=w_out[l].astype(BF16), wcq=w_cq[l].astype(BF16),
                wco=w_co[l].astype(BF16), wfi=w_ffn_in[l].astype(BF16), wfo=w_ffn_out[l].astype(BF16))


def kernel(x_prompt, x_sample, cache_fox_k, cache_fox_v, cache_fox_logf, cache_mem_k, cache_mem_v, state_sconv, state_gdn_conv, state_gdn, state_conf_conv, state_ffn_conv, page_table, mem_prompt, norm_mix, w_in, w_branch, w_out, sconv_w, gdn_conv_w, gdn_a_log, gdn_dt_bias, gdn_norm, conf_dw_w, conf_dw_b, conf_ln_g, conf_ln_b, fox_fbias, norm_cross, norm_mem, w_cq, w_ckv, w_co, norm_ffn, w_ffn_in, ffn_conv_w, w_ffn_out, norm_final):
    depth = w_in.shape[0]
    bp, t, _ = x_prompt.shape
    bs = x_sample.shape[0]
    n_pool = cache_fox_k.shape[0]
    xw = X_HEADS * X_DH
    prep = [_prep_layer(l, w_in, w_branch, w_out, w_cq, w_co, w_ffn_in, w_ffn_out, gdn_a_log, gdn_dt_bias, fox_fbias)
            for l in range(depth)]

    tt = min(t, 512)
    mp = bp * t
    x = x_prompt.reshape(mp, D_MODEL)
    mem2 = mem_prompt.reshape(bp * MEM_LEN, D_MODEL)
    p_out = {k: [] for k in ("fk", "fv", "fl", "mk", "mv", "sc", "gc", "gs", "cc", "ff")}
    zero = lambda *s: jnp.zeros(s, F32)
    kt_vt = None
    for l in range(depth):
        w = prep[l]
        z, zs = _rms_matmul(x, norm_mix[l], w["w_main"], w["w_small"], tm=min(mp, 2048), tn=1024, out_dtype=BF16)
        z3 = z.reshape(bp, t, NZ)
        o_a, st_sc = _sconv(z3, zero(bp, SC_W - 1, BRANCH), sconv_w[l], tt)
        q, k, v, st_gc = _gdn_pre(z3, zero(bp, GDN_W - 1, 3 * BRANCH), gdn_conv_w[l], tt)
        p, pt, qa, ka = _gates(zs.reshape(bp, t, SMALL), w["bias_row"], w["alog_row"], tt)
        o_b, st_gs = _gdn_chunked(q, k, v, p, pt, z3, gdn_norm[l], zero(bp, GDN_HEADS, GDN_DK, GDN_DK), tt)
        o_c, st_cc = _conformer(z3, zero(bp, CONF_W - 1, BRANCH), conf_dw_w[l], conf_dw_b[l],
                                conf_ln_g[l], conf_ln_b[l], tt)
        o_d = _fox_prompt(z3, qa, ka, tt)
        o_list = [o.reshape(mp, BRANCH) for o in (o_a, o_b, o_c, o_d)]
        x = _merge(o_list, z, x, w["wb"], w["wo"], tm=min(mp, 256))
        kv = _rms_matmul(mem2, norm_mem[l], w_ckv[l].astype(BF16), None, tm=min(bp * MEM_LEN, 1024), tn=512)
        kv3 = kv.reshape(bp, MEM_LEN, 2 * xw)
        x = _cross(x.reshape(bp, t, D_MODEL), norm_cross[l], w["wcq"],
                   kv3, pl.BlockSpec((None, MEM_LEN, xw), lambda i, j: (i, 0, 0)),
                   kv3, pl.BlockSpec((None, MEM_LEN, xw), lambda i, j: (i, 0, 1)), w["wco"], tt)
        x, st_ff = _ffn(x, norm_ffn[l], w["wfi"], ffn_conv_w[l], w["wfo"], zero(bp, FFN_W - 1, D_FF),
                        norm_final, l == depth - 1, tt=min(t, 512), tf=D_FF // 2)
        x = x.reshape(mp, D_MODEL)
        kt_vt = _kv_out(z3, l, depth, kt_vt, tt)
        p_out["fl"].append(pt[:, 8:16, :])
        p_out["mk"].append(kv3[:, :, :xw])
        p_out["mv"].append(kv3[:, :, xw:])
        for key, val in (("sc", st_sc), ("gc", st_gc), ("gs", st_gs), ("cc", st_cc), ("ff", st_ff)):
            p_out[key].append(val)
    y_prompt = x.reshape(bp, t, D_MODEL)

    ckt = jnp.transpose(cache_fox_k, (0, 2, 3, 4, 1))
    cvt = jnp.transpose(cache_fox_v, (0, 2, 3, 4, 1))
    clt = jnp.transpose(cache_fox_logf, (0, 2, 3, 1))
    cmk = cache_mem_k.reshape(bs, depth, MEM_LEN * X_HEADS, X_DH)
    cmv = cache_mem_v.reshape(bs, depth, MEM_LEN * X_HEADS, X_DH)
    xs_ = x_sample.reshape(bs, D_MODEL)
    s_out = {k: [] for k in ("fk", "fv", "fl", "sc", "gc", "gs", "cc", "ff")}
    for l in range(depth):
        w = prep[l]
        z, zs = _rms_matmul(xs_, norm_mix[l], w["w_main"], w["w_small"], tm=bs, tn=2048)
        z3 = z.reshape(bs, 1, NZ)
        o_a, st_sc = _sconv(z3, state_sconv, sconv_w[l], 1, layer=l)
        q, k, v, st_gc = _gdn_pre(z3, state_gdn_conv, gdn_conv_w[l], 1, layer=l)
        p = _gates(zs.reshape(bs, 1, SMALL), w["bias_row"], w["alog_row"], 1)[0]
        o_b, st_gs = _gdn_step(q, k, v, p, z3, gdn_norm[l], state_gdn, l)
        o_c, st_cc = _conformer(z3, state_conf_conv, conf_dw_w[l], conf_dw_b[l], conf_ln_g[l], conf_ln_b[l], 1,
                                layer=l)
        fq, fk, fv = (z3[:, :, c:c + BRANCH] for c in (C_FQ, C_FK, C_FV))
        flog = p[:, :, 8:16]
        fc = jnp.broadcast_to(jnp.swapaxes(flog, 1, 2), (bs, FOX_HEADS, LANES))
        o_d = _fox_decode(page_table, fq, fk, fv, fc, ckt, cvt, clt, l)
        o_list = [o.reshape(bs, BRANCH) for o in (o_a, o_b, o_c, o_d)]
        xs_ = _merge(o_list, z, xs_, w["wb"], w["wo"], tm=bs)
        q3 = _rms_matmul(xs_, norm_cross[l], w["wcq"], None, tm=bs, tn=xw).reshape(bs, 1, xw)
        o3 = _mem_attn_step(q3, cmk, cmv, l)
        xs_ = _proj_res(o3.reshape(bs, xw), w["wco"], xs_)
        xs_, h_a, h_b = _ffn_step(xs_, norm_ffn[l], w["wfi"], ffn_conv_w[l], w["wfo"],
                                  state_ffn_conv[:, l, 0], state_ffn_conv[:, l, 1], tf=256)
        s_out["fk"].append(fk)
        s_out["fv"].append(fv)
        s_out["fl"].append(flog)
        for key, val in (("sc", st_sc), ("gc", st_gc), ("gs", st_gs), ("cc", st_cc),
                         ("ff", jnp.stack([h_a, h_b], axis=1))):
            s_out[key].append(val)
    y_sample = _final_norm(xs_, norm_final, tm=bs).reshape(bs, 1, D_MODEL)

    heads = lambda a: a.reshape(a.shape[0], a.shape[1], depth, FOX_HEADS, FOX_DH)
    memh = lambda a: a.reshape(bp, depth, MEM_LEN, X_HEADS, X_DH)
    from_pos_minor = lambda a: jnp.transpose(a.reshape(bp, depth, FOX_HEADS, FOX_DH, t), (0, 4, 1, 2, 3))
    return (y_prompt, y_sample,
            from_pos_minor(kt_vt[0]), from_pos_minor(kt_vt[1]),
            jnp.transpose(jnp.stack(p_out["fl"], axis=1), (0, 3, 1, 2)),
            memh(jnp.stack(p_out["mk"], axis=1)), memh(jnp.stack(p_out["mv"], axis=1)),
            jnp.stack(p_out["sc"], axis=1), jnp.stack(p_out["gc"], axis=1), jnp.stack(p_out["gs"], axis=1),
            jnp.stack(p_out["cc"], axis=1), jnp.stack(p_out["ff"], axis=1),
            heads(jnp.stack(s_out["fk"], axis=2)), heads(jnp.stack(s_out["fv"], axis=2)),
            jnp.stack(s_out["fl"], axis=2),
            jnp.stack(s_out["sc"], axis=1), jnp.stack(s_out["gc"], axis=1), jnp.stack(s_out["gs"], axis=1),
            jnp.stack(s_out["cc"], axis=1), jnp.stack(s_out["ff"], axis=1))
```

```python
import functools

import jax
import jax.numpy as jnp
import numpy as np
from jax import lax
from jax.experimental import pallas as pl
from jax.experimental.pallas import tpu as pltpu

F32 = jnp.float32
BF16 = jnp.bfloat16

D_MODEL = 1024
BRANCH = 512
GDN_HEADS = 4
GDN_DK = 128
FOX_HEADS = 8
FOX_DH = 64
X_HEADS = 4
X_DH = 128
D_FF = 2816
MEM_LEN = 256
PAGE = 128
SC_W, GDN_W, CONF_W, FFN_W = 3, 4, 31, 3
NEG_INF = -1e30
LOG2E = 1.4426950408889634

C_AH, C_AB, C_AC = 0, 512, 1024
C_GQKV = 1536
C_GZ = 3072
C_CV, C_CG = 3584, 4096
C_FQ, C_FK, C_FV = 4608, 5120, 5632
C_GATE = 6144
NZ = 10240
SMALL = 128
LANES = 128
GDN_CHUNK = 128
PAGES_PER_STEP = 16


def _cparams(n_axes, vmem_mb=48):
    return pltpu.CompilerParams(dimension_semantics=("arbitrary",) * n_axes,
                                vmem_limit_bytes=vmem_mb * 1024 * 1024)


def _dot(a, b, precision=None):
    return jnp.dot(a, b, preferred_element_type=F32, precision=precision)


def _dot_nt(a, b, precision=None):
    return lax.dot_general(a, b, (((1,), (1,)), ((), ())), preferred_element_type=F32, precision=precision)


def _split3(x):
    hi = x.astype(BF16)
    r1 = x - hi.astype(F32)
    mid = r1.astype(BF16)
    return hi, mid, (r1 - mid.astype(F32)).astype(BF16)


def _dot_sel(sel01, x):
    s = sel01.astype(BF16)
    hi, mid, lo = _split3(x)
    return _dot(s, hi) + _dot(s, mid) + _dot(s, lo)


def _dot_sel_rhs(x, sel01):
    s = sel01.astype(BF16)
    hi, mid, lo = _split3(x)
    return _dot(hi, s) + _dot(mid, s) + _dot(lo, s)


def _rms(x, g, eps=1e-6):
    return (x * lax.rsqrt(jnp.mean(x * x, axis=-1, keepdims=True) + eps)) * g


def _softplus(t):
    return jnp.maximum(t, 0.0) + jnp.log1p(jnp.exp(-jnp.abs(t)))


def _silu(t):
    return t * jax.nn.sigmoid(t)


def _rms_mm_kernel(*refs, has_small):
    if has_small:
        x_ref, g_ref, w_ref, ws_ref, o_ref, os_ref, xn_ref = refs
    else:
        x_ref, g_ref, w_ref, o_ref, xn_ref = refs

    @pl.when(pl.program_id(1) == 0)
    def _():
        xn_ref[...] = _rms(x_ref[...], g_ref[...]).astype(BF16)
        if has_small:
            os_ref[...] = _dot(xn_ref[...], ws_ref[...])

    o_ref[...] = _dot(xn_ref[...], w_ref[...]).astype(o_ref.dtype)


def _rms_matmul(x, g, w, w_small, tm, tn, out_dtype=F32):
    m, k = x.shape
    n = w.shape[1]
    has_small = w_small is not None
    in_specs = [pl.BlockSpec((tm, k), lambda i, j: (i, 0)),
                pl.BlockSpec((1, k), lambda i, j: (0, 0)),
                pl.BlockSpec((k, tn), lambda i, j: (0, j))]
    out_specs = [pl.BlockSpec((tm, tn), lambda i, j: (i, j))]
    out_shape = [jax.ShapeDtypeStruct((m, n), out_dtype)]
    args = [x, g.reshape(1, k), w]
    if has_small:
        in_specs.append(pl.BlockSpec((k, SMALL), lambda i, j: (0, 0)))
        out_specs.append(pl.BlockSpec((tm, SMALL), lambda i, j: (i, 0)))
        out_shape.append(jax.ShapeDtypeStruct((m, SMALL), F32))
        args.append(w_small)
    res = pl.pallas_call(
        functools.partial(_rms_mm_kernel, has_small=has_small),
        grid=(m // tm, n // tn), in_specs=in_specs, out_specs=out_specs, out_shape=out_shape,
        scratch_shapes=[pltpu.VMEM((tm, k), BF16)], compiler_params=_cparams(2), name="rms_matmul")(*args)
    return res if has_small else res[0]


def _halo_rows(width):
    return 8 * ((width - 1 + 7) // 8)


def _conv_load_tile(x, hist_ref, ext_ref, tt, width):
    hp = _halo_rows(width)
    t = pl.program_id(1)

    @pl.when(t == 0)
    def _():
        ext_ref[0:hp, :] = jnp.zeros((hp, ext_ref.shape[1]), F32)
        ext_ref[hp - (width - 1):hp, :] = hist_ref[...]

    @pl.when(t > 0)
    def _():
        ext_ref[0:hp, :] = ext_ref[tt:tt + hp, :]

    ext_ref[hp:hp + tt, :] = x


def _conv_rows(ext_ref, w_ref, r0, nrows, width, c0=0, ncols=None):
    hp = _halo_rows(width)
    ncols = ext_ref.shape[1] if ncols is None else ncols
    acc = None
    for i in range(width):
        off = hp - (width - 1) + i + r0
        term = w_ref[i:i + 1, c0:c0 + ncols] * ext_ref[off:off + nrows, c0:c0 + ncols]
        acc = term if acc is None else acc + term
    return acc


def _conv_state(ext_ref, st_ref, tt, width):
    hp = _halo_rows(width)
    st_ref[...] = ext_ref[hp + tt - (width - 1):hp + tt, :]


def _sconv_kernel(h_ref, b_ref, c_ref, hist_ref, w_ref, o_ref, st_ref, ext_ref, *, tt):
    _conv_load_tile(c_ref[...].astype(F32) * h_ref[...].astype(F32), hist_ref, ext_ref, tt, SC_W)
    rb = min(tt, 128)
    for r0 in range(0, tt, rb):
        o_ref[r0:r0 + rb, :] = (b_ref[r0:r0 + rb, :].astype(F32)
                                * _conv_rows(ext_ref, w_ref, r0, rb, SC_W)).astype(o_ref.dtype)
    _conv_state(ext_ref, st_ref, tt, SC_W)


def _hist_spec(hist, layer, rows, cols):
    if hist.ndim == 4:
        return pl.BlockSpec((None, None, rows, cols), lambda i, j: (i, layer, 0, 0))
    return pl.BlockSpec((None, rows, cols), lambda i, j: (i, 0, 0))


def _sconv(z3, hist, w, tt, layer=0):
    b, t, _ = z3.shape
    col = lambda c: pl.BlockSpec((None, tt, BRANCH), lambda i, j, c=c: (i, j, c // BRANCH))
    return pl.pallas_call(
        functools.partial(_sconv_kernel, tt=tt), grid=(b, t // tt),
        in_specs=[col(C_AH), col(C_AB), col(C_AC),
                  _hist_spec(hist, layer, SC_W - 1, BRANCH),
                  pl.BlockSpec((SC_W, BRANCH), lambda i, j: (0, 0))],
        out_specs=[pl.BlockSpec((None, tt, BRANCH), lambda i, j: (i, j, 0)),
                   pl.BlockSpec((None, SC_W - 1, BRANCH), lambda i, j: (i, 0, 0))],
        out_shape=[jax.ShapeDtypeStruct((b, t, BRANCH), z3.dtype),
                   jax.ShapeDtypeStruct((b, SC_W - 1, BRANCH), F32)],
        scratch_shapes=[pltpu.VMEM((_halo_rows(SC_W) + tt, BRANCH), F32)],
        compiler_params=_cparams(2), name="sconv")(z3, z3, z3, hist, w)


def _conf_kernel(v_ref, g_ref, hist_ref, w_ref, cb_ref, lg_ref, lb_ref, o_ref, st_ref, ext_ref, *phase, tt):
    _conv_load_tile(v_ref[...].astype(F32) * jax.nn.sigmoid(g_ref[...].astype(F32)), hist_ref, ext_ref, tt, CONF_W)
    hp = _halo_rows(CONF_W)
    if phase:
        (ph_ref,) = phase
        n = hp + tt - 8
        for r in range(1, 8):
            ph_ref[r - 1, 0:n, :] = ext_ref[r:r + n, :]
    rb = min(tt, 64)
    for r0 in range(0, tt, rb):
        if phase:
            y = None
            for i in range(CONF_W):
                a, r = divmod(hp - (CONF_W - 1) + i, 8)
                rows = slice(8 * a + r0, 8 * a + r0 + rb)
                term = w_ref[i:i + 1, :] * (ext_ref[rows, :] if r == 0 else ph_ref[r - 1, rows, :])
                y = term if y is None else y + term
            y = y + cb_ref[...]
        else:
            y = _conv_rows(ext_ref, w_ref, r0, rb, CONF_W) + cb_ref[...]
        mu = jnp.mean(y, axis=-1, keepdims=True)
        yc = y - mu
        var = jnp.mean(yc * yc, axis=-1, keepdims=True)
        o_ref[r0:r0 + rb, :] = _silu((yc * lax.rsqrt(var + 1e-5)) * lg_ref[...] + lb_ref[...]).astype(o_ref.dtype)
    _conv_state(ext_ref, st_ref, tt, CONF_W)


def _conformer(z3, hist, w, cb, lg, lb, tt, layer=0):
    b, t, _ = z3.shape
    col = lambda c: pl.BlockSpec((None, tt, BRANCH), lambda i, j, c=c: (i, j, c // BRANCH))
    row = pl.BlockSpec((1, BRANCH), lambda i, j: (0, 0))
    scratch = [pltpu.VMEM((_halo_rows(CONF_W) + tt, BRANCH), F32)]
    if tt % 8 == 0:
        scratch.append(pltpu.VMEM((7, _halo_rows(CONF_W) + tt, BRANCH), F32))
    return pl.pallas_call(
        functools.partial(_conf_kernel, tt=tt), grid=(b, t // tt),
        in_specs=[col(C_CV), col(C_CG),
                  _hist_spec(hist, layer, CONF_W - 1, BRANCH),
                  pl.BlockSpec((CONF_W, BRANCH), lambda i, j: (0, 0)), row, row, row],
        out_specs=[pl.BlockSpec((None, tt, BRANCH), lambda i, j: (i, j, 0)),
                   pl.BlockSpec((None, CONF_W - 1, BRANCH), lambda i, j: (i, 0, 0))],
        out_shape=[jax.ShapeDtypeStruct((b, t, BRANCH), z3.dtype),
                   jax.ShapeDtypeStruct((b, CONF_W - 1, BRANCH), F32)],
        scratch_shapes=scratch,
        compiler_params=_cparams(2), name="conformer")(
            z3, z3, hist, w, cb.reshape(1, BRANCH), lg.reshape(1, BRANCH), lb.reshape(1, BRANCH))


def _gdn_pre_kernel(x_ref, hist_ref, w_ref, q_ref, k_ref, v_ref, st_ref, ext_ref, *, tt):
    _conv_load_tile(x_ref[...].astype(F32), hist_ref, ext_ref, tt, GDN_W)
    rb = min(tt, 128)
    for r0 in range(0, tt, rb):
        for part, dst in enumerate((q_ref, k_ref, v_ref)):
            y = _silu(_conv_rows(ext_ref, w_ref, r0, rb, GDN_W, c0=part * BRANCH, ncols=BRANCH))
            if part == 2:
                dst[r0:r0 + rb, :] = y
                continue
            scale = GDN_DK ** -0.5 if part == 0 else 1.0
            for h in range(GDN_HEADS):
                blk = y[:, h * GDN_DK:(h + 1) * GDN_DK]
                nrm = lax.rsqrt(jnp.sum(blk * blk, axis=-1, keepdims=True) + 1e-6)
                dst[r0:r0 + rb, h * GDN_DK:(h + 1) * GDN_DK] = blk * nrm * scale
    _conv_state(ext_ref, st_ref, tt, GDN_W)


def _gdn_pre(z3, hist, w, tt, layer=0):
    b, t, _ = z3.shape
    wq = 3 * BRANCH
    out = pl.BlockSpec((None, tt, BRANCH), lambda i, j: (i, j, 0))
    return pl.pallas_call(
        functools.partial(_gdn_pre_kernel, tt=tt), grid=(b, t // tt),
        in_specs=[pl.BlockSpec((None, tt, wq), lambda i, j: (i, j, C_GQKV // wq)),
                  _hist_spec(hist, layer, GDN_W - 1, wq),
                  pl.BlockSpec((GDN_W, wq), lambda i, j: (0, 0))],
        out_specs=[out, out, out, pl.BlockSpec((None, GDN_W - 1, wq), lambda i, j: (i, 0, 0))],
        out_shape=[jax.ShapeDtypeStruct((b, t, BRANCH), F32)] * 3 + [jax.ShapeDtypeStruct((b, GDN_W - 1, wq), F32)],
        scratch_shapes=[pltpu.VMEM((_halo_rows(GDN_W) + tt, wq), F32)],
        compiler_params=_cparams(2), name="gdn_pre")(z3, hist, w)


def _fox_bias_placement():
    pq = np.zeros((SMALL, BRANCH), np.float32)
    pk = np.zeros((SMALL, BRANCH), np.float32)
    for h in range(FOX_HEADS):
        base = (h // 2) * LANES + (FOX_DH if h % 2 == 0 else 0)
        for part in range(3):
            pq[8 * (part + 1) + h, base + part] = 1.0
            pq[0, base + 3 + part] = 1.0
            pk[0, base + part] = 1.0
            pk[8 * (part + 1) + h, base + 3 + part] = -1.0
    return jnp.asarray(pq, BF16), jnp.asarray(pk, BF16)


def _gates_kernel(zs_ref, bias_ref, alog_ref, *rest, tt, with_t):
    if with_t:
        pq_ref, pk_ref, p_ref, pt_ref, qa_ref, ka_ref, carry_ref = rest
    else:
        p_ref, carry_ref = rest
    zs = zs_ref[...]
    lane = lax.broadcasted_iota(jnp.int32, zs.shape, 1)
    tb = zs + bias_ref[...]
    g_log = -jnp.exp(alog_ref[...]) * _softplus(tb)
    beta = jax.nn.sigmoid(zs)
    flog = -_softplus(-tb)
    is_f = (lane >= 8) & (lane < 16)
    f_only = jnp.where(is_f, flog, 0.0)
    if tt > 1:
        r = lax.broadcasted_iota(jnp.int32, (tt, tt), 0)
        c = lax.broadcasted_iota(jnp.int32, (tt, tt), 1)
        csum = _dot_sel(jnp.where(r >= c, 1.0, 0.0), f_only)
    else:
        csum = f_only

    @pl.when(pl.program_id(1) == 0)
    def _():
        carry_ref[...] = jnp.zeros_like(carry_ref)

    csum = csum + carry_ref[...]
    carry_ref[...] = csum[tt - 1:tt, :]
    p = jnp.where(lane < 4, g_log, jnp.where(lane < 8, beta, f_only)) + pltpu.roll(csum, 8, axis=1)
    p_ref[...] = p
    if with_t:
        pt_ref[...] = p.T
        hi, mid, lo = (piece.astype(F32) for piece in _split3(csum * LOG2E))
        c3 = (hi + pltpu.roll(mid, 8, axis=1) + pltpu.roll(lo, 16, axis=1) + jnp.where(lane == 0, 1.0, 0.0)).astype(BF16)
        qa_ref[...] = _dot(c3, pq_ref[...]).astype(BF16)
        ka_ref[...] = _dot(c3, pk_ref[...]).astype(BF16)


def _gates(zs3, bias_row, alog_row, tt):
    b, t, _ = zs3.shape
    with_t = tt % LANES == 0
    row = pl.BlockSpec((1, SMALL), lambda i, j: (0, 0))
    in_specs = [pl.BlockSpec((None, tt, SMALL), lambda i, j: (i, j, 0)), row, row]
    args = [zs3, bias_row, alog_row]
    out_specs = [pl.BlockSpec((None, tt, SMALL), lambda i, j: (i, j, 0))]
    out_shape = [jax.ShapeDtypeStruct((b, t, SMALL), F32)]
    if with_t:
        place = pl.BlockSpec((SMALL, BRANCH), lambda i, j: (0, 0))
        aug = pl.BlockSpec((None, tt, BRANCH), lambda i, j: (i, j, 0))
        in_specs += [place, place]
        args += list(_fox_bias_placement())
        out_specs += [pl.BlockSpec((None, SMALL, tt), lambda i, j: (i, 0, j)), aug, aug]
        out_shape += [jax.ShapeDtypeStruct((b, SMALL, t), F32)] + [jax.ShapeDtypeStruct((b, t, BRANCH), BF16)] * 2
    res = pl.pallas_call(
        functools.partial(_gates_kernel, tt=tt, with_t=with_t), grid=(b, t // tt),
        in_specs=in_specs, out_specs=out_specs, out_shape=out_shape,
        scratch_shapes=[pltpu.VMEM((1, SMALL), F32)],
        compiler_params=_cparams(2), name="gates")(*args)
    return res if with_t else (res[0], None, None, None)


def _unit_lower_inverse_many(ns, c):
    r = lax.broadcasted_iota(jnp.int32, (c, c), 0)
    q = lax.broadcasted_iota(jnp.int32, (c, c), 1)
    mm = lambda a, b: _dot(a.astype(BF16), b.astype(BF16))
    blk8 = (r >> 3) == (q >> 3)
    eye = jnp.where(r == q, 1.0, 0.0)
    n0s = [jnp.where(blk8, n, 0.0) for n in ns]
    xs = [eye - n0 for n0 in n0s]
    ms = [mm(n0, n0) for n0 in n0s]
    xs = [x + mm(x, m) for x, m in zip(xs, ms)]
    ms = [mm(m, m) for m in ms]
    xs = [x + mm(x, m) for x, m in zip(xs, ms)]
    s = 3
    while (1 << s) < c:
        lower_left = ((r >> (s + 1)) == (q >> (s + 1))) & ((r >> s) != (q >> s))
        ts = [mm(x, jnp.where(lower_left, n, 0.0)) for x, n in zip(xs, ns)]
        xs = [x - mm(t, x) for x, t in zip(xs, ts)]
        s += 1
    return xs


def _gdn_chunk_kernel(q_ref, k_ref, v_ref, p_ref, pt_ref, gz_ref, nw_ref, s0_ref, o_ref, so_ref,
                      s_ref, u_ref, w_ref, qd_ref, kdt_ref, qk_ref, *, tt):
    c = GDN_CHUNK
    nc = tt // c

    @pl.when(pl.program_id(1) == 0)
    def _():
        s_ref[...] = s0_ref[...]

    r = lax.broadcasted_iota(jnp.int32, (c, c), 0)
    q_i = lax.broadcasted_iota(jnp.int32, (c, c), 1)
    tril = r >= q_i
    strict = r > q_i
    tri_f = jnp.where(tril, 1.0, 0.0).astype(F32)
    triu_f = jnp.where(r <= q_i, 1.0, 0.0).astype(F32)
    probs = [(ci, h) for ci in range(nc) for h in range(GDN_HEADS)]
    rows_of = lambda ci: slice(ci * c, (ci + 1) * c)
    cols_of = lambda h: slice(h * GDN_DK, (h + 1) * GDN_DK)

    d_cols = [_dot_sel(tri_f, p_ref[rows_of(ci), :]) for ci in range(nc)]
    d_rows = [_dot_sel_rhs(pt_ref[:, rows_of(ci)], triu_f) for ci in range(nc)]
    ns, rhss, dls = [], [], []
    for ci, h in probs:
        rows, hs = rows_of(ci), cols_of(h)
        qc, kc, vc = q_ref[rows, hs], k_ref[rows, hs], v_ref[rows, hs]
        beta = p_ref[rows, 4 + h:5 + h]
        dcol = d_cols[ci][:, h:h + 1]
        drow = d_rows[ci][h:h + 1, :]
        decay = jnp.where(tril, jnp.exp(jnp.where(tril, dcol - drow, 0.0)), 0.0)
        kb = kc * beta
        kbf = kc.astype(BF16)
        ns.append(jnp.where(strict, _dot_nt(kb.astype(BF16), kbf) * decay, 0.0))
        qk_ref[ci * GDN_HEADS + h] = jnp.where(tril, _dot_nt(qc.astype(BF16), kbf) * decay, 0.0).astype(BF16)
        ed = jnp.exp(dcol)
        dlast = dcol[c - 1:c, :]
        qd_ref[rows, hs] = (qc * ed).astype(BF16)
        kdt_ref[ci * GDN_HEADS + h] = (kc * jnp.exp(dlast - dcol)).T.astype(BF16)
        rhss.append(jnp.concatenate([vc * beta, kb * ed], axis=1).astype(BF16))
        dls.append(jnp.exp(dlast))
    t_invs = _unit_lower_inverse_many(ns, c)
    for (ci, h), t_inv, rhs in zip(probs, t_invs, rhss):
        sol = _dot(t_inv.astype(BF16), rhs)
        u_ref[rows_of(ci), cols_of(h)] = sol[:, :GDN_DK]
        w_ref[rows_of(ci), cols_of(h)] = sol[:, GDN_DK:].astype(BF16)

    for ci in range(nc):
        rows = rows_of(ci)
        s_olds = [s_ref[h] for h in range(GDN_HEADS)]
        s_bfs = [s.astype(BF16) for s in s_olds]
        vns = [(u_ref[rows, cols_of(h)] - _dot(w_ref[rows, cols_of(h)], s_bfs[h])).astype(BF16)
               for h in range(GDN_HEADS)]
        for h in range(GDN_HEADS):
            hs = cols_of(h)
            idx = ci * GDN_HEADS + h
            o = _dot(qd_ref[rows, hs], s_bfs[h]) + _dot(qk_ref[idx], vns[h])
            s_ref[h] = s_olds[h] * dls[idx] + _dot(kdt_ref[idx], vns[h])
            o_ref[rows, hs] = (_rms(o, nw_ref[...]) * _silu(gz_ref[rows, hs].astype(F32))).astype(o_ref.dtype)
    so_ref[...] = s_ref[...]


def _gdn_chunked(q, k, v, p, pt, z3, nw, s0, tt):
    b, t, _ = q.shape
    blk = pl.BlockSpec((None, tt, BRANCH), lambda i, j: (i, j, 0))
    st = pl.BlockSpec((None, GDN_HEADS, GDN_DK, GDN_DK), lambda i, j: (i, 0, 0, 0))
    return pl.pallas_call(
        functools.partial(_gdn_chunk_kernel, tt=tt), grid=(b, t // tt),
        in_specs=[blk, blk, blk,
                  pl.BlockSpec((None, tt, SMALL), lambda i, j: (i, j, 0)),
                  pl.BlockSpec((None, SMALL, tt), lambda i, j: (i, 0, j)),
                  pl.BlockSpec((None, tt, BRANCH), lambda i, j: (i, j, C_GZ // BRANCH)),
                  pl.BlockSpec((1, GDN_DK), lambda i, j: (0, 0)), st],
        out_specs=[blk, st],
        out_shape=[jax.ShapeDtypeStruct((b, t, BRANCH), z3.dtype),
                   jax.ShapeDtypeStruct((b, GDN_HEADS, GDN_DK, GDN_DK), F32)],
        scratch_shapes=[pltpu.VMEM((GDN_HEADS, GDN_DK, GDN_DK), F32),
                        pltpu.VMEM((tt, BRANCH), F32), pltpu.VMEM((tt, BRANCH), BF16), pltpu.VMEM((tt, BRANCH), BF16),
                        pltpu.VMEM((tt // GDN_CHUNK * GDN_HEADS, GDN_DK, GDN_CHUNK), BF16),
                        pltpu.VMEM((tt // GDN_CHUNK * GDN_HEADS, GDN_CHUNK, GDN_CHUNK), BF16)],
        compiler_params=_cparams(2), name="gdn_chunked")(q, k, v, p, pt, z3, nw.reshape(1, GDN_DK), s0)


def _gdn_step_kernel(q_ref, k_ref, v_ref, p_ref, gz_ref, nw_ref, s0_ref, o_ref, so_ref):
    r8 = lax.broadcasted_iota(jnp.int32, (8, GDN_DK), 0)
    rr = lax.broadcasted_iota(jnp.int32, (GDN_DK, GDN_DK), 0)
    cc = lax.broadcasted_iota(jnp.int32, (GDN_DK, GDN_DK), 1)
    p = p_ref[...]
    for h in range(GDN_HEADS):
        hs = slice(h * GDN_DK, (h + 1) * GDN_DK)
        q, k, v = q_ref[:, hs], k_ref[:, hs], v_ref[:, hs]
        g = p[:, h:h + 1]
        beta = p[:, 4 + h:5 + h]
        a = jnp.exp(g)
        w = k * beta * a
        s_old = s0_ref[h]
        s_bf = s_old.astype(BF16)
        lhs = jnp.where(r8 == 0, jnp.broadcast_to(w, (8, GDN_DK)),
                        jnp.where(r8 == 1, jnp.broadcast_to(q * a, (8, GDN_DK)), 0.0))
        ws = _dot(lhs.astype(BF16), s_bf)
        v_new = v * beta - ws[0:1, :]
        qk = jnp.sum(q.astype(BF16).astype(F32) * k.astype(BF16).astype(F32), axis=-1, keepdims=True)
        o = ws[1:2, :] + qk.astype(BF16).astype(F32) * v_new.astype(BF16).astype(F32)
        kdiag = jnp.where(rr == cc, jnp.broadcast_to(k, (GDN_DK, GDN_DK)), 0.0)
        outer = _dot(kdiag.astype(BF16), jnp.broadcast_to(v_new, (GDN_DK, GDN_DK)).astype(BF16))
        so_ref[h] = s_old * a + outer
        on = _rms(o, nw_ref[...])
        o_ref[:, hs] = on * _silu(gz_ref[:, hs])


def _gdn_step(q, k, v, p, z3, nw, state, layer):
    b = q.shape[0]
    blk = pl.BlockSpec((None, 1, BRANCH), lambda i: (i, 0, 0))
    st = pl.BlockSpec((None, GDN_HEADS, GDN_DK, GDN_DK), lambda i: (i, 0, 0, 0))
    st_in = pl.BlockSpec((None, None, GDN_HEADS, GDN_DK, GDN_DK), lambda i: (i, layer, 0, 0, 0))
    s0 = state
    return pl.pallas_call(
        _gdn_step_kernel, grid=(b,),
        in_specs=[blk, blk, blk, pl.BlockSpec((None, 1, SMALL), lambda i: (i, 0, 0)),
                  pl.BlockSpec((None, 1, BRANCH), lambda i: (i, 0, C_GZ // BRANCH)),
                  pl.BlockSpec((1, GDN_DK), lambda i: (0, 0)), st_in],
        out_specs=[blk, st],
        out_shape=[jax.ShapeDtypeStruct((b, 1, BRANCH), F32),
                   jax.ShapeDtypeStruct((b, GDN_HEADS, GDN_DK, GDN_DK), F32)],
        compiler_params=_cparams(1), name="gdn_step")(q, k, v, p, z3, nw.reshape(1, GDN_DK), s0)


def _fox_kernel(q_ref, k_ref, v_ref, qa_ref, ka_ref, o_ref, m_ref, acc_ref, *, tq, tk):
    qi, kj = pl.program_id(1), pl.program_id(2)

    @pl.when(kj == 0)
    def _():
        m_ref[...] = jnp.full(m_ref.shape, NEG_INF, F32)
        acc_ref[...] = jnp.zeros(acc_ref.shape, F32)

    def step(on_diagonal):
        lane = lax.broadcasted_iota(jnp.int32, (1, LANES), 1)
        if on_diagonal:
            causal = (lax.broadcasted_iota(jnp.int32, (tq, tk), 1) <= lax.broadcasted_iota(jnp.int32, (tq, tk), 0))
        for h in range(FOX_HEADS):
            ps = slice((h // 2) * LANES, (h // 2 + 1) * LANES)
            own = (lane >= (h % 2) * FOX_DH) & (lane < (h % 2 + 1) * FOX_DH)
            q2 = (q_ref[:, ps].astype(F32) * (FOX_DH ** -0.5 * LOG2E)).astype(BF16)
            s = _dot_nt(jnp.where(own, q2, qa_ref[:, ps]), jnp.where(own, k_ref[:, ps].astype(BF16), ka_ref[:, ps]))
            if on_diagonal:
                s = jnp.where(causal, s, NEG_INF)
            m_old = m_ref[h]
            m_new = jnp.maximum(m_old, jnp.max(s, axis=-1, keepdims=True))
            pr = jnp.exp2(s - jnp.concatenate([m_new] * (tk // LANES), axis=1))
            v1 = jnp.where(own, v_ref[:, ps].astype(BF16), jnp.ones((), BF16))
            acc_ref[h] = acc_ref[h] * jnp.exp2(m_old - m_new) + _dot(pr.astype(BF16), v1)
            m_ref[h] = m_new

    @pl.when(kj < qi)
    def _():
        step(False)

    @pl.when(kj == qi)
    def _():
        step(True)
        first = lax.broadcasted_iota(jnp.int32, (1, LANES), 1) < FOX_DH
        for pair in range(FOX_HEADS // 2):
            a0, a1 = acc_ref[2 * pair], acc_ref[2 * pair + 1]
            o0 = a0 / pltpu.roll(a0, FOX_DH, axis=1)
            o1 = a1 / pltpu.roll(a1, FOX_DH, axis=1)
            o_ref[:, pair * LANES:(pair + 1) * LANES] = jnp.where(first, o0, o1).astype(o_ref.dtype)


def _fox_prompt(z3, qa, ka, tq):
    b, t, _ = z3.shape
    tk = tq
    nq = t // tq
    kv = lambda c: pl.BlockSpec((None, tk, BRANCH), lambda i, qi, kj, c=c: (i, jnp.minimum(kj, qi), c // BRANCH))
    return pl.pallas_call(
        functools.partial(_fox_kernel, tq=tq, tk=tk), grid=(b, nq, nq),
        in_specs=[pl.BlockSpec((None, tq, BRANCH), lambda i, qi, kj: (i, qi, C_FQ // BRANCH)),
                  kv(C_FK), kv(C_FV),
                  pl.BlockSpec((None, tq, BRANCH), lambda i, qi, kj: (i, qi, 0)),
                  pl.BlockSpec((None, tk, BRANCH), lambda i, qi, kj: (i, jnp.minimum(kj, qi), 0))],
        out_specs=pl.BlockSpec((None, tq, BRANCH), lambda i, qi, kj: (i, qi, 0)),
        out_shape=jax.ShapeDtypeStruct((b, t, BRANCH), z3.dtype),
        scratch_shapes=[pltpu.VMEM((FOX_HEADS, tq, LANES), F32), pltpu.VMEM((FOX_HEADS, tq, LANES), F32)],
        compiler_params=_cparams(3), name="fox_prompt")(z3, z3, z3, qa, ka)


def _fox_decode_kernel(pt_ref, q_ref, kc_ref, vc_ref, fc_ref, *refs):
    del pt_ref
    n = PAGES_PER_STEP
    k_refs, v_refs, lf_refs = refs[:n], refs[n:2 * n], refs[2 * n:3 * n]
    o_ref, qb_ref, m_ref, l_ref, acc_ref, run_ref = refs[3 * n:]
    j = pl.program_id(1)
    nj = pl.num_programs(1)
    lane = lax.broadcasted_iota(jnp.int32, (1, LANES), 1)
    row8 = lax.broadcasted_iota(jnp.int32, (FOX_HEADS, LANES), 0)

    def to_cols(row):
        return jnp.broadcast_to(row, (LANES, BRANCH)).T

    def head_sum(x):
        out = jnp.zeros((FOX_HEADS, LANES), F32)
        for h in range(FOX_HEADS):
            sh = jnp.sum(x[h * FOX_DH:(h + 1) * FOX_DH, :], axis=0, keepdims=True)
            out = jnp.where(row8 == h, jnp.broadcast_to(sh, (FOX_HEADS, LANES)), out)
        return out

    @pl.when(j == 0)
    def _():
        qb_ref[...] = to_cols(q_ref[...] * (FOX_DH ** -0.5))
        m_ref[...] = head_sum(qb_ref[...] * to_cols(kc_ref[...]))
        l_ref[...] = jnp.ones(l_ref.shape, F32)
        acc_ref[...] = jnp.where(lane == 0, to_cols(vc_ref[...]), 0.0)
        run_ref[...] = fc_ref[...]

    qb = qb_ref[...]
    lf = jnp.concatenate([lf_refs[i][...] for i in range(n)], axis=0)
    pr_ = lax.broadcasted_iota(jnp.int32, (LANES, LANES), 0)
    pc_ = lax.broadcasted_iota(jnp.int32, (LANES, LANES), 1)
    in_page = _dot_sel_rhs(lf, jnp.where(pr_ > pc_, 1.0, 0.0))
    tot = jnp.sum(lf, axis=-1, keepdims=True)
    run = run_ref[...]
    logits = [None] * n
    for i in reversed(range(n)):
        s_i = head_sum(k_refs[i][...].reshape(BRANCH, LANES) * qb)
        logits[i] = s_i + in_page[8 * i:8 * (i + 1), :] + run
        run = run + tot[8 * i:8 * (i + 1), :]
    run_ref[...] = run
    mx = logits[0]
    for i in range(1, n):
        mx = jnp.maximum(mx, logits[i])
    m_old = m_ref[...]
    m_new = jnp.maximum(m_old, jnp.max(mx, axis=-1, keepdims=True))
    alpha = jnp.exp(m_old - m_new)
    ps = [jnp.exp(logits[i] - m_new) for i in range(n)]
    psum = ps[0]
    for i in range(1, n):
        psum = psum + ps[i]
    l_ref[...] = alpha * l_ref[...] + jnp.sum(psum, axis=-1, keepdims=True)
    m_ref[...] = m_new
    for h in range(FOX_HEADS):
        hs = slice(h * FOX_DH, (h + 1) * FOX_DH)
        a = acc_ref[hs, :] * alpha[h:h + 1, :]
        for i in range(n):
            a = a + v_refs[i][h] * ps[i][h:h + 1, :]
        acc_ref[hs, :] = a

    @pl.when(j == nj - 1)
    def _():
        l_all = l_ref[...]
        l_cols = jnp.concatenate([jnp.broadcast_to(l_all[h:h + 1, :], (FOX_DH, LANES)) for h in range(FOX_HEADS)],
                                 axis=0)
        o_cols = jnp.broadcast_to(jnp.sum(acc_ref[...], axis=-1, keepdims=True), (BRANCH, LANES)) / l_cols
        o_ref[...] = o_cols.T[0:1, :]


def _fox_decode(page_table, q, kc, vc, fc, cache_kt, cache_vt, cache_lft, layer):
    b, n_pages = page_table.shape
    nj = n_pages // PAGES_PER_STEP
    tok = pl.BlockSpec((None, 1, BRANCH), lambda i, j, pt: (i, 0, 0))
    pidx = lambda i, j, pt, k: pt[i, (nj - 1 - j) * PAGES_PER_STEP + k]
    page = lambda k: pl.BlockSpec((None, None, FOX_HEADS, FOX_DH, PAGE),
                                  lambda i, j, pt, k=k: (pidx(i, j, pt, k), layer, 0, 0, 0))
    lpage = lambda k: pl.BlockSpec((None, None, FOX_HEADS, PAGE),
                                   lambda i, j, pt, k=k: (pidx(i, j, pt, k), layer, 0, 0))
    rng = range(PAGES_PER_STEP)
    in_specs = ([tok, tok, tok, pl.BlockSpec((None, FOX_HEADS, LANES), lambda i, j, pt: (i, 0, 0))]
                + [page(k) for k in rng] * 2 + [lpage(k) for k in rng])
    return pl.pallas_call(
        _fox_decode_kernel,
        grid_spec=pltpu.PrefetchScalarGridSpec(
            num_scalar_prefetch=1, grid=(b, nj), in_specs=in_specs, out_specs=tok,
            scratch_shapes=[pltpu.VMEM((BRANCH, LANES), F32), pltpu.VMEM((FOX_HEADS, LANES), F32),
                            pltpu.VMEM((FOX_HEADS, LANES), F32), pltpu.VMEM((BRANCH, LANES), F32),
                            pltpu.VMEM((FOX_HEADS, LANES), F32)]),
        out_shape=jax.ShapeDtypeStruct((b, 1, BRANCH), F32),
        compiler_params=_cparams(2), name="fox_decode")(
            page_table, q, kc, vc, fc, *([cache_kt] * PAGES_PER_STEP), *([cache_vt] * PAGES_PER_STEP),
            *([cache_lft] * PAGES_PER_STEP))


def _merge_kernel(oa, ob, oc, od, g0, g1, g2, g3, x_ref, wb_ref, wo_ref, out_ref):
    merged = None
    for i, (o_ref, g_ref) in enumerate(((oa, g0), (ob, g1), (oc, g2), (od, g3))):
        term = jax.nn.sigmoid(g_ref[...].astype(F32)) * _dot(o_ref[...].astype(BF16), wb_ref[i])
        merged = term if merged is None else merged + term
    out_ref[...] = x_ref[...] + _dot(merged.astype(BF16), wo_ref[...])


def _merge(o_list, z2, x2, wb, wo, tm):
    m = x2.shape[0]
    o_spec = pl.BlockSpec((tm, BRANCH), lambda i: (i, 0))
    gate = lambda g: pl.BlockSpec((tm, D_MODEL), lambda i, g=g: (i, C_GATE // D_MODEL + g))
    xs = pl.BlockSpec((tm, D_MODEL), lambda i: (i, 0))
    return pl.pallas_call(
        _merge_kernel, grid=(m // tm,),
        in_specs=[o_spec] * 4 + [gate(g) for g in range(4)] + [
            xs, pl.BlockSpec((4, BRANCH, D_MODEL), lambda i: (0, 0, 0)),
            pl.BlockSpec((D_MODEL, D_MODEL), lambda i: (0, 0))],
        out_specs=xs, out_shape=jax.ShapeDtypeStruct((m, D_MODEL), F32),
        compiler_params=_cparams(1), name="merge")(*o_list, z2, z2, z2, z2, x2, wb, wo)


def _cross_kernel(x_ref, g_ref, wq_ref, mk_ref, mv_ref, wo_ref, out_ref):
    x = x_ref[...]
    q = _dot(_rms(x, g_ref[...]).astype(BF16), wq_ref[...])
    mk = mk_ref[...].astype(BF16)
    mv = mv_ref[...].astype(BF16)
    outs = []
    for h in range(X_HEADS):
        hs = slice(h * X_DH, (h + 1) * X_DH)
        s = _dot_nt(q[:, hs].astype(BF16), mk[:, hs]) * (X_DH ** -0.5)
        e = jnp.exp(s - jnp.max(s, axis=-1, keepdims=True))
        pr = e / jnp.sum(e, axis=-1, keepdims=True)
        outs.append(_dot(pr.astype(BF16), mv[:, hs]))
    out_ref[...] = x + _dot(jnp.concatenate(outs, axis=1).astype(BF16), wo_ref[...])


def _cross(x3, g, wq, mk_arr, mk_spec, mv_arr, mv_spec, wo, tt):
    b, t, _ = x3.shape
    xw = X_HEADS * X_DH
    xs = pl.BlockSpec((None, tt, D_MODEL), lambda i, j: (i, j, 0))
    return pl.pallas_call(
        _cross_kernel, grid=(b, t // tt),
        in_specs=[xs, pl.BlockSpec((1, D_MODEL), lambda i, j: (0, 0)),
                  pl.BlockSpec((D_MODEL, xw), lambda i, j: (0, 0)), mk_spec, mv_spec,
                  pl.BlockSpec((xw, D_MODEL), lambda i, j: (0, 0))],
        out_specs=xs, out_shape=jax.ShapeDtypeStruct((b, t, D_MODEL), F32),
        compiler_params=_cparams(2), name="cross_attn")(x3, g.reshape(1, D_MODEL), wq, mk_arr, mv_arr, wo)


def _mem_attn_step_kernel(q_ref, km_ref, vm_ref, o_ref):
    q = q_ref[...]
    row8 = lax.broadcasted_iota(jnp.int32, (8, X_DH), 0)
    q8 = jnp.zeros((8, X_DH), F32)
    for h in range(X_HEADS):
        q8 = jnp.where(row8 == h, jnp.broadcast_to(q[:, h * X_DH:(h + 1) * X_DH], (8, X_DH)), q8)
    n = MEM_LEN * X_HEADS
    s = _dot_nt(q8.astype(BF16), km_ref[...].astype(BF16)) * (X_DH ** -0.5)
    col = lax.broadcasted_iota(jnp.int32, (8, n), 1)
    row = lax.broadcasted_iota(jnp.int32, (8, n), 0)
    s = jnp.where((col & (X_HEADS - 1)) == row, s, NEG_INF)
    e = jnp.exp(s - jnp.max(s, axis=-1, keepdims=True))
    pr = e / jnp.sum(e, axis=-1, keepdims=True)
    o8 = _dot(pr.astype(BF16), vm_ref[...].astype(BF16))
    o_ref[...] = jnp.concatenate([o8[h:h + 1, :] for h in range(X_HEADS)], axis=1)


def _mem_attn_step(q3, km, vm, layer):
    b = q3.shape[0]
    xw = X_HEADS * X_DH
    tok = pl.BlockSpec((None, 1, xw), lambda i: (i, 0, 0))
    mem = pl.BlockSpec((None, None, MEM_LEN * X_HEADS, X_DH), lambda i: (i, layer, 0, 0))
    return pl.pallas_call(
        _mem_attn_step_kernel, grid=(b,), in_specs=[tok, mem, mem], out_specs=tok,
        out_shape=jax.ShapeDtypeStruct((b, 1, xw), F32),
        compiler_params=_cparams(1), name="mem_attn_step")(q3, km, vm)


def _proj_res_kernel(a_ref, w_ref, x_ref, o_ref):
    o_ref[...] = x_ref[...] + _dot(a_ref[...].astype(BF16), w_ref[...])


def _proj_res(a, w, x):
    m, k = a.shape
    n = w.shape[1]
    full = lambda r, c: pl.BlockSpec((r, c), lambda i: (0, 0))
    return pl.pallas_call(
        _proj_res_kernel, grid=(1,), in_specs=[full(m, k), full(k, n), full(m, n)], out_specs=full(m, n),
        out_shape=jax.ShapeDtypeStruct((m, n), F32), compiler_params=_cparams(1), name="proj_res")(a, w, x)


def _ffn_kernel(x_ref, g_ref, wg_ref, wu_ref, cw_ref, wd_ref, hist_ref, fg_ref, out_ref, st_ref,
                xn_ref, acc_ref, halo_ref, ext_ref, *, tt, tf, nf, final_norm):
    t, f = pl.program_id(1), pl.program_id(2)
    hp = _halo_rows(FFN_W)

    @pl.when(f == 0)
    def _():
        xn_ref[...] = _rms(x_ref[...], g_ref[...]).astype(BF16)
        acc_ref[...] = jnp.zeros(acc_ref.shape, F32)

    @pl.when(t == 0)
    def _():
        ext_ref[0:hp, :] = jnp.zeros((hp, ext_ref.shape[1]), F32)
        ext_ref[hp - (FFN_W - 1):hp, :] = hist_ref[...]

    @pl.when(t > 0)
    def _():
        ext_ref[0:hp, :] = halo_ref[f]

    xn = xn_ref[...]
    blocks = [(c0, min(512, tf - c0)) for c0 in range(0, tf, 512)]
    n = len(blocks)
    us, acts, down = {}, {}, None
    for s in range(n + 2):
        if s < n:
            c0, w = blocks[s]
            ext_ref[hp:hp + tt, c0:c0 + w] = _dot(xn, wg_ref[:, c0:c0 + w])
            us[s] = _dot(xn, wu_ref[:, c0:c0 + w])
        if 1 <= s <= n:
            c0, w = blocks[s - 1]
            gc = _conv_rows(ext_ref, cw_ref, 0, tt, FFN_W, c0=c0, ncols=w)
            acts[s - 1] = (_silu(gc) * us.pop(s - 1)).astype(BF16)
        if s >= 2:
            c0, w = blocks[s - 2]
            part = _dot(acts.pop(s - 2), wd_ref[c0:c0 + w, :])
            down = part if down is None else down + part
    halo_ref[f] = ext_ref[tt:tt + hp, :]
    acc_ref[...] += down

    @pl.when(f == nf - 1)
    def _():
        y = x_ref[...] + acc_ref[...]
        out_ref[...] = _rms(y, fg_ref[...]) if final_norm else y

    @pl.when((f == nf - 1) & (t == pl.num_programs(1) - 1))
    def _():
        for ff in range(nf):
            st_ref[:, ff * tf:(ff + 1) * tf] = halo_ref[ff][hp - (FFN_W - 1):hp, :]


def _ffn(x3, g, w_in, cw, w_out, hist, final_g, final_norm, tt, tf):
    b, t, _ = x3.shape
    nf = D_FF // tf
    xs = pl.BlockSpec((None, tt, D_MODEL), lambda i, j, f: (i, j, 0))
    st = pl.BlockSpec((None, FFN_W - 1, tf), lambda i, j, f: (i, 0, f))
    st_out = pl.BlockSpec((None, FFN_W - 1, D_FF), lambda i, j, f: (i, 0, 0))
    return pl.pallas_call(
        functools.partial(_ffn_kernel, tt=tt, tf=tf, nf=nf, final_norm=final_norm), grid=(b, t // tt, nf),
        in_specs=[xs, pl.BlockSpec((1, D_MODEL), lambda i, j, f: (0, 0)),
                  pl.BlockSpec((D_MODEL, tf), lambda i, j, f: (0, f)),
                  pl.BlockSpec((D_MODEL, tf), lambda i, j, f: (0, nf + f)),
                  pl.BlockSpec((FFN_W, tf), lambda i, j, f: (0, f)),
                  pl.BlockSpec((tf, D_MODEL), lambda i, j, f: (f, 0)), st,
                  pl.BlockSpec((1, D_MODEL), lambda i, j, f: (0, 0))],
        out_specs=[xs, st_out],
        out_shape=[jax.ShapeDtypeStruct((b, t, D_MODEL), F32), jax.ShapeDtypeStruct((b, FFN_W - 1, D_FF), F32)],
        scratch_shapes=[pltpu.VMEM((tt, D_MODEL), BF16), pltpu.VMEM((tt, D_MODEL), F32),
                        pltpu.VMEM((nf, _halo_rows(FFN_W), tf), F32),
                        pltpu.VMEM((_halo_rows(FFN_W) + tt, tf), F32)],
        compiler_params=_cparams(3), name="conv_ffn")(
            x3, g.reshape(1, D_MODEL), w_in, w_in, cw, w_out, hist, final_g.reshape(1, D_MODEL))


def _ffn_step_kernel(x_ref, g_ref, wg_ref, wu_ref, cw_ref, wd_ref, h0_ref, h1_ref, out_ref, s0_ref, s1_ref,
                     xn_ref, acc_ref):
    f = pl.program_id(0)

    @pl.when(f == 0)
    def _():
        xn_ref[...] = _rms(x_ref[...], g_ref[...]).astype(BF16)
        acc_ref[...] = jnp.zeros(acc_ref.shape, F32)

    gcol = _dot(xn_ref[...], wg_ref[...])
    u = _dot(xn_ref[...], wu_ref[...])
    gc = cw_ref[0:1, :] * h0_ref[...] + cw_ref[1:2, :] * h1_ref[...] + cw_ref[2:3, :] * gcol
    s0_ref[...] = h1_ref[...]
    s1_ref[...] = gcol
    acc_ref[...] += _dot((_silu(gc) * u).astype(BF16), wd_ref[...])

    @pl.when(f == pl.num_programs(0) - 1)
    def _():
        out_ref[...] = x_ref[...] + acc_ref[...]


def _ffn_step(x2, g, w_in, cw, w_out, h0, h1, tf):
    b = x2.shape[0]
    nf = D_FF // tf
    xs = pl.BlockSpec((b, D_MODEL), lambda f: (0, 0))
    hs = pl.BlockSpec((b, tf), lambda f: (0, f))
    return pl.pallas_call(
        _ffn_step_kernel, grid=(nf,),
        in_specs=[xs, pl.BlockSpec((1, D_MODEL), lambda f: (0, 0)),
                  pl.BlockSpec((D_MODEL, tf), lambda f: (0, f)),
                  pl.BlockSpec((D_MODEL, tf), lambda f: (0, nf + f)),
                  pl.BlockSpec((FFN_W, tf), lambda f: (0, f)),
                  pl.BlockSpec((tf, D_MODEL), lambda f: (f, 0)), hs, hs],
        out_specs=[xs, hs, hs],
        out_shape=[jax.ShapeDtypeStruct((b, D_MODEL), F32), jax.ShapeDtypeStruct((b, D_FF), F32),
                   jax.ShapeDtypeStruct((b, D_FF), F32)],
        scratch_shapes=[pltpu.VMEM((b, D_MODEL), BF16), pltpu.VMEM((b, D_MODEL), F32)],
        compiler_params=_cparams(1), name="conv_ffn_step")(x2, g.reshape(1, D_MODEL), w_in, w_in, cw, w_out, h0, h1)


def _kv_out_kernel(k_ref, v_ref, *refs):
    kt_ref, vt_ref = refs[-2:]
    kt_ref[...] = k_ref[...].astype(F32).T
    vt_ref[...] = v_ref[...].astype(F32).T


def _kv_out(z3, layer, depth, prev, tt):
    b, t, _ = z3.shape
    col = lambda c: pl.BlockSpec((None, tt, BRANCH), lambda i, j, c=c: (i, j, c // BRANCH))
    out = pl.BlockSpec((None, None, BRANCH, tt), lambda i, j: (i, layer, 0, j))
    shape = jax.ShapeDtypeStruct((b, depth, BRANCH, t), F32)
    in_specs, args, aliases = [col(C_FK), col(C_FV)], [z3, z3], {}
    if prev is not None:
        in_specs += [pl.BlockSpec(memory_space=pl.ANY)] * 2
        args += list(prev)
        aliases = {2: 0, 3: 1}
    return pl.pallas_call(
        _kv_out_kernel, grid=(b, t // tt), in_specs=in_specs, out_specs=[out, out], out_shape=[shape, shape],
        input_output_aliases=aliases, compiler_params=_cparams(2), name="kv_out")(*args)


def _final_norm_kernel(x_ref, g_ref, o_ref):
    o_ref[...] = _rms(x_ref[...], g_ref[...])


def _final_norm(x2, g, tm):
    m = x2.shape[0]
    xs = pl.BlockSpec((tm, D_MODEL), lambda i: (i, 0))
    return pl.pallas_call(
        _final_norm_kernel, grid=(m // tm,),
        in_specs=[xs, pl.BlockSpec((1, D_MODEL), lambda i: (0, 0))], out_specs=xs,
        out_shape=jax.ShapeDtypeStruct((m, D_MODEL), F32),
        compiler_params=_cparams(1), name="final_norm")(x2, g.reshape(1, D_MODEL))


def _prep_layer(l, w_in, w_branch, w_out, w_cq, w_co, w_ffn_in, w_ffn_out, gdn_a_log, gdn_dt_bias, fox_fbias):
    w = w_in[l]
    w_main = jnp.concatenate([w[:, :3584], w[:, 3592:6152], w[:, 6160:]], axis=1).astype(BF16)
    w_small = jnp.concatenate([w[:, 3584:3592], w[:, 6152:6160],
                               jnp.zeros((D_MODEL, SMALL - 16), F32)], axis=1).astype(BF16)
    zeros4 = jnp.zeros((4,), F32)
    bias_row = jnp.concatenate([gdn_dt_bias[l], zeros4, fox_fbias[l], jnp.zeros((SMALL - 16,), F32)]).reshape(1, SMALL)
    alog_row = jnp.concatenate([gdn_a_log[l], jnp.zeros((SMALL - 4,), F32)]).reshape(1, SMALL)
    return dict(w_main=w_main, w_small=w_small, bias_row=bias_row, alog_row=alog_row,
                wb=w_branch[l].astype(BF16), wo=w_out[l].astype(BF16), wcq=w_cq[l].astype(BF16),
                wco=w_co[l].astype(BF16), wfi=w_ffn_in[l].astype(BF16), wfo=w_ffn_out[l].astype(BF16))


def kernel(x_prompt, x_sample, cache_fox_k, cache_fox_v, cache_fox_logf, cache_mem_k, cache_mem_v, state_sconv, state_gdn_conv, state_gdn, state_conf_conv, state_ffn_conv, page_table, mem_prompt, norm_mix, w_in, w_branch, w_out, sconv_w, gdn_conv_w, gdn_a_log, gdn_dt_bias, gdn_norm, conf_dw_w, conf_dw_b, conf_ln_g, conf_ln_b, fox_fbias, norm_cross, norm_mem, w_cq, w_ckv, w_co, norm_ffn, w_ffn_in, ffn_conv_w, w_ffn_out, norm_final):
    depth = w_in.shape[0]
    bp, t, _ = x_prompt.shape
    bs = x_sample.shape[0]
    n_pool = cache_fox_k.shape[0]
    xw = X_HEADS * X_DH
    prep = [_prep_layer(l, w_in, w_branch, w_out, w_cq, w_co, w_ffn_in, w_ffn_out, gdn_a_log, gdn_dt_bias, fox_fbias)
            for l in range(depth)]

    tt = min(t, 512)
    mp = bp * t
    x = x_prompt.reshape(mp, D_MODEL)
    mem2 = mem_prompt.reshape(bp * MEM_LEN, D_MODEL)
    p_out = {k: [] for k in ("fk", "fv", "fl", "mk", "mv", "sc", "gc", "gs", "cc", "ff")}
    zero = lambda *s: jnp.zeros(s, F32)
    kt_vt = None
    for l in range(depth):
        w = prep[l]
        z, zs = _rms_matmul(x, norm_mix[l], w["w_main"], w["w_small"], tm=min(mp, 2048), tn=1024, out_dtype=BF16)
        z3 = z.reshape(bp, t, NZ)
        o_a, st_sc = _sconv(z3, zero(bp, SC_W - 1, BRANCH), sconv_w[l], tt)
        q, k, v, st_gc = _gdn_pre(z3, zero(bp, GDN_W - 1, 3 * BRANCH), gdn_conv_w[l], tt)
        p, pt, qa, ka = _gates(zs.reshape(bp, t, SMALL), w["bias_row"], w["alog_row"], tt)
        o_b, st_gs = _gdn_chunked(q, k, v, p, pt, z3, gdn_norm[l], zero(bp, GDN_HEADS, GDN_DK, GDN_DK), tt)
        o_c, st_cc = _conformer(z3, zero(bp, CONF_W - 1, BRANCH), conf_dw_w[l], conf_dw_b[l],
                                conf_ln_g[l], conf_ln_b[l], tt)
        o_d = _fox_prompt(z3, qa, ka, tt)
        o_list = [o.reshape(mp, BRANCH) for o in (o_a, o_b, o_c, o_d)]
        x = _merge(o_list, z, x, w["wb"], w["wo"], tm=min(mp, 512))
        kv = _rms_matmul(mem2, norm_mem[l], w_ckv[l].astype(BF16), None, tm=min(bp * MEM_LEN, 1024), tn=512)
        kv3 = kv.reshape(bp, MEM_LEN, 2 * xw)
        x = _cross(x.reshape(bp, t, D_MODEL), norm_cross[l], w["wcq"],
                   kv3, pl.BlockSpec((None, MEM_LEN, xw), lambda i, j: (i, 0, 0)),
                   kv3, pl.BlockSpec((None, MEM_LEN, xw), lambda i, j: (i, 0, 1)), w["wco"], tt)
        x, st_ff = _ffn(x, norm_ffn[l], w["wfi"], ffn_conv_w[l], w["wfo"], zero(bp, FFN_W - 1, D_FF),
                        norm_final, l == depth - 1, tt=min(t, 512), tf=D_FF // 2)
        x = x.reshape(mp, D_MODEL)
        kt_vt = _kv_out(z3, l, depth, kt_vt, tt)
        p_out["fl"].append(pt[:, 8:16, :])
        p_out["mk"].append(kv3[:, :, :xw])
        p_out["mv"].append(kv3[:, :, xw:])
        for key, val in (("sc", st_sc), ("gc", st_gc), ("gs", st_gs), ("cc", st_cc), ("ff", st_ff)):
            p_out[key].append(val)
    y_prompt = x.reshape(bp, t, D_MODEL)

    ckt = jnp.transpose(cache_fox_k, (0, 2, 3, 4, 1))
    cvt = jnp.transpose(cache_fox_v, (0, 2, 3, 4, 1))
    clt = jnp.transpose(cache_fox_logf, (0, 2, 3, 1))
    cmk = cache_mem_k.reshape(bs, depth, MEM_LEN * X_HEADS, X_DH)
    cmv = cache_mem_v.reshape(bs, depth, MEM_LEN * X_HEADS, X_DH)
    xs_ = x_sample.reshape(bs, D_MODEL)
    s_out = {k: [] for k in ("fk", "fv", "fl", "sc", "gc", "gs", "cc", "ff")}
    for l in range(depth):
        w = prep[l]
        z, zs = _rms_matmul(xs_, norm_mix[l], w["w_main"], w["w_small"], tm=bs, tn=2048)
        z3 = z.reshape(bs, 1, NZ)
        o_a, st_sc = _sconv(z3, state_sconv, sconv_w[l], 1, layer=l)
        q, k, v, st_gc = _gdn_pre(z3, state_gdn_conv, gdn_conv_w[l], 1, layer=l)
        p = _gates(zs.reshape(bs, 1, SMALL), w["bias_row"], w["alog_row"], 1)[0]
        o_b, st_gs = _gdn_step(q, k, v, p, z3, gdn_norm[l], state_gdn, l)
        o_c, st_cc = _conformer(z3, state_conf_conv, conf_dw_w[l], conf_dw_b[l], conf_ln_g[l], conf_ln_b[l], 1,
                                layer=l)
        fq, fk, fv = (z3[:, :, c:c + BRANCH] for c in (C_FQ, C_FK, C_FV))
        flog = p[:, :, 8:16]
        fc = jnp.broadcast_to(jnp.swapaxes(flog, 1, 2), (bs, FOX_HEADS, LANES))
        o_d = _fox_decode(page_table, fq, fk, fv, fc, ckt, cvt, clt, l)
        o_list = [o.reshape(bs, BRANCH) for o in (o_a, o_b, o_c, o_d)]
        xs_ = _merge(o_list, z, xs_, w["wb"], w["wo"], tm=bs)
        q3 = _rms_matmul(xs_, norm_cross[l], w["wcq"], None, tm=bs, tn=xw).reshape(bs, 1, xw)
        o3 = _mem_attn_step(q3, cmk, cmv, l)
        xs_ = _proj_res(o3.reshape(bs, xw), w["wco"], xs_)
        xs_, h_a, h_b = _ffn_step(xs_, norm_ffn[l], w["wfi"], ffn_conv_w[l], w["wfo"],
                                  state_ffn_conv[:, l, 0], state_ffn_conv[:, l, 1], tf=256)
        s_out["fk"].append(fk)
        s_out["fv"].append(fv)
        s_out["fl"].append(flog)
        for key, val in (("sc", st_sc), ("gc", st_gc), ("gs", st_gs), ("cc", st_cc),
                         ("ff", jnp.stack([h_a, h_b], axis=1))):
            s_out[key].append(val)
    y_sample = _final_norm(xs_, norm_final, tm=bs).reshape(bs, 1, D_MODEL)

    heads = lambda a: a.reshape(a.shape[0], a.shape[1], depth, FOX_HEADS, FOX_DH)
    memh = lambda a: a.reshape(bp, depth, MEM_LEN, X_HEADS, X_DH)
    from_pos_minor = lambda a: jnp.transpose(a.reshape(bp, depth, FOX_HEADS, FOX_DH, t), (0, 4, 1, 2, 3))
    return (y_prompt, y_sample,
            from_pos_minor(kt_vt[0]), from_pos_minor(kt_vt[1]),
            jnp.transpose(jnp.stack(p_out["fl"], axis=1), (0, 3, 1, 2)),
            memh(jnp.stack(p_out["mk"], axis=1)), memh(jnp.stack(p_out["mv"], axis=1)),
            jnp.stack(p_out["sc"], axis=1), jnp.stack(p_out["gc"], axis=1), jnp.stack(p_out["gs"], axis=1),
            jnp.stack(p_out["cc"], axis=1), jnp.stack(p_out["ff"], axis=1),
            heads(jnp.stack(s_out["fk"], axis=2)), heads(jnp.stack(s_out["fv"], axis=2)),
            jnp.stack(s_out["fl"], axis=2),
            jnp.stack(s_out["sc"], axis=1), jnp.stack(s_out["gc"], axis=1), jnp.stack(s_out["gs"], axis=1),
            jnp.stack(s_out["cc"], axis=1), jnp.stack(s_out["ff"], axis=1))
```

```python
import functools

import jax
import jax.numpy as jnp
import numpy as np
from jax import lax
from jax.experimental import pallas as pl
from jax.experimental.pallas import tpu as pltpu

F32 = jnp.float32
BF16 = jnp.bfloat16

D_MODEL = 1024
BRANCH = 512
GDN_HEADS = 4
GDN_DK = 128
FOX_HEADS = 8
FOX_DH = 64
X_HEADS = 4
X_DH = 128
D_FF = 2816
MEM_LEN = 256
PAGE = 128
SC_W, GDN_W, CONF_W, FFN_W = 3, 4, 31, 3
NEG_INF = -1e30
LOG2E = 1.4426950408889634

C_AH, C_AB, C_AC = 0, 512, 1024
C_GQKV = 1536
C_GZ = 3072
C_CV, C_CG = 3584, 4096
C_FQ, C_FK, C_FV = 4608, 5120, 5632
C_GATE = 6144
NZ = 10240
SMALL = 128
LANES = 128
GDN_CHUNK = 128
PAGES_PER_STEP = 16


def _cparams(n_axes, vmem_mb=48):
    return pltpu.CompilerParams(dimension_semantics=("arbitrary",) * n_axes,
                                vmem_limit_bytes=vmem_mb * 1024 * 1024)


def _dot(a, b):
    return jnp.dot(a, b, preferred_element_type=F32)


def _dot_nt(a, b):
    return lax.dot_general(a, b, (((1,), (1,)), ((), ())), preferred_element_type=F32)


def _split3(x):
    hi = x.astype(BF16)
    r1 = x - hi.astype(F32)
    mid = r1.astype(BF16)
    return hi, mid, (r1 - mid.astype(F32)).astype(BF16)


def _dot_sel(sel01, x):
    s = sel01.astype(BF16)
    hi, mid, lo = _split3(x)
    return _dot(s, hi) + _dot(s, mid) + _dot(s, lo)


def _dot_sel_rhs(x, sel01):
    s = sel01.astype(BF16)
    hi, mid, lo = _split3(x)
    return _dot(hi, s) + _dot(mid, s) + _dot(lo, s)


def _rms(x, g, eps=1e-6):
    return (x * lax.rsqrt(jnp.mean(x * x, axis=-1, keepdims=True) + eps)) * g


def _softplus(t):
    return jnp.maximum(t, 0.0) + jnp.log1p(jnp.exp(-jnp.abs(t)))


def _silu(t):
    return t * jax.nn.sigmoid(t)


def _rms_mm_kernel(*refs, has_small, w_rows_are_outputs):
    if has_small:
        x_ref, g_ref, w_ref, ws_ref, o_ref, os_ref, xn_ref = refs
    else:
        x_ref, g_ref, w_ref, o_ref, xn_ref = refs
    mm = _dot_nt if w_rows_are_outputs else _dot

    @pl.when(pl.program_id(1) == 0)
    def _():
        xn_ref[...] = _rms(x_ref[...], g_ref[...]).astype(BF16)
        if has_small:
            os_ref[...] = mm(xn_ref[...], ws_ref[...])

    o_ref[...] = mm(xn_ref[...], w_ref[...]).astype(o_ref.dtype)


def _rms_matmul(x, g, w, w_small, tm, tn, out_dtype=F32, w_rows_are_outputs=False):
    m, k = x.shape
    n = w.shape[0] if w_rows_are_outputs else w.shape[1]
    has_small = w_small is not None
    wt = w_rows_are_outputs
    in_specs = [pl.BlockSpec((tm, k), lambda i, j: (i, 0)),
                pl.BlockSpec((1, k), lambda i, j: (0, 0)),
                pl.BlockSpec((tn, k), lambda i, j: (j, 0)) if wt else pl.BlockSpec((k, tn), lambda i, j: (0, j))]
    out_specs = [pl.BlockSpec((tm, tn), lambda i, j: (i, j))]
    out_shape = [jax.ShapeDtypeStruct((m, n), out_dtype)]
    args = [x, g.reshape(1, k), w]
    if has_small:
        in_specs.append(pl.BlockSpec((SMALL, k) if wt else (k, SMALL), lambda i, j: (0, 0)))
        out_specs.append(pl.BlockSpec((tm, SMALL), lambda i, j: (i, 0)))
        out_shape.append(jax.ShapeDtypeStruct((m, SMALL), F32))
        args.append(w_small)
    res = pl.pallas_call(
        functools.partial(_rms_mm_kernel, has_small=has_small, w_rows_are_outputs=wt),
        grid=(m // tm, n // tn), in_specs=in_specs, out_specs=out_specs, out_shape=out_shape,
        scratch_shapes=[pltpu.VMEM((tm, k), BF16)], compiler_params=_cparams(2), name="rms_matmul")(*args)
    return res if has_small else res[0]


def _halo_rows(width):
    return 8 * ((width - 1 + 7) // 8)


def _conv_load_tile(x, hist_ref, ext_ref, tt, width):
    hp = _halo_rows(width)
    t = pl.program_id(1)

    @pl.when(t == 0)
    def _():
        ext_ref[0:hp, :] = jnp.zeros((hp, ext_ref.shape[1]), F32)
        ext_ref[hp - (width - 1):hp, :] = hist_ref[...]

    @pl.when(t > 0)
    def _():
        ext_ref[0:hp, :] = ext_ref[tt:tt + hp, :]

    ext_ref[hp:hp + tt, :] = x


def _conv_rows(ext_ref, w_ref, r0, nrows, width, c0=0, ncols=None):
    hp = _halo_rows(width)
    ncols = ext_ref.shape[1] if ncols is None else ncols
    acc = None
    for i in range(width):
        off = hp - (width - 1) + i + r0
        term = w_ref[i:i + 1, c0:c0 + ncols] * ext_ref[off:off + nrows, c0:c0 + ncols]
        acc = term if acc is None else acc + term
    return acc


def _conv_state(ext_ref, st_ref, tt, width):
    hp = _halo_rows(width)
    st_ref[...] = ext_ref[hp + tt - (width - 1):hp + tt, :]


def _sconv_kernel(h_ref, b_ref, c_ref, hist_ref, w_ref, o_ref, st_ref, ext_ref, *, tt):
    _conv_load_tile(c_ref[...].astype(F32) * h_ref[...].astype(F32), hist_ref, ext_ref, tt, SC_W)
    rb = min(tt, 128)
    for r0 in range(0, tt, rb):
        o_ref[r0:r0 + rb, :] = (b_ref[r0:r0 + rb, :].astype(F32)
                                * _conv_rows(ext_ref, w_ref, r0, rb, SC_W)).astype(o_ref.dtype)
    _conv_state(ext_ref, st_ref, tt, SC_W)


def _hist_spec(hist, layer, rows, cols):
    if hist.ndim == 4:
        return pl.BlockSpec((None, None, rows, cols), lambda i, j: (i, layer, 0, 0))
    return pl.BlockSpec((None, rows, cols), lambda i, j: (i, 0, 0))


def _sconv(z3, hist, w, tt, layer=0):
    b, t, _ = z3.shape
    col = lambda c: pl.BlockSpec((None, tt, BRANCH), lambda i, j, c=c: (i, j, c // BRANCH))
    return pl.pallas_call(
        functools.partial(_sconv_kernel, tt=tt), grid=(b, t // tt),
        in_specs=[col(C_AH), col(C_AB), col(C_AC),
                  _hist_spec(hist, layer, SC_W - 1, BRANCH),
                  pl.BlockSpec((SC_W, BRANCH), lambda i, j: (0, 0))],
        out_specs=[pl.BlockSpec((None, tt, BRANCH), lambda i, j: (i, j, 0)),
                   pl.BlockSpec((None, SC_W - 1, BRANCH), lambda i, j: (i, 0, 0))],
        out_shape=[jax.ShapeDtypeStruct((b, t, BRANCH), z3.dtype),
                   jax.ShapeDtypeStruct((b, SC_W - 1, BRANCH), F32)],
        scratch_shapes=[pltpu.VMEM((_halo_rows(SC_W) + tt, BRANCH), F32)],
        compiler_params=_cparams(2), name="sconv")(z3, z3, z3, hist, w)


def _conf_kernel(v_ref, g_ref, hist_ref, w_ref, cb_ref, lg_ref, lb_ref, o_ref, st_ref, ext_ref, *phase, tt):
    _conv_load_tile(v_ref[...].astype(F32) * jax.nn.sigmoid(g_ref[...].astype(F32)), hist_ref, ext_ref, tt, CONF_W)
    hp = _halo_rows(CONF_W)
    if phase:
        (ph_ref,) = phase
        n = hp + tt - 8
        for r in range(1, 8):
            ph_ref[r - 1, 0:n, :] = ext_ref[r:r + n, :]
    rb = min(tt, 64)
    for r0 in range(0, tt, rb):
        if phase:
            y = None
            for i in range(CONF_W):
                a, r = divmod(hp - (CONF_W - 1) + i, 8)
                rows = slice(8 * a + r0, 8 * a + r0 + rb)
                term = w_ref[i:i + 1, :] * (ext_ref[rows, :] if r == 0 else ph_ref[r - 1, rows, :])
                y = term if y is None else y + term
            y = y + cb_ref[...]
        else:
            y = _conv_rows(ext_ref, w_ref, r0, rb, CONF_W) + cb_ref[...]
        mu = jnp.mean(y, axis=-1, keepdims=True)
        yc = y - mu
        var = jnp.mean(yc * yc, axis=-1, keepdims=True)
        o_ref[r0:r0 + rb, :] = _silu((yc * lax.rsqrt(var + 1e-5)) * lg_ref[...] + lb_ref[...]).astype(o_ref.dtype)
    _conv_state(ext_ref, st_ref, tt, CONF_W)


def _conformer(z3, hist, w, cb, lg, lb, tt, layer=0):
    b, t, _ = z3.shape
    col = lambda c: pl.BlockSpec((None, tt, BRANCH), lambda i, j, c=c: (i, j, c // BRANCH))
    row = pl.BlockSpec((1, BRANCH), lambda i, j: (0, 0))
    scratch = [pltpu.VMEM((_halo_rows(CONF_W) + tt, BRANCH), F32)]
    if tt % 8 == 0:
        scratch.append(pltpu.VMEM((7, _halo_rows(CONF_W) + tt, BRANCH), F32))
    return pl.pallas_call(
        functools.partial(_conf_kernel, tt=tt), grid=(b, t // tt),
        in_specs=[col(C_CV), col(C_CG),
                  _hist_spec(hist, layer, CONF_W - 1, BRANCH),
                  pl.BlockSpec((CONF_W, BRANCH), lambda i, j: (0, 0)), row, row, row],
        out_specs=[pl.BlockSpec((None, tt, BRANCH), lambda i, j: (i, j, 0)),
                   pl.BlockSpec((None, CONF_W - 1, BRANCH), lambda i, j: (i, 0, 0))],
        out_shape=[jax.ShapeDtypeStruct((b, t, BRANCH), z3.dtype),
                   jax.ShapeDtypeStruct((b, CONF_W - 1, BRANCH), F32)],
        scratch_shapes=scratch,
        compiler_params=_cparams(2), name="conformer")(
            z3, z3, hist, w, cb.reshape(1, BRANCH), lg.reshape(1, BRANCH), lb.reshape(1, BRANCH))


def _gdn_pre_kernel(x_ref, hist_ref, w_ref, q_ref, k_ref, v_ref, st_ref, ext_ref, *, tt):
    _conv_load_tile(x_ref[...].astype(F32), hist_ref, ext_ref, tt, GDN_W)
    rb = min(tt, 128)
    for r0 in range(0, tt, rb):
        for part, dst in enumerate((q_ref, k_ref, v_ref)):
            y = _silu(_conv_rows(ext_ref, w_ref, r0, rb, GDN_W, c0=part * BRANCH, ncols=BRANCH))
            if part == 2:
                dst[r0:r0 + rb, :] = y
                continue
            scale = GDN_DK ** -0.5 if part == 0 else 1.0
            for h in range(GDN_HEADS):
                blk = y[:, h * GDN_DK:(h + 1) * GDN_DK]
                nrm = lax.rsqrt(jnp.sum(blk * blk, axis=-1, keepdims=True) + 1e-6)
                dst[r0:r0 + rb, h * GDN_DK:(h + 1) * GDN_DK] = blk * nrm * scale
    _conv_state(ext_ref, st_ref, tt, GDN_W)


def _gdn_pre(z3, hist, w, tt, layer=0):
    b, t, _ = z3.shape
    wq = 3 * BRANCH
    out = pl.BlockSpec((None, tt, BRANCH), lambda i, j: (i, j, 0))
    return pl.pallas_call(
        functools.partial(_gdn_pre_kernel, tt=tt), grid=(b, t // tt),
        in_specs=[pl.BlockSpec((None, tt, wq), lambda i, j: (i, j, C_GQKV // wq)),
                  _hist_spec(hist, layer, GDN_W - 1, wq),
                  pl.BlockSpec((GDN_W, wq), lambda i, j: (0, 0))],
        out_specs=[out, out, out, pl.BlockSpec((None, GDN_W - 1, wq), lambda i, j: (i, 0, 0))],
        out_shape=[jax.ShapeDtypeStruct((b, t, BRANCH), F32)] * 3 + [jax.ShapeDtypeStruct((b, GDN_W - 1, wq), F32)],
        scratch_shapes=[pltpu.VMEM((_halo_rows(GDN_W) + tt, wq), F32)],
        compiler_params=_cparams(2), name="gdn_pre")(z3, hist, w)


def _fox_bias_placement():
    pq = np.zeros((SMALL, BRANCH), np.float32)
    pk = np.zeros((SMALL, BRANCH), np.float32)
    for h in range(FOX_HEADS):
        base = (h // 2) * LANES + (FOX_DH if h % 2 == 0 else 0)
        for part in range(3):
            pq[8 * (part + 1) + h, base + part] = 1.0
            pq[0, base + 3 + part] = 1.0
            pk[0, base + part] = 1.0
            pk[8 * (part + 1) + h, base + 3 + part] = -1.0
    return jnp.asarray(pq, BF16), jnp.asarray(pk, BF16)


def _gates_kernel(zs_ref, bias_ref, alog_ref, *rest, tt, with_t):
    if with_t:
        pq_ref, pk_ref, p_ref, pt_ref, qa_ref, ka_ref, carry_ref = rest
    else:
        p_ref, carry_ref = rest
    zs = zs_ref[...]
    lane = lax.broadcasted_iota(jnp.int32, zs.shape, 1)
    tb = zs + bias_ref[...]
    g_log = -jnp.exp(alog_ref[...]) * _softplus(tb)
    beta = jax.nn.sigmoid(zs)
    flog = -_softplus(-tb)
    is_f = (lane >= 8) & (lane < 16)
    f_only = jnp.where(is_f, flog, 0.0)
    if tt > 1:
        r = lax.broadcasted_iota(jnp.int32, (tt, tt), 0)
        c = lax.broadcasted_iota(jnp.int32, (tt, tt), 1)
        csum = _dot_sel(jnp.where(r >= c, 1.0, 0.0), f_only)
    else:
        csum = f_only

    @pl.when(pl.program_id(1) == 0)
    def _():
        carry_ref[...] = jnp.zeros_like(carry_ref)

    csum = csum + carry_ref[...]
    carry_ref[...] = csum[tt - 1:tt, :]
    p = jnp.where(lane < 4, g_log, jnp.where(lane < 8, beta, f_only)) + pltpu.roll(csum, 8, axis=1)
    p_ref[...] = p
    if with_t:
        pt_ref[...] = p.T
        hi, mid, lo = (piece.astype(F32) for piece in _split3(csum * LOG2E))
        c3 = (hi + pltpu.roll(mid, 8, axis=1) + pltpu.roll(lo, 16, axis=1) + jnp.where(lane == 0, 1.0, 0.0)).astype(BF16)
        qa_ref[...] = _dot(c3, pq_ref[...]).astype(BF16)
        ka_ref[...] = _dot(c3, pk_ref[...]).astype(BF16)


def _gates(zs3, bias_row, alog_row, tt):
    b, t, _ = zs3.shape
    with_t = tt % LANES == 0
    row = pl.BlockSpec((1, SMALL), lambda i, j: (0, 0))
    in_specs = [pl.BlockSpec((None, tt, SMALL), lambda i, j: (i, j, 0)), row, row]
    args = [zs3, bias_row, alog_row]
    out_specs = [pl.BlockSpec((None, tt, SMALL), lambda i, j: (i, j, 0))]
    out_shape = [jax.ShapeDtypeStruct((b, t, SMALL), F32)]
    if with_t:
        place = pl.BlockSpec((SMALL, BRANCH), lambda i, j: (0, 0))
        aug = pl.BlockSpec((None, tt, BRANCH), lambda i, j: (i, j, 0))
        in_specs += [place, place]
        args += list(_fox_bias_placement())
        out_specs += [pl.BlockSpec((None, SMALL, tt), lambda i, j: (i, 0, j)), aug, aug]
        out_shape += [jax.ShapeDtypeStruct((b, SMALL, t), F32)] + [jax.ShapeDtypeStruct((b, t, BRANCH), BF16)] * 2
    res = pl.pallas_call(
        functools.partial(_gates_kernel, tt=tt, with_t=with_t), grid=(b, t // tt),
        in_specs=in_specs, out_specs=out_specs, out_shape=out_shape,
        scratch_shapes=[pltpu.VMEM((1, SMALL), F32)],
        compiler_params=_cparams(2), name="gates")(*args)
    return res if with_t else (res[0], None, None, None)


def _unit_lower_inverse_many(ns, c):
    r = lax.broadcasted_iota(jnp.int32, (c, c), 0)
    q = lax.broadcasted_iota(jnp.int32, (c, c), 1)
    mm = lambda a, b: _dot(a.astype(BF16), b.astype(BF16))
    blk8 = (r >> 3) == (q >> 3)
    eye = jnp.where(r == q, 1.0, 0.0)
    n0s = [jnp.where(blk8, n, 0.0) for n in ns]
    xs = [eye - n0 for n0 in n0s]
    ms = [mm(n0, n0) for n0 in n0s]
    xs = [x + mm(x, m) for x, m in zip(xs, ms)]
    ms = [mm(m, m) for m in ms]
    xs = [x + mm(x, m) for x, m in zip(xs, ms)]
    s = 3
    while (1 << s) < c:
        lower_left = ((r >> (s + 1)) == (q >> (s + 1))) & ((r >> s) != (q >> s))
        ts = [mm(x, jnp.where(lower_left, n, 0.0)) for x, n in zip(xs, ns)]
        xs = [x - mm(t, x) for x, t in zip(xs, ts)]
        s += 1
    return xs


def _gdn_chunk_kernel(q_ref, k_ref, v_ref, p_ref, pt_ref, gz_ref, nw_ref, s0_ref, o_ref, so_ref,
                      s_ref, u_ref, w_ref, qd_ref, kdt_ref, qk_ref, *, tt):
    c = GDN_CHUNK
    nc = tt // c

    @pl.when(pl.program_id(1) == 0)
    def _():
        s_ref[...] = s0_ref[...]

    r = lax.broadcasted_iota(jnp.int32, (c, c), 0)
    q_i = lax.broadcasted_iota(jnp.int32, (c, c), 1)
    tril = r >= q_i
    strict = r > q_i
    tri_f = jnp.where(tril, 1.0, 0.0).astype(F32)
    triu_f = jnp.where(r <= q_i, 1.0, 0.0).astype(F32)
    probs = [(ci, h) for ci in range(nc) for h in range(GDN_HEADS)]
    rows_of = lambda ci: slice(ci * c, (ci + 1) * c)
    cols_of = lambda h: slice(h * GDN_DK, (h + 1) * GDN_DK)

    d_cols = [_dot_sel(tri_f, p_ref[rows_of(ci), :]) for ci in range(nc)]
    d_rows = [_dot_sel_rhs(pt_ref[:, rows_of(ci)], triu_f) for ci in range(nc)]
    ns, rhss, dls = [], [], []
    for ci, h in probs:
        rows, hs = rows_of(ci), cols_of(h)
        qc, kc, vc = q_ref[rows, hs], k_ref[rows, hs], v_ref[rows, hs]
        beta = p_ref[rows, 4 + h:5 + h]
        dcol = d_cols[ci][:, h:h + 1]
        drow = d_rows[ci][h:h + 1, :]
        decay = jnp.where(tril, jnp.exp(jnp.where(tril, dcol - drow, 0.0)), 0.0)
        kb = kc * beta
        kbf = kc.astype(BF16)
        ns.append(jnp.where(strict, _dot_nt(kb.astype(BF16), kbf) * decay, 0.0))
        qk_ref[ci * GDN_HEADS + h] = jnp.where(tril, _dot_nt(qc.astype(BF16), kbf) * decay, 0.0).astype(BF16)
        ed = jnp.exp(dcol)
        dlast = dcol[c - 1:c, :]
        qd_ref[rows, hs] = (qc * ed).astype(BF16)
        kdt_ref[ci * GDN_HEADS + h] = (kc * jnp.exp(dlast - dcol)).T.astype(BF16)
        rhss.append(jnp.concatenate([vc * beta, kb * ed], axis=1).astype(BF16))
        dls.append(jnp.exp(dlast))
    t_invs = _unit_lower_inverse_many(ns, c)
    for (ci, h), t_inv, rhs in zip(probs, t_invs, rhss):
        sol = _dot(t_inv.astype(BF16), rhs)
        u_ref[rows_of(ci), cols_of(h)] = sol[:, :GDN_DK]
        w_ref[rows_of(ci), cols_of(h)] = sol[:, GDN_DK:].astype(BF16)

    for ci in range(nc):
        rows = rows_of(ci)
        s_olds = [s_ref[h] for h in range(GDN_HEADS)]
        s_bfs = [s.astype(BF16) for s in s_olds]
        vns = [(u_ref[rows, cols_of(h)] - _dot(w_ref[rows, cols_of(h)], s_bfs[h])).astype(BF16)
               for h in range(GDN_HEADS)]
        for h in range(GDN_HEADS):
            hs = cols_of(h)
            idx = ci * GDN_HEADS + h
            o = _dot(qd_ref[rows, hs], s_bfs[h]) + _dot(qk_ref[idx], vns[h])
            s_ref[h] = s_olds[h] * dls[idx] + _dot(kdt_ref[idx], vns[h])
            o_ref[rows, hs] = (_rms(o, nw_ref[...]) * _silu(gz_ref[rows, hs].astype(F32))).astype(o_ref.dtype)
    so_ref[...] = s_ref[...]


def _gdn_chunked(q, k, v, p, pt, z3, nw, s0, tt):
    b, t, _ = q.shape
    blk = pl.BlockSpec((None, tt, BRANCH), lambda i, j: (i, j, 0))
    st = pl.BlockSpec((None, GDN_HEADS, GDN_DK, GDN_DK), lambda i, j: (i, 0, 0, 0))
    return pl.pallas_call(
        functools.partial(_gdn_chunk_kernel, tt=tt), grid=(b, t // tt),
        in_specs=[blk, blk, blk,
                  pl.BlockSpec((None, tt, SMALL), lambda i, j: (i, j, 0)),
                  pl.BlockSpec((None, SMALL, tt), lambda i, j: (i, 0, j)),
                  pl.BlockSpec((None, tt, BRANCH), lambda i, j: (i, j, C_GZ // BRANCH)),
                  pl.BlockSpec((1, GDN_DK), lambda i, j: (0, 0)), st],
        out_specs=[blk, st],
        out_shape=[jax.ShapeDtypeStruct((b, t, BRANCH), z3.dtype),
                   jax.ShapeDtypeStruct((b, GDN_HEADS, GDN_DK, GDN_DK), F32)],
        scratch_shapes=[pltpu.VMEM((GDN_HEADS, GDN_DK, GDN_DK), F32),
                        pltpu.VMEM((tt, BRANCH), F32), pltpu.VMEM((tt, BRANCH), BF16), pltpu.VMEM((tt, BRANCH), BF16),
                        pltpu.VMEM((tt // GDN_CHUNK * GDN_HEADS, GDN_DK, GDN_CHUNK), BF16),
                        pltpu.VMEM((tt // GDN_CHUNK * GDN_HEADS, GDN_CHUNK, GDN_CHUNK), BF16)],
        compiler_params=_cparams(2), name="gdn_chunked")(q, k, v, p, pt, z3, nw.reshape(1, GDN_DK), s0)


def _gdn_step_kernel(q_ref, k_ref, v_ref, p_ref, gz_ref, nw_ref, s0_ref, o_ref, so_ref):
    r8 = lax.broadcasted_iota(jnp.int32, (8, GDN_DK), 0)
    rr = lax.broadcasted_iota(jnp.int32, (GDN_DK, GDN_DK), 0)
    cc = lax.broadcasted_iota(jnp.int32, (GDN_DK, GDN_DK), 1)
    p = p_ref[...]
    for h in range(GDN_HEADS):
        hs = slice(h * GDN_DK, (h + 1) * GDN_DK)
        q, k, v = q_ref[:, hs], k_ref[:, hs], v_ref[:, hs]
        g = p[:, h:h + 1]
        beta = p[:, 4 + h:5 + h]
        a = jnp.exp(g)
        w = k * beta * a
        s_old = s0_ref[h]
        s_bf = s_old.astype(BF16)
        lhs = jnp.where(r8 == 0, jnp.broadcast_to(w, (8, GDN_DK)),
                        jnp.where(r8 == 1, jnp.broadcast_to(q * a, (8, GDN_DK)), 0.0))
        ws = _dot(lhs.astype(BF16), s_bf)
        v_new = v * beta - ws[0:1, :]
        qk = jnp.sum(q.astype(BF16).astype(F32) * k.astype(BF16).astype(F32), axis=-1, keepdims=True)
        o = ws[1:2, :] + qk.astype(BF16).astype(F32) * v_new.astype(BF16).astype(F32)
        kdiag = jnp.where(rr == cc, jnp.broadcast_to(k, (GDN_DK, GDN_DK)), 0.0)
        outer = _dot(kdiag.astype(BF16), jnp.broadcast_to(v_new, (GDN_DK, GDN_DK)).astype(BF16))
        so_ref[h] = s_old * a + outer
        on = _rms(o, nw_ref[...])
        o_ref[:, hs] = on * _silu(gz_ref[:, hs])


def _gdn_step(q, k, v, p, z3, nw, state, layer):
    b = q.shape[0]
    blk = pl.BlockSpec((None, 1, BRANCH), lambda i: (i, 0, 0))
    st = pl.BlockSpec((None, GDN_HEADS, GDN_DK, GDN_DK), lambda i: (i, 0, 0, 0))
    st_in = pl.BlockSpec((None, None, GDN_HEADS, GDN_DK, GDN_DK), lambda i: (i, layer, 0, 0, 0))
    s0 = state
    return pl.pallas_call(
        _gdn_step_kernel, grid=(b,),
        in_specs=[blk, blk, blk, pl.BlockSpec((None, 1, SMALL), lambda i: (i, 0, 0)),
                  pl.BlockSpec((None, 1, BRANCH), lambda i: (i, 0, C_GZ // BRANCH)),
                  pl.BlockSpec((1, GDN_DK), lambda i: (0, 0)), st_in],
        out_specs=[blk, st],
        out_shape=[jax.ShapeDtypeStruct((b, 1, BRANCH), F32),
                   jax.ShapeDtypeStruct((b, GDN_HEADS, GDN_DK, GDN_DK), F32)],
        compiler_params=_cparams(1), name="gdn_step")(q, k, v, p, z3, nw.reshape(1, GDN_DK), s0)


def _fox_kernel(q_ref, k_ref, v_ref, qa_ref, ka_ref, o_ref, m_ref, acc_ref, *, tq, tk):
    qi, kj = pl.program_id(1), pl.program_id(2)

    @pl.when(kj == 0)
    def _():
        m_ref[...] = jnp.full(m_ref.shape, NEG_INF, F32)
        acc_ref[...] = jnp.zeros(acc_ref.shape, F32)

    def step(on_diagonal):
        lane = lax.broadcasted_iota(jnp.int32, (1, LANES), 1)
        if on_diagonal:
            causal = (lax.broadcasted_iota(jnp.int32, (tq, tk), 1) <= lax.broadcasted_iota(jnp.int32, (tq, tk), 0))
        for h in range(FOX_HEADS):
            ps = slice((h // 2) * LANES, (h // 2 + 1) * LANES)
            own = (lane >= (h % 2) * FOX_DH) & (lane < (h % 2 + 1) * FOX_DH)
            q2 = (q_ref[:, ps].astype(F32) * (FOX_DH ** -0.5 * LOG2E)).astype(BF16)
            s = _dot_nt(jnp.where(own, q2, qa_ref[:, ps]), jnp.where(own, k_ref[:, ps].astype(BF16), ka_ref[:, ps]))
            if on_diagonal:
                s = jnp.where(causal, s, NEG_INF)
            m_old = m_ref[h]
            m_new = jnp.maximum(m_old, jnp.max(s, axis=-1, keepdims=True))
            pr = jnp.exp2(s - jnp.concatenate([m_new] * (tk // LANES), axis=1))
            v1 = jnp.where(own, v_ref[:, ps].astype(BF16), jnp.ones((), BF16))
            acc_ref[h] = acc_ref[h] * jnp.exp2(m_old - m_new) + _dot(pr.astype(BF16), v1)
            m_ref[h] = m_new

    @pl.when(kj < qi)
    def _():
        step(False)

    @pl.when(kj == qi)
    def _():
        step(True)
        first = lax.broadcasted_iota(jnp.int32, (1, LANES), 1) < FOX_DH
        for pair in range(FOX_HEADS // 2):
            a0, a1 = acc_ref[2 * pair], acc_ref[2 * pair + 1]
            o0 = a0 / pltpu.roll(a0, FOX_DH, axis=1)
            o1 = a1 / pltpu.roll(a1, FOX_DH, axis=1)
            o_ref[:, pair * LANES:(pair + 1) * LANES] = jnp.where(first, o0, o1).astype(o_ref.dtype)


def _fox_prompt(z3, qa, ka, tq):
    b, t, _ = z3.shape
    tk = tq
    nq = t // tq
    kv = lambda c: pl.BlockSpec((None, tk, BRANCH), lambda i, qi, kj, c=c: (i, jnp.minimum(kj, qi), c // BRANCH))
    return pl.pallas_call(
        functools.partial(_fox_kernel, tq=tq, tk=tk), grid=(b, nq, nq),
        in_specs=[pl.BlockSpec((None, tq, BRANCH), lambda i, qi, kj: (i, qi, C_FQ // BRANCH)),
                  kv(C_FK), kv(C_FV),
                  pl.BlockSpec((None, tq, BRANCH), lambda i, qi, kj: (i, qi, 0)),
                  pl.BlockSpec((None, tk, BRANCH), lambda i, qi, kj: (i, jnp.minimum(kj, qi), 0))],
        out_specs=pl.BlockSpec((None, tq, BRANCH), lambda i, qi, kj: (i, qi, 0)),
        out_shape=jax.ShapeDtypeStruct((b, t, BRANCH), z3.dtype),
        scratch_shapes=[pltpu.VMEM((FOX_HEADS, tq, LANES), F32), pltpu.VMEM((FOX_HEADS, tq, LANES), F32)],
        compiler_params=_cparams(3), name="fox_prompt")(z3, z3, z3, qa, ka)


def _fox_decode_kernel(pt_ref, q_ref, kc_ref, vc_ref, fc_ref, *refs):
    del pt_ref
    n = PAGES_PER_STEP
    k_refs, v_refs, lf_refs = refs[:n], refs[n:2 * n], refs[2 * n:3 * n]
    o_ref, qb_ref, m_ref, l_ref, acc_ref, run_ref = refs[3 * n:]
    j = pl.program_id(1)
    nj = pl.num_programs(1)
    lane = lax.broadcasted_iota(jnp.int32, (1, LANES), 1)
    row8 = lax.broadcasted_iota(jnp.int32, (FOX_HEADS, LANES), 0)

    def to_cols(row):
        return jnp.broadcast_to(row, (LANES, BRANCH)).T

    def head_sum(x):
        out = jnp.zeros((FOX_HEADS, LANES), F32)
        for h in range(FOX_HEADS):
            sh = jnp.sum(x[h * FOX_DH:(h + 1) * FOX_DH, :], axis=0, keepdims=True)
            out = jnp.where(row8 == h, jnp.broadcast_to(sh, (FOX_HEADS, LANES)), out)
        return out

    @pl.when(j == 0)
    def _():
        qb_ref[...] = to_cols(q_ref[...] * (FOX_DH ** -0.5))
        m_ref[...] = head_sum(qb_ref[...] * to_cols(kc_ref[...]))
        l_ref[...] = jnp.ones(l_ref.shape, F32)
        acc_ref[...] = jnp.where(lane == 0, to_cols(vc_ref[...]), 0.0)
        run_ref[...] = fc_ref[...]

    qb = qb_ref[...]
    lf = jnp.concatenate([lf_refs[i][...] for i in range(n)], axis=0)
    pr_ = lax.broadcasted_iota(jnp.int32, (LANES, LANES), 0)
    pc_ = lax.broadcasted_iota(jnp.int32, (LANES, LANES), 1)
    in_page = _dot_sel_rhs(lf, jnp.where(pr_ > pc_, 1.0, 0.0))
    tot = jnp.sum(lf, axis=-1, keepdims=True)
    run = run_ref[...]
    logits = [None] * n
    for i in reversed(range(n)):
        s_i = head_sum(k_refs[i][...].reshape(BRANCH, LANES) * qb)
        logits[i] = s_i + in_page[8 * i:8 * (i + 1), :] + run
        run = run + tot[8 * i:8 * (i + 1), :]
    run_ref[...] = run
    mx = logits[0]
    for i in range(1, n):
        mx = jnp.maximum(mx, logits[i])
    m_old = m_ref[...]
    m_new = jnp.maximum(m_old, jnp.max(mx, axis=-1, keepdims=True))
    alpha = jnp.exp(m_old - m_new)
    ps = [jnp.exp(logits[i] - m_new) for i in range(n)]
    psum = ps[0]
    for i in range(1, n):
        psum = psum + ps[i]
    l_ref[...] = alpha * l_ref[...] + jnp.sum(psum, axis=-1, keepdims=True)
    m_ref[...] = m_new
    for h in range(FOX_HEADS):
        hs = slice(h * FOX_DH, (h + 1) * FOX_DH)
        a = acc_ref[hs, :] * alpha[h:h + 1, :]
        for i in range(n):
            a = a + v_refs[i][h] * ps[i][h:h + 1, :]
        acc_ref[hs, :] = a

    @pl.when(j == nj - 1)
    def _():
        l_all = l_ref[...]
        l_cols = jnp.concatenate([jnp.broadcast_to(l_all[h:h + 1, :], (FOX_DH, LANES)) for h in range(FOX_HEADS)],
                                 axis=0)
        o_cols = jnp.broadcast_to(jnp.sum(acc_ref[...], axis=-1, keepdims=True), (BRANCH, LANES)) / l_cols
        o_ref[...] = o_cols.T[0:1, :]


def _fox_decode(page_table, q, kc, vc, fc, cache_kt, cache_vt, cache_lft, layer):
    b, n_pages = page_table.shape
    nj = n_pages // PAGES_PER_STEP
    tok = pl.BlockSpec((None, 1, BRANCH), lambda i, j, pt: (i, 0, 0))
    pidx = lambda i, j, pt, k: pt[i, (nj - 1 - j) * PAGES_PER_STEP + k]
    page = lambda k: pl.BlockSpec((None, None, FOX_HEADS, FOX_DH, PAGE),
                                  lambda i, j, pt, k=k: (pidx(i, j, pt, k), layer, 0, 0, 0))
    lpage = lambda k: pl.BlockSpec((None, None, FOX_HEADS, PAGE),
                                   lambda i, j, pt, k=k: (pidx(i, j, pt, k), layer, 0, 0))
    rng = range(PAGES_PER_STEP)
    in_specs = ([tok, tok, tok, pl.BlockSpec((None, FOX_HEADS, LANES), lambda i, j, pt: (i, 0, 0))]
                + [page(k) for k in rng] * 2 + [lpage(k) for k in rng])
    return pl.pallas_call(
        _fox_decode_kernel,
        grid_spec=pltpu.PrefetchScalarGridSpec(
            num_scalar_prefetch=1, grid=(b, nj), in_specs=in_specs, out_specs=tok,
            scratch_shapes=[pltpu.VMEM((BRANCH, LANES), F32), pltpu.VMEM((FOX_HEADS, LANES), F32),
                            pltpu.VMEM((FOX_HEADS, LANES), F32), pltpu.VMEM((BRANCH, LANES), F32),
                            pltpu.VMEM((FOX_HEADS, LANES), F32)]),
        out_shape=jax.ShapeDtypeStruct((b, 1, BRANCH), F32),
        compiler_params=_cparams(2), name="fox_decode")(
            page_table, q, kc, vc, fc, *([cache_kt] * PAGES_PER_STEP), *([cache_vt] * PAGES_PER_STEP),
            *([cache_lft] * PAGES_PER_STEP))


def _merge_kernel(oa, ob, oc, od, g0, g1, g2, g3, x_ref, wb_ref, wo_ref, out_ref):
    merged = None
    for i, (o_ref, g_ref) in enumerate(((oa, g0), (ob, g1), (oc, g2), (od, g3))):
        term = jax.nn.sigmoid(g_ref[...].astype(F32)) * _dot(o_ref[...].astype(BF16), wb_ref[i])
        merged = term if merged is None else merged + term
    out_ref[...] = x_ref[...] + _dot(merged.astype(BF16), wo_ref[...])


def _merge(o_list, z2, x2, wb, wo, tm):
    m = x2.shape[0]
    o_spec = pl.BlockSpec((tm, BRANCH), lambda i: (i, 0))
    gate = lambda g: pl.BlockSpec((tm, D_MODEL), lambda i, g=g: (i, C_GATE // D_MODEL + g))
    xs = pl.BlockSpec((tm, D_MODEL), lambda i: (i, 0))
    return pl.pallas_call(
        _merge_kernel, grid=(m // tm,),
        in_specs=[o_spec] * 4 + [gate(g) for g in range(4)] + [
            xs, pl.BlockSpec((4, BRANCH, D_MODEL), lambda i: (0, 0, 0)),
            pl.BlockSpec((D_MODEL, D_MODEL), lambda i: (0, 0))],
        out_specs=xs, out_shape=jax.ShapeDtypeStruct((m, D_MODEL), F32),
        compiler_params=_cparams(1), name="merge")(*o_list, z2, z2, z2, z2, x2, wb, wo)


def _cross_kernel(x_ref, g_ref, wq_ref, mk_ref, mv_ref, wo_ref, out_ref):
    x = x_ref[...]
    q = _dot(_rms(x, g_ref[...]).astype(BF16), wq_ref[...])
    mk = mk_ref[...].astype(BF16)
    mv = mv_ref[...].astype(BF16)
    outs = []
    for h in range(X_HEADS):
        hs = slice(h * X_DH, (h + 1) * X_DH)
        s = _dot_nt(q[:, hs].astype(BF16), mk[:, hs]) * (X_DH ** -0.5)
        e = jnp.exp(s - jnp.max(s, axis=-1, keepdims=True))
        pr = e / jnp.sum(e, axis=-1, keepdims=True)
        outs.append(_dot(pr.astype(BF16), mv[:, hs]))
    out_ref[...] = x + _dot(jnp.concatenate(outs, axis=1).astype(BF16), wo_ref[...])


def _cross(x3, g, wq, mk_arr, mk_spec, mv_arr, mv_spec, wo, tt):
    b, t, _ = x3.shape
    xw = X_HEADS * X_DH
    xs = pl.BlockSpec((None, tt, D_MODEL), lambda i, j: (i, j, 0))
    return pl.pallas_call(
        _cross_kernel, grid=(b, t // tt),
        in_specs=[xs, pl.BlockSpec((1, D_MODEL), lambda i, j: (0, 0)),
                  pl.BlockSpec((D_MODEL, xw), lambda i, j: (0, 0)), mk_spec, mv_spec,
                  pl.BlockSpec((xw, D_MODEL), lambda i, j: (0, 0))],
        out_specs=xs, out_shape=jax.ShapeDtypeStruct((b, t, D_MODEL), F32),
        compiler_params=_cparams(2), name="cross_attn")(x3, g.reshape(1, D_MODEL), wq, mk_arr, mv_arr, wo)


def _mem_attn_step_kernel(q_ref, km_ref, vm_ref, o_ref):
    q = q_ref[...]
    row8 = lax.broadcasted_iota(jnp.int32, (8, X_DH), 0)
    q8 = jnp.zeros((8, X_DH), F32)
    for h in range(X_HEADS):
        q8 = jnp.where(row8 == h, jnp.broadcast_to(q[:, h * X_DH:(h + 1) * X_DH], (8, X_DH)), q8)
    n = MEM_LEN * X_HEADS
    s = _dot_nt(q8.astype(BF16), km_ref[...].astype(BF16)) * (X_DH ** -0.5)
    col = lax.broadcasted_iota(jnp.int32, (8, n), 1)
    row = lax.broadcasted_iota(jnp.int32, (8, n), 0)
    s = jnp.where((col & (X_HEADS - 1)) == row, s, NEG_INF)
    e = jnp.exp(s - jnp.max(s, axis=-1, keepdims=True))
    pr = e / jnp.sum(e, axis=-1, keepdims=True)
    o8 = _dot(pr.astype(BF16), vm_ref[...].astype(BF16))
    o_ref[...] = jnp.concatenate([o8[h:h + 1, :] for h in range(X_HEADS)], axis=1)


def _mem_attn_step(q3, km, vm, layer):
    b = q3.shape[0]
    xw = X_HEADS * X_DH
    tok = pl.BlockSpec((None, 1, xw), lambda i: (i, 0, 0))
    mem = pl.BlockSpec((None, None, MEM_LEN * X_HEADS, X_DH), lambda i: (i, layer, 0, 0))
    return pl.pallas_call(
        _mem_attn_step_kernel, grid=(b,), in_specs=[tok, mem, mem], out_specs=tok,
        out_shape=jax.ShapeDtypeStruct((b, 1, xw), F32),
        compiler_params=_cparams(1), name="mem_attn_step")(q3, km, vm)


def _proj_res_kernel(a_ref, w_ref, x_ref, o_ref):
    o_ref[...] = x_ref[...] + _dot(a_ref[...].astype(BF16), w_ref[...])


def _proj_res(a, w, x):
    m, k = a.shape
    n = w.shape[1]
    full = lambda r, c: pl.BlockSpec((r, c), lambda i: (0, 0))
    return pl.pallas_call(
        _proj_res_kernel, grid=(1,), in_specs=[full(m, k), full(k, n), full(m, n)], out_specs=full(m, n),
        out_shape=jax.ShapeDtypeStruct((m, n), F32), compiler_params=_cparams(1), name="proj_res")(a, w, x)


def _ffn_kernel(x_ref, g_ref, wg_ref, wu_ref, cw_ref, wd_ref, hist_ref, fg_ref, out_ref, st_ref,
                xn_ref, acc_ref, halo_ref, ext_ref, *, tt, tf, nf, final_norm):
    t, f = pl.program_id(1), pl.program_id(2)
    hp = _halo_rows(FFN_W)

    @pl.when(f == 0)
    def _():
        xn_ref[...] = _rms(x_ref[...], g_ref[...]).astype(BF16)

    @pl.when(t == 0)
    def _():
        ext_ref[0:hp, :] = jnp.zeros((hp, ext_ref.shape[1]), F32)
        ext_ref[hp - (FFN_W - 1):hp, :] = hist_ref[...]

    @pl.when(t > 0)
    def _():
        ext_ref[0:hp, :] = halo_ref[f]

    xn = xn_ref[...]
    blocks = [(c0, min(512, tf - c0)) for c0 in range(0, tf, 512)]
    n = len(blocks)
    us, acts, down = {}, {}, None
    for s in range(n + 2):
        if s < n:
            c0, w = blocks[s]
            ext_ref[hp:hp + tt, c0:c0 + w] = _dot(xn, wg_ref[:, c0:c0 + w])
            us[s] = _dot(xn, wu_ref[:, c0:c0 + w])
        if 1 <= s <= n:
            c0, w = blocks[s - 1]
            gc = _conv_rows(ext_ref, cw_ref, 0, tt, FFN_W, c0=c0, ncols=w)
            acts[s - 1] = (_silu(gc) * us.pop(s - 1)).astype(BF16)
        if s >= 2:
            c0, w = blocks[s - 2]
            part = _dot(acts.pop(s - 2), wd_ref[c0:c0 + w, :])
            down = part if down is None else down + part
    halo_ref[f] = ext_ref[tt:tt + hp, :]

    @pl.when(f == 0)
    def _():
        acc_ref[...] = down

    @pl.when((f > 0) & (f < nf - 1))
    def _():
        acc_ref[...] += down

    @pl.when(f == nf - 1)
    def _():
        y = x_ref[...] + (acc_ref[...] + down if nf > 1 else down)
        out_ref[...] = _rms(y, fg_ref[...]) if final_norm else y

    @pl.when((f == nf - 1) & (t == pl.num_programs(1) - 1))
    def _():
        for ff in range(nf):
            st_ref[:, ff * tf:(ff + 1) * tf] = halo_ref[ff][hp - (FFN_W - 1):hp, :]


def _ffn(x3, g, w_in, cw, w_out, hist, final_g, final_norm, tt, tf):
    b, t, _ = x3.shape
    nf = D_FF // tf
    xs = pl.BlockSpec((None, tt, D_MODEL), lambda i, j, f: (i, j, 0))
    st = pl.BlockSpec((None, FFN_W - 1, tf), lambda i, j, f: (i, 0, f))
    st_out = pl.BlockSpec((None, FFN_W - 1, D_FF), lambda i, j, f: (i, 0, 0))
    return pl.pallas_call(
        functools.partial(_ffn_kernel, tt=tt, tf=tf, nf=nf, final_norm=final_norm), grid=(b, t // tt, nf),
        in_specs=[xs, pl.BlockSpec((1, D_MODEL), lambda i, j, f: (0, 0)),
                  pl.BlockSpec((D_MODEL, tf), lambda i, j, f: (0, f)),
                  pl.BlockSpec((D_MODEL, tf), lambda i, j, f: (0, nf + f)),
                  pl.BlockSpec((FFN_W, tf), lambda i, j, f: (0, f)),
                  pl.BlockSpec((tf, D_MODEL), lambda i, j, f: (f, 0)), st,
                  pl.BlockSpec((1, D_MODEL), lambda i, j, f: (0, 0))],
        out_specs=[xs, st_out],
        out_shape=[jax.ShapeDtypeStruct((b, t, D_MODEL), F32), jax.ShapeDtypeStruct((b, FFN_W - 1, D_FF), F32)],
        scratch_shapes=[pltpu.VMEM((tt, D_MODEL), BF16), pltpu.VMEM((tt, D_MODEL), F32),
                        pltpu.VMEM((nf, _halo_rows(FFN_W), tf), F32),
                        pltpu.VMEM((_halo_rows(FFN_W) + tt, tf), F32)],
        compiler_params=_cparams(3), name="conv_ffn")(
            x3, g.reshape(1, D_MODEL), w_in, w_in, cw, w_out, hist, final_g.reshape(1, D_MODEL))


def _ffn_step_kernel(x_ref, g_ref, wg_ref, wu_ref, cw_ref, wd_ref, h0_ref, h1_ref, out_ref, s0_ref, s1_ref,
                     xn_ref, acc_ref):
    f = pl.program_id(0)

    @pl.when(f == 0)
    def _():
        xn_ref[...] = _rms(x_ref[...], g_ref[...]).astype(BF16)
        acc_ref[...] = jnp.zeros(acc_ref.shape, F32)

    gcol = _dot(xn_ref[...], wg_ref[...])
    u = _dot(xn_ref[...], wu_ref[...])
    gc = cw_ref[0:1, :] * h0_ref[...] + cw_ref[1:2, :] * h1_ref[...] + cw_ref[2:3, :] * gcol
    s0_ref[...] = h1_ref[...]
    s1_ref[...] = gcol
    acc_ref[...] += _dot((_silu(gc) * u).astype(BF16), wd_ref[...])

    @pl.when(f == pl.num_programs(0) - 1)
    def _():
        out_ref[...] = x_ref[...] + acc_ref[...]


def _ffn_step(x2, g, w_in, cw, w_out, h0, h1, tf):
    b = x2.shape[0]
    nf = D_FF // tf
    xs = pl.BlockSpec((b, D_MODEL), lambda f: (0, 0))
    hs = pl.BlockSpec((b, tf), lambda f: (0, f))
    return pl.pallas_call(
        _ffn_step_kernel, grid=(nf,),
        in_specs=[xs, pl.BlockSpec((1, D_MODEL), lambda f: (0, 0)),
                  pl.BlockSpec((D_MODEL, tf), lambda f: (0, f)),
                  pl.BlockSpec((D_MODEL, tf), lambda f: (0, nf + f)),
                  pl.BlockSpec((FFN_W, tf), lambda f: (0, f)),
                  pl.BlockSpec((tf, D_MODEL), lambda f: (f, 0)), hs, hs],
        out_specs=[xs, hs, hs],
        out_shape=[jax.ShapeDtypeStruct((b, D_MODEL), F32), jax.ShapeDtypeStruct((b, D_FF), F32),
                   jax.ShapeDtypeStruct((b, D_FF), F32)],
        scratch_shapes=[pltpu.VMEM((b, D_MODEL), BF16), pltpu.VMEM((b, D_MODEL), F32)],
        compiler_params=_cparams(1), name="conv_ffn_step")(x2, g.reshape(1, D_MODEL), w_in, w_in, cw, w_out, h0, h1)


def _kv_out_kernel(k_ref, v_ref, *refs):
    kt_ref, vt_ref = refs[-2:]
    kt_ref[...] = k_ref[...].astype(F32).T
    vt_ref[...] = v_ref[...].astype(F32).T


def _kv_out(z3, layer, depth, prev, tt):
    b, t, _ = z3.shape
    col = lambda c: pl.BlockSpec((None, tt, BRANCH), lambda i, j, c=c: (i, j, c // BRANCH))
    out = pl.BlockSpec((None, None, BRANCH, tt), lambda i, j: (i, layer, 0, j))
    shape = jax.ShapeDtypeStruct((b, depth, BRANCH, t), F32)
    in_specs, args, aliases = [col(C_FK), col(C_FV)], [z3, z3], {}
    if prev is not None:
        in_specs += [pl.BlockSpec(memory_space=pl.ANY)] * 2
        args += list(prev)
        aliases = {2: 0, 3: 1}
    return pl.pallas_call(
        _kv_out_kernel, grid=(b, t // tt), in_specs=in_specs, out_specs=[out, out], out_shape=[shape, shape],
        input_output_aliases=aliases, compiler_params=_cparams(2), name="kv_out")(*args)


def _final_norm_kernel(x_ref, g_ref, o_ref):
    o_ref[...] = _rms(x_ref[...], g_ref[...])


def _final_norm(x2, g, tm):
    m = x2.shape[0]
    xs = pl.BlockSpec((tm, D_MODEL), lambda i: (i, 0))
    return pl.pallas_call(
        _final_norm_kernel, grid=(m // tm,),
        in_specs=[xs, pl.BlockSpec((1, D_MODEL), lambda i: (0, 0))], out_specs=xs,
        out_shape=jax.ShapeDtypeStruct((m, D_MODEL), F32),
        compiler_params=_cparams(1), name="final_norm")(x2, g.reshape(1, D_MODEL))


def _prep_layer(l, w_in, w_branch, w_out, w_cq, w_co, w_ffn_in, w_ffn_out, gdn_a_log, gdn_dt_bias, fox_fbias):
    w = jnp.transpose(w_in[l])
    w_main = jnp.concatenate([w[:3584], w[3592:6152], w[6160:]], axis=0).astype(BF16)
    w_small = jnp.concatenate([w[3584:3592], w[6152:6160],
                               jnp.zeros((SMALL - 16, D_MODEL), F32)], axis=0).astype(BF16)
    zeros4 = jnp.zeros((4,), F32)
    bias_row = jnp.concatenate([gdn_dt_bias[l], zeros4, fox_fbias[l], jnp.zeros((SMALL - 16,), F32)]).reshape(1, SMALL)
    alog_row = jnp.concatenate([gdn_a_log[l], jnp.zeros((SMALL - 4,), F32)]).reshape(1, SMALL)
    return dict(w_main=w_main, w_small=w_small, bias_row=bias_row, alog_row=alog_row,
                wb=w_branch[l].astype(BF16), wo=w_out[l].astype(BF16), wcq=w_cq[l].astype(BF16),
                wco=w_co[l].astype(BF16), wfi=w_ffn_in[l].astype(BF16), wfo=w_ffn_out[l].astype(BF16))


def kernel(x_prompt, x_sample, cache_fox_k, cache_fox_v, cache_fox_logf, cache_mem_k, cache_mem_v, state_sconv, state_gdn_conv, state_gdn, state_conf_conv, state_ffn_conv, page_table, mem_prompt, norm_mix, w_in, w_branch, w_out, sconv_w, gdn_conv_w, gdn_a_log, gdn_dt_bias, gdn_norm, conf_dw_w, conf_dw_b, conf_ln_g, conf_ln_b, fox_fbias, norm_cross, norm_mem, w_cq, w_ckv, w_co, norm_ffn, w_ffn_in, ffn_conv_w, w_ffn_out, norm_final):
    depth = w_in.shape[0]
    bp, t, _ = x_prompt.shape
    bs = x_sample.shape[0]
    xw = X_HEADS * X_DH
    prep = [_prep_layer(l, w_in, w_branch, w_out, w_cq, w_co, w_ffn_in, w_ffn_out, gdn_a_log, gdn_dt_bias, fox_fbias)
            for l in range(depth)]

    tt = min(t, 512)
    mp = bp * t
    x = x_prompt.reshape(mp, D_MODEL)
    mem2 = mem_prompt.reshape(bp * MEM_LEN, D_MODEL)
    p_out = {k: [] for k in ("fl", "mk", "mv", "sc", "gc", "gs", "cc", "ff")}
    zero = lambda *s: jnp.zeros(s, F32)
    kt_vt = None
    for l in range(depth):
        w = prep[l]
        z, zs = _rms_matmul(x, norm_mix[l], w["w_main"], w["w_small"], tm=min(mp, 2048), tn=1024, out_dtype=BF16,
                            w_rows_are_outputs=True)
        z3 = z.reshape(bp, t, NZ)
        o_a, st_sc = _sconv(z3, zero(bp, SC_W - 1, BRANCH), sconv_w[l], tt)
        q, k, v, st_gc = _gdn_pre(z3, zero(bp, GDN_W - 1, 3 * BRANCH), gdn_conv_w[l], tt)
        p, pt, qa, ka = _gates(zs.reshape(bp, t, SMALL), w["bias_row"], w["alog_row"], tt)
        o_b, st_gs = _gdn_chunked(q, k, v, p, pt, z3, gdn_norm[l], zero(bp, GDN_HEADS, GDN_DK, GDN_DK), tt)
        o_c, st_cc = _conformer(z3, zero(bp, CONF_W - 1, BRANCH), conf_dw_w[l], conf_dw_b[l],
                                conf_ln_g[l], conf_ln_b[l], tt)
        o_d = _fox_prompt(z3, qa, ka, tt)
        o_list = [o.reshape(mp, BRANCH) for o in (o_a, o_b, o_c, o_d)]
        x = _merge(o_list, z, x, w["wb"], w["wo"], tm=min(mp, 512))
        kv = _rms_matmul(mem2, norm_mem[l], w_ckv[l].astype(BF16), None, tm=min(bp * MEM_LEN, 1024), tn=512)
        kv3 = kv.reshape(bp, MEM_LEN, 2 * xw)
        x = _cross(x.reshape(bp, t, D_MODEL), norm_cross[l], w["wcq"],
                   kv3, pl.BlockSpec((None, MEM_LEN, xw), lambda i, j: (i, 0, 0)),
                   kv3, pl.BlockSpec((None, MEM_LEN, xw), lambda i, j: (i, 0, 1)), w["wco"], tt)
        x, st_ff = _ffn(x, norm_ffn[l], w["wfi"], ffn_conv_w[l], w["wfo"], zero(bp, FFN_W - 1, D_FF),
                        norm_final, l == depth - 1, tt=min(t, 512), tf=D_FF // 2)
        x = x.reshape(mp, D_MODEL)
        kt_vt = _kv_out(z3, l, depth, kt_vt, tt)
        p_out["fl"].append(pt[:, 8:16, :])
        p_out["mk"].append(kv3[:, :, :xw])
        p_out["mv"].append(kv3[:, :, xw:])
        for key, val in (("sc", st_sc), ("gc", st_gc), ("gs", st_gs), ("cc", st_cc), ("ff", st_ff)):
            p_out[key].append(val)
    y_prompt = x.reshape(bp, t, D_MODEL)

    ckt = jnp.transpose(cache_fox_k, (0, 2, 3, 4, 1))
    cvt = jnp.transpose(cache_fox_v, (0, 2, 3, 4, 1))
    clt = jnp.transpose(cache_fox_logf, (0, 2, 3, 1))
    cmk = cache_mem_k.reshape(bs, depth, MEM_LEN * X_HEADS, X_DH)
    cmv = cache_mem_v.reshape(bs, depth, MEM_LEN * X_HEADS, X_DH)
    xs_ = x_sample.reshape(bs, D_MODEL)
    s_out = {k: [] for k in ("fk", "fv", "fl", "sc", "gc", "gs", "cc", "ff")}
    for l in range(depth):
        w = prep[l]
        z, zs = _rms_matmul(xs_, norm_mix[l], w["w_main"], w["w_small"], tm=bs, tn=2048, w_rows_are_outputs=True)
        z3 = z.reshape(bs, 1, NZ)
        o_a, st_sc = _sconv(z3, state_sconv, sconv_w[l], 1, layer=l)
        q, k, v, st_gc = _gdn_pre(z3, state_gdn_conv, gdn_conv_w[l], 1, layer=l)
        p = _gates(zs.reshape(bs, 1, SMALL), w["bias_row"], w["alog_row"], 1)[0]
        o_b, st_gs = _gdn_step(q, k, v, p, z3, gdn_norm[l], state_gdn, l)
        o_c, st_cc = _conformer(z3, state_conf_conv, conf_dw_w[l], conf_dw_b[l], conf_ln_g[l], conf_ln_b[l], 1,
                                layer=l)
        fq, fk, fv = (z3[:, :, c:c + BRANCH] for c in (C_FQ, C_FK, C_FV))
        flog = p[:, :, 8:16]
        fc = jnp.broadcast_to(jnp.swapaxes(flog, 1, 2), (bs, FOX_HEADS, LANES))
        o_d = _fox_decode(page_table, fq, fk, fv, fc, ckt, cvt, clt, l)
        o_list = [o.reshape(bs, BRANCH) for o in (o_a, o_b, o_c, o_d)]
        xs_ = _merge(o_list, z, xs_, w["wb"], w["wo"], tm=bs)
        q3 = _rms_matmul(xs_, norm_cross[l], w["wcq"], None, tm=bs, tn=xw).reshape(bs, 1, xw)
        o3 = _mem_attn_step(q3, cmk, cmv, l)
        xs_ = _proj_res(o3.reshape(bs, xw), w["wco"], xs_)
        xs_, h_a, h_b = _ffn_step(xs_, norm_ffn[l], w["wfi"], ffn_conv_w[l], w["wfo"],
                                  state_ffn_conv[:, l, 0], state_ffn_conv[:, l, 1], tf=256)
        s_out["fk"].append(fk)
        s_out["fv"].append(fv)
        s_out["fl"].append(flog)
        for key, val in (("sc", st_sc), ("gc", st_gc), ("gs", st_gs), ("cc", st_cc),
                         ("ff", jnp.stack([h_a, h_b], axis=1))):
            s_out[key].append(val)
    y_sample = _final_norm(xs_, norm_final, tm=bs).reshape(bs, 1, D_MODEL)

    heads = lambda a: a.reshape(a.shape[0], a.shape[1], depth, FOX_HEADS, FOX_DH)
    memh = lambda a: a.reshape(bp, depth, MEM_LEN, X_HEADS, X_DH)
    from_pos_minor = lambda a: jnp.transpose(a.reshape(bp, depth, FOX_HEADS, FOX_DH, t), (0, 4, 1, 2, 3))
    return (y_prompt, y_sample,
            from_pos_minor(kt_vt[0]), from_pos_minor(kt_vt[1]),
            jnp.transpose(jnp.stack(p_out["fl"], axis=1), (0, 3, 1, 2)),
            memh(jnp.stack(p_out["mk"], axis=1)), memh(jnp.stack(p_out["mv"], axis=1)),
            jnp.stack(p_out["sc"], axis=1), jnp.stack(p_out["gc"], axis=1), jnp.stack(p_out["gs"], axis=1),
            jnp.stack(p_out["cc"], axis=1), jnp.stack(p_out["ff"], axis=1),
            heads(jnp.stack(s_out["fk"], axis=2)), heads(jnp.stack(s_out["fv"], axis=2)),
            jnp.stack(s_out["fl"], axis=2),
            jnp.stack(s_out["sc"], axis=1), jnp.stack(s_out["gc"], axis=1), jnp.stack(s_out["gs"], axis=1),
            jnp.stack(s_out["cc"], axis=1), jnp.stack(s_out["ff"], axis=1))
```

```python
import functools

import jax
import jax.numpy as jnp
import numpy as np
from jax import lax
from jax.experimental import pallas as pl
from jax.experimental.pallas import tpu as pltpu

F32 = jnp.float32
BF16 = jnp.bfloat16

D_MODEL = 1024
BRANCH = 512
GDN_HEADS = 4
GDN_DK = 128
FOX_HEADS = 8
FOX_DH = 64
X_HEADS = 4
X_DH = 128
D_FF = 2816
MEM_LEN = 256
PAGE = 128
SC_W, GDN_W, CONF_W, FFN_W = 3, 4, 31, 3
NEG_INF = -1e30
LOG2E = 1.4426950408889634

C_AH, C_AB, C_AC = 0, 512, 1024
C_GQKV = 1536
C_GZ = 3072
C_CV, C_CG = 3584, 4096
C_FQ, C_FK, C_FV = 4608, 5120, 5632
C_GATE = 6144
NZ = 10240
SMALL = 128
LANES = 128
GDN_CHUNK = 128
PAGES_PER_STEP = 32


def _cparams(n_axes, vmem_mb=48):
    return pltpu.CompilerParams(dimension_semantics=("arbitrary",) * n_axes,
                                vmem_limit_bytes=vmem_mb * 1024 * 1024)


def _dot(a, b):
    return jnp.dot(a, b, preferred_element_type=F32)


def _dot_nt(a, b):
    return lax.dot_general(a, b, (((1,), (1,)), ((), ())), preferred_element_type=F32)


def _split3(x):
    hi = x.astype(BF16)
    r1 = x - hi.astype(F32)
    mid = r1.astype(BF16)
    return hi, mid, (r1 - mid.astype(F32)).astype(BF16)


def _dot_sel(sel01, x):
    s = sel01.astype(BF16)
    hi, mid, lo = _split3(x)
    return _dot(s, hi) + _dot(s, mid) + _dot(s, lo)


def _dot_sel_rhs(x, sel01):
    s = sel01.astype(BF16)
    hi, mid, lo = _split3(x)
    return _dot(hi, s) + _dot(mid, s) + _dot(lo, s)


def _rms(x, g, eps=1e-6):
    return (x * lax.rsqrt(jnp.mean(x * x, axis=-1, keepdims=True) + eps)) * g


def _softplus(t):
    return jnp.maximum(t, 0.0) + jnp.log1p(jnp.exp(-jnp.abs(t)))


def _silu(t):
    return t * jax.nn.sigmoid(t)


def _rms_mm_kernel(*refs, has_small, w_rows_are_outputs):
    if has_small:
        x_ref, g_ref, w_ref, ws_ref, o_ref, os_ref, xn_ref = refs
    else:
        x_ref, g_ref, w_ref, o_ref, xn_ref = refs
    mm = _dot_nt if w_rows_are_outputs else _dot

    @pl.when(pl.program_id(1) == 0)
    def _():
        xn_ref[...] = _rms(x_ref[...], g_ref[...]).astype(BF16)
        if has_small:
            os_ref[...] = mm(xn_ref[...], ws_ref[...])

    o_ref[...] = mm(xn_ref[...], w_ref[...]).astype(o_ref.dtype)


def _rms_matmul(x, g, w, w_small, tm, tn, out_dtype=F32, w_rows_are_outputs=False):
    m, k = x.shape
    n = w.shape[0] if w_rows_are_outputs else w.shape[1]
    has_small = w_small is not None
    wt = w_rows_are_outputs
    in_specs = [pl.BlockSpec((tm, k), lambda i, j: (i, 0)),
                pl.BlockSpec((1, k), lambda i, j: (0, 0)),
                pl.BlockSpec((tn, k), lambda i, j: (j, 0)) if wt else pl.BlockSpec((k, tn), lambda i, j: (0, j))]
    out_specs = [pl.BlockSpec((tm, tn), lambda i, j: (i, j))]
    out_shape = [jax.ShapeDtypeStruct((m, n), out_dtype)]
    args = [x, g.reshape(1, k), w]
    if has_small:
        in_specs.append(pl.BlockSpec((SMALL, k) if wt else (k, SMALL), lambda i, j: (0, 0)))
        out_specs.append(pl.BlockSpec((tm, SMALL), lambda i, j: (i, 0)))
        out_shape.append(jax.ShapeDtypeStruct((m, SMALL), F32))
        args.append(w_small)
    res = pl.pallas_call(
        functools.partial(_rms_mm_kernel, has_small=has_small, w_rows_are_outputs=wt),
        grid=(m // tm, n // tn), in_specs=in_specs, out_specs=out_specs, out_shape=out_shape,
        scratch_shapes=[pltpu.VMEM((tm, k), BF16)], compiler_params=_cparams(2), name="rms_matmul")(*args)
    return res if has_small else res[0]


def _halo_rows(width):
    return 8 * ((width - 1 + 7) // 8)


def _conv_load_tile(x, hist_ref, ext_ref, tt, width):
    hp = _halo_rows(width)
    t = pl.program_id(1)

    @pl.when(t == 0)
    def _():
        ext_ref[0:hp, :] = jnp.zeros((hp, ext_ref.shape[1]), F32)
        ext_ref[hp - (width - 1):hp, :] = hist_ref[...]

    @pl.when(t > 0)
    def _():
        ext_ref[0:hp, :] = ext_ref[tt:tt + hp, :]

    ext_ref[hp:hp + tt, :] = x


def _conv_rows(ext_ref, w_ref, r0, nrows, width, c0=0, ncols=None):
    hp = _halo_rows(width)
    ncols = ext_ref.shape[1] if ncols is None else ncols
    acc = None
    for i in range(width):
        off = hp - (width - 1) + i + r0
        term = w_ref[i:i + 1, c0:c0 + ncols] * ext_ref[off:off + nrows, c0:c0 + ncols]
        acc = term if acc is None else acc + term
    return acc


def _conv_state(ext_ref, st_ref, tt, width):
    hp = _halo_rows(width)
    st_ref[...] = ext_ref[hp + tt - (width - 1):hp + tt, :]


def _sconv_kernel(h_ref, b_ref, c_ref, hist_ref, w_ref, o_ref, st_ref, ext_ref, *, tt):
    _conv_load_tile(c_ref[...].astype(F32) * h_ref[...].astype(F32), hist_ref, ext_ref, tt, SC_W)
    rb = min(tt, 128)
    for r0 in range(0, tt, rb):
        o_ref[r0:r0 + rb, :] = (b_ref[r0:r0 + rb, :].astype(F32)
                                * _conv_rows(ext_ref, w_ref, r0, rb, SC_W)).astype(o_ref.dtype)
    _conv_state(ext_ref, st_ref, tt, SC_W)


def _hist_spec(hist, layer, rows, cols):
    if hist.ndim == 4:
        return pl.BlockSpec((None, None, rows, cols), lambda i, j: (i, layer, 0, 0))
    return pl.BlockSpec((None, rows, cols), lambda i, j: (i, 0, 0))


def _sconv(z3, hist, w, tt, layer=0):
    b, t, _ = z3.shape
    col = lambda c: pl.BlockSpec((None, tt, BRANCH), lambda i, j, c=c: (i, j, c // BRANCH))
    return pl.pallas_call(
        functools.partial(_sconv_kernel, tt=tt), grid=(b, t // tt),
        in_specs=[col(C_AH), col(C_AB), col(C_AC),
                  _hist_spec(hist, layer, SC_W - 1, BRANCH),
                  pl.BlockSpec((SC_W, BRANCH), lambda i, j: (0, 0))],
        out_specs=[pl.BlockSpec((None, tt, BRANCH), lambda i, j: (i, j, 0)),
                   pl.BlockSpec((None, SC_W - 1, BRANCH), lambda i, j: (i, 0, 0))],
        out_shape=[jax.ShapeDtypeStruct((b, t, BRANCH), z3.dtype),
                   jax.ShapeDtypeStruct((b, SC_W - 1, BRANCH), F32)],
        scratch_shapes=[pltpu.VMEM((_halo_rows(SC_W) + tt, BRANCH), F32)],
        compiler_params=_cparams(2), name="sconv")(z3, z3, z3, hist, w)


def _conf_kernel(v_ref, g_ref, hist_ref, w_ref, cb_ref, lg_ref, lb_ref, o_ref, st_ref, ext_ref, *phase, tt):
    _conv_load_tile(v_ref[...].astype(F32) * jax.nn.sigmoid(g_ref[...].astype(F32)), hist_ref, ext_ref, tt, CONF_W)
    hp = _halo_rows(CONF_W)
    if phase:
        (ph_ref,) = phase
        n = hp + tt - 8
        for r in range(1, 8):
            ph_ref[r - 1, 0:n, :] = ext_ref[r:r + n, :]
    rb = min(tt, 64)
    for r0 in range(0, tt, rb):
        if phase:
            y = None
            for i in range(CONF_W):
                a, r = divmod(hp - (CONF_W - 1) + i, 8)
                rows = slice(8 * a + r0, 8 * a + r0 + rb)
                term = w_ref[i:i + 1, :] * (ext_ref[rows, :] if r == 0 else ph_ref[r - 1, rows, :])
                y = term if y is None else y + term
            y = y + cb_ref[...]
        else:
            y = _conv_rows(ext_ref, w_ref, r0, rb, CONF_W) + cb_ref[...]
        mu = jnp.mean(y, axis=-1, keepdims=True)
        yc = y - mu
        var = jnp.mean(yc * yc, axis=-1, keepdims=True)
        o_ref[r0:r0 + rb, :] = _silu((yc * lax.rsqrt(var + 1e-5)) * lg_ref[...] + lb_ref[...]).astype(o_ref.dtype)
    _conv_state(ext_ref, st_ref, tt, CONF_W)


def _conformer(z3, hist, w, cb, lg, lb, tt, layer=0):
    b, t, _ = z3.shape
    col = lambda c: pl.BlockSpec((None, tt, BRANCH), lambda i, j, c=c: (i, j, c // BRANCH))
    row = pl.BlockSpec((1, BRANCH), lambda i, j: (0, 0))
    scratch = [pltpu.VMEM((_halo_rows(CONF_W) + tt, BRANCH), F32)]
    if tt % 8 == 0:
        scratch.append(pltpu.VMEM((7, _halo_rows(CONF_W) + tt, BRANCH), F32))
    return pl.pallas_call(
        functools.partial(_conf_kernel, tt=tt), grid=(b, t // tt),
        in_specs=[col(C_CV), col(C_CG),
                  _hist_spec(hist, layer, CONF_W - 1, BRANCH),
                  pl.BlockSpec((CONF_W, BRANCH), lambda i, j: (0, 0)), row, row, row],
        out_specs=[pl.BlockSpec((None, tt, BRANCH), lambda i, j: (i, j, 0)),
                   pl.BlockSpec((None, CONF_W - 1, BRANCH), lambda i, j: (i, 0, 0))],
        out_shape=[jax.ShapeDtypeStruct((b, t, BRANCH), z3.dtype),
                   jax.ShapeDtypeStruct((b, CONF_W - 1, BRANCH), F32)],
        scratch_shapes=scratch,
        compiler_params=_cparams(2), name="conformer")(
            z3, z3, hist, w, cb.reshape(1, BRANCH), lg.reshape(1, BRANCH), lb.reshape(1, BRANCH))


def _gdn_pre_kernel(x_ref, hist_ref, w_ref, q_ref, k_ref, v_ref, st_ref, ext_ref, *, tt):
    _conv_load_tile(x_ref[...].astype(F32), hist_ref, ext_ref, tt, GDN_W)
    rb = min(tt, 128)
    for r0 in range(0, tt, rb):
        for part, dst in enumerate((q_ref, k_ref, v_ref)):
            y = _silu(_conv_rows(ext_ref, w_ref, r0, rb, GDN_W, c0=part * BRANCH, ncols=BRANCH))
            if part == 2:
                dst[r0:r0 + rb, :] = y
                continue
            scale = GDN_DK ** -0.5 if part == 0 else 1.0
            for h in range(GDN_HEADS):
                blk = y[:, h * GDN_DK:(h + 1) * GDN_DK]
                nrm = lax.rsqrt(jnp.sum(blk * blk, axis=-1, keepdims=True) + 1e-6)
                dst[r0:r0 + rb, h * GDN_DK:(h + 1) * GDN_DK] = blk * nrm * scale
    _conv_state(ext_ref, st_ref, tt, GDN_W)


def _gdn_pre(z3, hist, w, tt, layer=0):
    b, t, _ = z3.shape
    wq = 3 * BRANCH
    out = pl.BlockSpec((None, tt, BRANCH), lambda i, j: (i, j, 0))
    return pl.pallas_call(
        functools.partial(_gdn_pre_kernel, tt=tt), grid=(b, t // tt),
        in_specs=[pl.BlockSpec((None, tt, wq), lambda i, j: (i, j, C_GQKV // wq)),
                  _hist_spec(hist, layer, GDN_W - 1, wq),
                  pl.BlockSpec((GDN_W, wq), lambda i, j: (0, 0))],
        out_specs=[out, out, out, pl.BlockSpec((None, GDN_W - 1, wq), lambda i, j: (i, 0, 0))],
        out_shape=[jax.ShapeDtypeStruct((b, t, BRANCH), F32)] * 3 + [jax.ShapeDtypeStruct((b, GDN_W - 1, wq), F32)],
        scratch_shapes=[pltpu.VMEM((_halo_rows(GDN_W) + tt, wq), F32)],
        compiler_params=_cparams(2), name="gdn_pre")(z3, hist, w)


def _fox_bias_placement():
    pq = np.zeros((SMALL, BRANCH), np.float32)
    pk = np.zeros((SMALL, BRANCH), np.float32)
    for h in range(FOX_HEADS):
        base = (h // 2) * LANES + (FOX_DH if h % 2 == 0 else 0)
        for part in range(3):
            pq[8 * (part + 1) + h, base + part] = 1.0
            pq[0, base + 3 + part] = 1.0
            pk[0, base + part] = 1.0
            pk[8 * (part + 1) + h, base + 3 + part] = -1.0
    return jnp.asarray(pq, BF16), jnp.asarray(pk, BF16)


def _gates_kernel(zs_ref, bias_ref, alog_ref, *rest, tt, with_t):
    if with_t:
        pq_ref, pk_ref, p_ref, pt_ref, qa_ref, ka_ref, carry_ref = rest
    else:
        p_ref, carry_ref = rest
    zs = zs_ref[...]
    lane = lax.broadcasted_iota(jnp.int32, zs.shape, 1)
    tb = zs + bias_ref[...]
    g_log = -jnp.exp(alog_ref[...]) * _softplus(tb)
    beta = jax.nn.sigmoid(zs)
    flog = -_softplus(-tb)
    is_f = (lane >= 8) & (lane < 16)
    f_only = jnp.where(is_f, flog, 0.0)
    if tt > 1:
        r = lax.broadcasted_iota(jnp.int32, (tt, tt), 0)
        c = lax.broadcasted_iota(jnp.int32, (tt, tt), 1)
        csum = _dot_sel(jnp.where(r >= c, 1.0, 0.0), f_only)
    else:
        csum = f_only

    @pl.when(pl.program_id(1) == 0)
    def _():
        carry_ref[...] = jnp.zeros_like(carry_ref)

    csum = csum + carry_ref[...]
    carry_ref[...] = csum[tt - 1:tt, :]
    p = jnp.where(lane < 4, g_log, jnp.where(lane < 8, beta, f_only)) + pltpu.roll(csum, 8, axis=1)
    p_ref[...] = p
    if with_t:
        pt_ref[...] = p.T
        hi, mid, lo = (piece.astype(F32) for piece in _split3(csum * LOG2E))
        c3 = (hi + pltpu.roll(mid, 8, axis=1) + pltpu.roll(lo, 16, axis=1) + jnp.where(lane == 0, 1.0, 0.0)).astype(BF16)
        qa_ref[...] = _dot(c3, pq_ref[...]).astype(BF16)
        ka_ref[...] = _dot(c3, pk_ref[...]).astype(BF16)


def _gates(zs3, bias_row, alog_row, tt):
    b, t, _ = zs3.shape
    with_t = tt % LANES == 0
    row = pl.BlockSpec((1, SMALL), lambda i, j: (0, 0))
    in_specs = [pl.BlockSpec((None, tt, SMALL), lambda i, j: (i, j, 0)), row, row]
    args = [zs3, bias_row, alog_row]
    out_specs = [pl.BlockSpec((None, tt, SMALL), lambda i, j: (i, j, 0))]
    out_shape = [jax.ShapeDtypeStruct((b, t, SMALL), F32)]
    if with_t:
        place = pl.BlockSpec((SMALL, BRANCH), lambda i, j: (0, 0))
        aug = pl.BlockSpec((None, tt, BRANCH), lambda i, j: (i, j, 0))
        in_specs += [place, place]
        args += list(_fox_bias_placement())
        out_specs += [pl.BlockSpec((None, SMALL, tt), lambda i, j: (i, 0, j)), aug, aug]
        out_shape += [jax.ShapeDtypeStruct((b, SMALL, t), F32)] + [jax.ShapeDtypeStruct((b, t, BRANCH), BF16)] * 2
    res = pl.pallas_call(
        functools.partial(_gates_kernel, tt=tt, with_t=with_t), grid=(b, t // tt),
        in_specs=in_specs, out_specs=out_specs, out_shape=out_shape,
        scratch_shapes=[pltpu.VMEM((1, SMALL), F32)],
        compiler_params=_cparams(2), name="gates")(*args)
    return res if with_t else (res[0], None, None, None)


def _unit_lower_inverse_many(ns, c):
    r = lax.broadcasted_iota(jnp.int32, (c, c), 0)
    q = lax.broadcasted_iota(jnp.int32, (c, c), 1)
    mm = lambda a, b: _dot(a.astype(BF16), b.astype(BF16))
    blk8 = (r >> 3) == (q >> 3)
    eye = jnp.where(r == q, 1.0, 0.0)
    n0s = [jnp.where(blk8, n, 0.0) for n in ns]
    xs = [eye - n0 for n0 in n0s]
    ms = [mm(n0, n0) for n0 in n0s]
    xs = [x + mm(x, m) for x, m in zip(xs, ms)]
    ms = [mm(m, m) for m in ms]
    xs = [x + mm(x, m) for x, m in zip(xs, ms)]
    s = 3
    while (1 << s) < c:
        lower_left = ((r >> (s + 1)) == (q >> (s + 1))) & ((r >> s) != (q >> s))
        ts = [mm(x, jnp.where(lower_left, n, 0.0)) for x, n in zip(xs, ns)]
        xs = [x - mm(t, x) for x, t in zip(xs, ts)]
        s += 1
    return xs


def _gdn_chunk_kernel(q_ref, k_ref, v_ref, p_ref, pt_ref, gz_ref, nw_ref, s0_ref, o_ref, so_ref,
                      s_ref, u_ref, w_ref, qd_ref, kdt_ref, qk_ref, *, tt):
    c = GDN_CHUNK
    nc = tt // c

    @pl.when(pl.program_id(1) == 0)
    def _():
        s_ref[...] = s0_ref[...]

    r = lax.broadcasted_iota(jnp.int32, (c, c), 0)
    q_i = lax.broadcasted_iota(jnp.int32, (c, c), 1)
    tril = r >= q_i
    strict = r > q_i
    tri_f = jnp.where(tril, 1.0, 0.0).astype(F32)
    triu_f = jnp.where(r <= q_i, 1.0, 0.0).astype(F32)
    probs = [(ci, h) for ci in range(nc) for h in range(GDN_HEADS)]
    rows_of = lambda ci: slice(ci * c, (ci + 1) * c)
    cols_of = lambda h: slice(h * GDN_DK, (h + 1) * GDN_DK)

    d_cols = [_dot_sel(tri_f, p_ref[rows_of(ci), :]) for ci in range(nc)]
    d_rows = [_dot_sel_rhs(pt_ref[:, rows_of(ci)], triu_f) for ci in range(nc)]
    ns, rhss, dls = [], [], []
    for ci, h in probs:
        rows, hs = rows_of(ci), cols_of(h)
        qc, kc, vc = q_ref[rows, hs], k_ref[rows, hs], v_ref[rows, hs]
        beta = p_ref[rows, 4 + h:5 + h]
        dcol = d_cols[ci][:, h:h + 1]
        drow = d_rows[ci][h:h + 1, :]
        decay = jnp.where(tril, jnp.exp(jnp.where(tril, dcol - drow, 0.0)), 0.0)
        kb = kc * beta
        kbf = kc.astype(BF16)
        ns.append(jnp.where(strict, _dot_nt(kb.astype(BF16), kbf) * decay, 0.0))
        qk_ref[ci * GDN_HEADS + h] = jnp.where(tril, _dot_nt(qc.astype(BF16), kbf) * decay, 0.0).astype(BF16)
        ed = jnp.exp(dcol)
        dlast = dcol[c - 1:c, :]
        qd_ref[rows, hs] = (qc * ed).astype(BF16)
        kdt_ref[ci * GDN_HEADS + h] = (kc * jnp.exp(dlast - dcol)).T.astype(BF16)
        rhss.append(jnp.concatenate([vc * beta, kb * ed], axis=1).astype(BF16))
        dls.append(jnp.exp(dlast))
    t_invs = _unit_lower_inverse_many(ns, c)
    for (ci, h), t_inv, rhs in zip(probs, t_invs, rhss):
        sol = _dot(t_inv.astype(BF16), rhs)
        u_ref[rows_of(ci), cols_of(h)] = sol[:, :GDN_DK]
        w_ref[rows_of(ci), cols_of(h)] = sol[:, GDN_DK:].astype(BF16)

    for ci in range(nc):
        rows = rows_of(ci)
        s_olds = [s_ref[h] for h in range(GDN_HEADS)]
        s_bfs = [s.astype(BF16) for s in s_olds]
        vns = [(u_ref[rows, cols_of(h)] - _dot(w_ref[rows, cols_of(h)], s_bfs[h])).astype(BF16)
               for h in range(GDN_HEADS)]
        for h in range(GDN_HEADS):
            hs = cols_of(h)
            idx = ci * GDN_HEADS + h
            o = _dot(qd_ref[rows, hs], s_bfs[h]) + _dot(qk_ref[idx], vns[h])
            s_ref[h] = s_olds[h] * dls[idx] + _dot(kdt_ref[idx], vns[h])
            o_ref[rows, hs] = (_rms(o, nw_ref[...]) * _silu(gz_ref[rows, hs].astype(F32))).astype(o_ref.dtype)
    so_ref[...] = s_ref[...]


def _gdn_chunked(q, k, v, p, pt, z3, nw, s0, tt):
    b, t, _ = q.shape
    blk = pl.BlockSpec((None, tt, BRANCH), lambda i, j: (i, j, 0))
    st = pl.BlockSpec((None, GDN_HEADS, GDN_DK, GDN_DK), lambda i, j: (i, 0, 0, 0))
    return pl.pallas_call(
        functools.partial(_gdn_chunk_kernel, tt=tt), grid=(b, t // tt),
        in_specs=[blk, blk, blk,
                  pl.BlockSpec((None, tt, SMALL), lambda i, j: (i, j, 0)),
                  pl.BlockSpec((None, SMALL, tt), lambda i, j: (i, 0, j)),
                  pl.BlockSpec((None, tt, BRANCH), lambda i, j: (i, j, C_GZ // BRANCH)),
                  pl.BlockSpec((1, GDN_DK), lambda i, j: (0, 0)), st],
        out_specs=[blk, st],
        out_shape=[jax.ShapeDtypeStruct((b, t, BRANCH), z3.dtype),
                   jax.ShapeDtypeStruct((b, GDN_HEADS, GDN_DK, GDN_DK), F32)],
        scratch_shapes=[pltpu.VMEM((GDN_HEADS, GDN_DK, GDN_DK), F32),
                        pltpu.VMEM((tt, BRANCH), F32), pltpu.VMEM((tt, BRANCH), BF16), pltpu.VMEM((tt, BRANCH), BF16),
                        pltpu.VMEM((tt // GDN_CHUNK * GDN_HEADS, GDN_DK, GDN_CHUNK), BF16),
                        pltpu.VMEM((tt // GDN_CHUNK * GDN_HEADS, GDN_CHUNK, GDN_CHUNK), BF16)],
        compiler_params=_cparams(2), name="gdn_chunked")(q, k, v, p, pt, z3, nw.reshape(1, GDN_DK), s0)


def _gdn_step_kernel(q_ref, k_ref, v_ref, p_ref, gz_ref, nw_ref, s0_ref, o_ref, so_ref):
    r8 = lax.broadcasted_iota(jnp.int32, (8, GDN_DK), 0)
    rr = lax.broadcasted_iota(jnp.int32, (GDN_DK, GDN_DK), 0)
    cc = lax.broadcasted_iota(jnp.int32, (GDN_DK, GDN_DK), 1)
    p = p_ref[...]
    for h in range(GDN_HEADS):
        hs = slice(h * GDN_DK, (h + 1) * GDN_DK)
        q, k, v = q_ref[:, hs], k_ref[:, hs], v_ref[:, hs]
        g = p[:, h:h + 1]
        beta = p[:, 4 + h:5 + h]
        a = jnp.exp(g)
        w = k * beta * a
        s_old = s0_ref[h]
        s_bf = s_old.astype(BF16)
        lhs = jnp.where(r8 == 0, jnp.broadcast_to(w, (8, GDN_DK)),
                        jnp.where(r8 == 1, jnp.broadcast_to(q * a, (8, GDN_DK)), 0.0))
        ws = _dot(lhs.astype(BF16), s_bf)
        v_new = v * beta - ws[0:1, :]
        qk = jnp.sum(q.astype(BF16).astype(F32) * k.astype(BF16).astype(F32), axis=-1, keepdims=True)
        o = ws[1:2, :] + qk.astype(BF16).astype(F32) * v_new.astype(BF16).astype(F32)
        kdiag = jnp.where(rr == cc, jnp.broadcast_to(k, (GDN_DK, GDN_DK)), 0.0)
        outer = _dot(kdiag.astype(BF16), jnp.broadcast_to(v_new, (GDN_DK, GDN_DK)).astype(BF16))
        so_ref[h] = s_old * a + outer
        on = _rms(o, nw_ref[...])
        o_ref[:, hs] = on * _silu(gz_ref[:, hs])


def _gdn_step(q, k, v, p, z3, nw, state, layer):
    b = q.shape[0]
    blk = pl.BlockSpec((None, 1, BRANCH), lambda i: (i, 0, 0))
    st = pl.BlockSpec((None, GDN_HEADS, GDN_DK, GDN_DK), lambda i: (i, 0, 0, 0))
    st_in = pl.BlockSpec((None, None, GDN_HEADS, GDN_DK, GDN_DK), lambda i: (i, layer, 0, 0, 0))
    s0 = state
    return pl.pallas_call(
        _gdn_step_kernel, grid=(b,),
        in_specs=[blk, blk, blk, pl.BlockSpec((None, 1, SMALL), lambda i: (i, 0, 0)),
                  pl.BlockSpec((None, 1, BRANCH), lambda i: (i, 0, C_GZ // BRANCH)),
                  pl.BlockSpec((1, GDN_DK), lambda i: (0, 0)), st_in],
        out_specs=[blk, st],
        out_shape=[jax.ShapeDtypeStruct((b, 1, BRANCH), F32),
                   jax.ShapeDtypeStruct((b, GDN_HEADS, GDN_DK, GDN_DK), F32)],
        compiler_params=_cparams(1), name="gdn_step")(q, k, v, p, z3, nw.reshape(1, GDN_DK), s0)


def _fox_kernel(q_ref, k_ref, v_ref, qa_ref, ka_ref, o_ref, m_ref, acc_ref, *, tq, tk):
    qi, kj = pl.program_id(1), pl.program_id(2)

    @pl.when(kj == 0)
    def _():
        m_ref[...] = jnp.full(m_ref.shape, NEG_INF, F32)
        acc_ref[...] = jnp.zeros(acc_ref.shape, F32)

    def step(on_diagonal):
        lane = lax.broadcasted_iota(jnp.int32, (1, LANES), 1)
        if on_diagonal:
            causal = (lax.broadcasted_iota(jnp.int32, (tq, tk), 1) <= lax.broadcasted_iota(jnp.int32, (tq, tk), 0))
        for h in range(FOX_HEADS):
            ps = slice((h // 2) * LANES, (h // 2 + 1) * LANES)
            own = (lane >= (h % 2) * FOX_DH) & (lane < (h % 2 + 1) * FOX_DH)
            q2 = (q_ref[:, ps].astype(F32) * (FOX_DH ** -0.5 * LOG2E)).astype(BF16)
            s = _dot_nt(jnp.where(own, q2, qa_ref[:, ps]), jnp.where(own, k_ref[:, ps].astype(BF16), ka_ref[:, ps]))
            if on_diagonal:
                s = jnp.where(causal, s, NEG_INF)
            m_old = m_ref[h]
            m_new = jnp.maximum(m_old, jnp.max(s, axis=-1, keepdims=True))
            pr = jnp.exp2(s - jnp.concatenate([m_new] * (tk // LANES), axis=1))
            v1 = jnp.where(own, v_ref[:, ps].astype(BF16), jnp.ones((), BF16))
            acc_ref[h] = acc_ref[h] * jnp.exp2(m_old - m_new) + _dot(pr.astype(BF16), v1)
            m_ref[h] = m_new

    @pl.when(kj < qi)
    def _():
        step(False)

    @pl.when(kj == qi)
    def _():
        step(True)
        first = lax.broadcasted_iota(jnp.int32, (1, LANES), 1) < FOX_DH
        for pair in range(FOX_HEADS // 2):
            a0, a1 = acc_ref[2 * pair], acc_ref[2 * pair + 1]
            o0 = a0 / pltpu.roll(a0, FOX_DH, axis=1)
            o1 = a1 / pltpu.roll(a1, FOX_DH, axis=1)
            o_ref[:, pair * LANES:(pair + 1) * LANES] = jnp.where(first, o0, o1).astype(o_ref.dtype)


def _fox_prompt(z3, qa, ka, tq):
    b, t, _ = z3.shape
    tk = tq
    nq = t // tq
    kv = lambda c: pl.BlockSpec((None, tk, BRANCH), lambda i, qi, kj, c=c: (i, jnp.minimum(kj, qi), c // BRANCH))
    return pl.pallas_call(
        functools.partial(_fox_kernel, tq=tq, tk=tk), grid=(b, nq, nq),
        in_specs=[pl.BlockSpec((None, tq, BRANCH), lambda i, qi, kj: (i, qi, C_FQ // BRANCH)),
                  kv(C_FK), kv(C_FV),
                  pl.BlockSpec((None, tq, BRANCH), lambda i, qi, kj: (i, qi, 0)),
                  pl.BlockSpec((None, tk, BRANCH), lambda i, qi, kj: (i, jnp.minimum(kj, qi), 0))],
        out_specs=pl.BlockSpec((None, tq, BRANCH), lambda i, qi, kj: (i, qi, 0)),
        out_shape=jax.ShapeDtypeStruct((b, t, BRANCH), z3.dtype),
        scratch_shapes=[pltpu.VMEM((FOX_HEADS, tq, LANES), F32), pltpu.VMEM((FOX_HEADS, tq, LANES), F32)],
        compiler_params=_cparams(3), name="fox_prompt")(z3, z3, z3, qa, ka)


def _fox_decode_kernel(pt_ref, q_ref, kc_ref, vc_ref, fc_ref, *refs):
    del pt_ref
    n = PAGES_PER_STEP
    k_refs, v_refs, lf_refs = refs[:n], refs[n:2 * n], refs[2 * n:3 * n]
    o_ref, qb_ref, m_ref, l_ref, acc_ref, run_ref = refs[3 * n:]
    j = pl.program_id(1)
    nj = pl.num_programs(1)
    lane = lax.broadcasted_iota(jnp.int32, (1, LANES), 1)
    row8 = lax.broadcasted_iota(jnp.int32, (FOX_HEADS, LANES), 0)

    def to_cols(row):
        return jnp.broadcast_to(row, (LANES, BRANCH)).T

    def head_sum(x):
        out = jnp.zeros((FOX_HEADS, LANES), F32)
        for h in range(FOX_HEADS):
            sh = jnp.sum(x[h * FOX_DH:(h + 1) * FOX_DH, :], axis=0, keepdims=True)
            out = jnp.where(row8 == h, jnp.broadcast_to(sh, (FOX_HEADS, LANES)), out)
        return out

    @pl.when(j == 0)
    def _():
        qb_ref[...] = to_cols(q_ref[...] * (FOX_DH ** -0.5))
        m_ref[...] = head_sum(qb_ref[...] * to_cols(kc_ref[...]))
        l_ref[...] = jnp.ones(l_ref.shape, F32)
        acc_ref[...] = jnp.where(lane == 0, to_cols(vc_ref[...]), 0.0)
        run_ref[...] = fc_ref[...]

    qb = qb_ref[...]
    lf = jnp.concatenate([lf_refs[i][...] for i in range(n)], axis=0)
    pr_ = lax.broadcasted_iota(jnp.int32, (LANES, LANES), 0)
    pc_ = lax.broadcasted_iota(jnp.int32, (LANES, LANES), 1)
    in_page = _dot_sel_rhs(lf, jnp.where(pr_ > pc_, 1.0, 0.0))
    tot = jnp.sum(lf, axis=-1, keepdims=True)
    run = run_ref[...]
    logits = [None] * n
    for i in reversed(range(n)):
        s_i = head_sum(k_refs[i][...].reshape(BRANCH, LANES) * qb)
        logits[i] = s_i + in_page[8 * i:8 * (i + 1), :] + run
        run = run + tot[8 * i:8 * (i + 1), :]
    run_ref[...] = run
    mx = logits[0]
    for i in range(1, n):
        mx = jnp.maximum(mx, logits[i])
    m_old = m_ref[...]
    m_new = jnp.maximum(m_old, jnp.max(mx, axis=-1, keepdims=True))
    alpha = jnp.exp(m_old - m_new)
    ps = [jnp.exp(logits[i] - m_new) for i in range(n)]
    psum = ps[0]
    for i in range(1, n):
        psum = psum + ps[i]
    l_ref[...] = alpha * l_ref[...] + jnp.sum(psum, axis=-1, keepdims=True)
    m_ref[...] = m_new
    for h in range(FOX_HEADS):
        hs = slice(h * FOX_DH, (h + 1) * FOX_DH)
        a = acc_ref[hs, :] * alpha[h:h + 1, :]
        for i in range(n):
            a = a + v_refs[i][h] * ps[i][h:h + 1, :]
        acc_ref[hs, :] = a

    @pl.when(j == nj - 1)
    def _():
        l_all = l_ref[...]
        l_cols = jnp.concatenate([jnp.broadcast_to(l_all[h:h + 1, :], (FOX_DH, LANES)) for h in range(FOX_HEADS)],
                                 axis=0)
        o_cols = jnp.broadcast_to(jnp.sum(acc_ref[...], axis=-1, keepdims=True), (BRANCH, LANES)) / l_cols
        o_ref[...] = o_cols.T[0:1, :]


def _fox_decode(page_table, q, kc, vc, fc, cache_kt, cache_vt, cache_lft, layer):
    b, n_pages = page_table.shape
    nj = n_pages // PAGES_PER_STEP
    tok = pl.BlockSpec((None, 1, BRANCH), lambda i, j, pt: (i, 0, 0))
    pidx = lambda i, j, pt, k: pt[i, (nj - 1 - j) * PAGES_PER_STEP + k]
    page = lambda k: pl.BlockSpec((None, None, FOX_HEADS, FOX_DH, PAGE),
                                  lambda i, j, pt, k=k: (pidx(i, j, pt, k), layer, 0, 0, 0))
    lpage = lambda k: pl.BlockSpec((None, None, FOX_HEADS, PAGE),
                                   lambda i, j, pt, k=k: (pidx(i, j, pt, k), layer, 0, 0))
    rng = range(PAGES_PER_STEP)
    in_specs = ([tok, tok, tok, pl.BlockSpec((None, FOX_HEADS, LANES), lambda i, j, pt: (i, 0, 0))]
                + [page(k) for k in rng] * 2 + [lpage(k) for k in rng])
    return pl.pallas_call(
        _fox_decode_kernel,
        grid_spec=pltpu.PrefetchScalarGridSpec(
            num_scalar_prefetch=1, grid=(b, nj), in_specs=in_specs, out_specs=tok,
            scratch_shapes=[pltpu.VMEM((BRANCH, LANES), F32), pltpu.VMEM((FOX_HEADS, LANES), F32),
                            pltpu.VMEM((FOX_HEADS, LANES), F32), pltpu.VMEM((BRANCH, LANES), F32),
                            pltpu.VMEM((FOX_HEADS, LANES), F32)]),
        out_shape=jax.ShapeDtypeStruct((b, 1, BRANCH), F32),
        compiler_params=_cparams(2), name="fox_decode")(
            page_table, q, kc, vc, fc, *([cache_kt] * PAGES_PER_STEP), *([cache_vt] * PAGES_PER_STEP),
            *([cache_lft] * PAGES_PER_STEP))


def _merge_kernel(oa, ob, oc, od, g0, g1, g2, g3, x_ref, wb_ref, wo_ref, out_ref):
    merged = None
    for i, (o_ref, g_ref) in enumerate(((oa, g0), (ob, g1), (oc, g2), (od, g3))):
        term = jax.nn.sigmoid(g_ref[...].astype(F32)) * _dot(o_ref[...].astype(BF16), wb_ref[i])
        merged = term if merged is None else merged + term
    out_ref[...] = x_ref[...] + _dot(merged.astype(BF16), wo_ref[...])


def _merge(o_list, z2, x2, wb, wo, tm):
    m = x2.shape[0]
    o_spec = pl.BlockSpec((tm, BRANCH), lambda i: (i, 0))
    gate = lambda g: pl.BlockSpec((tm, D_MODEL), lambda i, g=g: (i, C_GATE // D_MODEL + g))
    xs = pl.BlockSpec((tm, D_MODEL), lambda i: (i, 0))
    return pl.pallas_call(
        _merge_kernel, grid=(m // tm,),
        in_specs=[o_spec] * 4 + [gate(g) for g in range(4)] + [
            xs, pl.BlockSpec((4, BRANCH, D_MODEL), lambda i: (0, 0, 0)),
            pl.BlockSpec((D_MODEL, D_MODEL), lambda i: (0, 0))],
        out_specs=xs, out_shape=jax.ShapeDtypeStruct((m, D_MODEL), F32),
        compiler_params=_cparams(1), name="merge")(*o_list, z2, z2, z2, z2, x2, wb, wo)


def _cross_kernel(x_ref, g_ref, wq_ref, mk_ref, mv_ref, wo_ref, out_ref):
    x = x_ref[...]
    q = _dot(_rms(x, g_ref[...]).astype(BF16), wq_ref[...])
    mk = mk_ref[...].astype(BF16)
    mv = mv_ref[...].astype(BF16)
    outs = []
    for h in range(X_HEADS):
        hs = slice(h * X_DH, (h + 1) * X_DH)
        s = _dot_nt(q[:, hs].astype(BF16), mk[:, hs]) * (X_DH ** -0.5)
        e = jnp.exp(s - jnp.max(s, axis=-1, keepdims=True))
        pr = e / jnp.sum(e, axis=-1, keepdims=True)
        outs.append(_dot(pr.astype(BF16), mv[:, hs]))
    out_ref[...] = x + _dot(jnp.concatenate(outs, axis=1).astype(BF16), wo_ref[...])


def _cross(x3, g, wq, mk_arr, mk_spec, mv_arr, mv_spec, wo, tt):
    b, t, _ = x3.shape
    xw = X_HEADS * X_DH
    xs = pl.BlockSpec((None, tt, D_MODEL), lambda i, j: (i, j, 0))
    return pl.pallas_call(
        _cross_kernel, grid=(b, t // tt),
        in_specs=[xs, pl.BlockSpec((1, D_MODEL), lambda i, j: (0, 0)),
                  pl.BlockSpec((D_MODEL, xw), lambda i, j: (0, 0)), mk_spec, mv_spec,
                  pl.BlockSpec((xw, D_MODEL), lambda i, j: (0, 0))],
        out_specs=xs, out_shape=jax.ShapeDtypeStruct((b, t, D_MODEL), F32),
        compiler_params=_cparams(2), name="cross_attn")(x3, g.reshape(1, D_MODEL), wq, mk_arr, mv_arr, wo)


def _mem_attn_step_kernel(q_ref, km_ref, vm_ref, o_ref):
    q = q_ref[...]
    row8 = lax.broadcasted_iota(jnp.int32, (8, X_DH), 0)
    q8 = jnp.zeros((8, X_DH), F32)
    for h in range(X_HEADS):
        q8 = jnp.where(row8 == h, jnp.broadcast_to(q[:, h * X_DH:(h + 1) * X_DH], (8, X_DH)), q8)
    n = MEM_LEN * X_HEADS
    s = _dot_nt(q8.astype(BF16), km_ref[...].astype(BF16)) * (X_DH ** -0.5)
    col = lax.broadcasted_iota(jnp.int32, (8, n), 1)
    row = lax.broadcasted_iota(jnp.int32, (8, n), 0)
    s = jnp.where((col & (X_HEADS - 1)) == row, s, NEG_INF)
    e = jnp.exp(s - jnp.max(s, axis=-1, keepdims=True))
    pr = e / jnp.sum(e, axis=-1, keepdims=True)
    o8 = _dot(pr.astype(BF16), vm_ref[...].astype(BF16))
    o_ref[...] = jnp.concatenate([o8[h:h + 1, :] for h in range(X_HEADS)], axis=1)


def _mem_attn_step(q3, km, vm, layer):
    b = q3.shape[0]
    xw = X_HEADS * X_DH
    tok = pl.BlockSpec((None, 1, xw), lambda i: (i, 0, 0))
    mem = pl.BlockSpec((None, None, MEM_LEN * X_HEADS, X_DH), lambda i: (i, layer, 0, 0))
    return pl.pallas_call(
        _mem_attn_step_kernel, grid=(b,), in_specs=[tok, mem, mem], out_specs=tok,
        out_shape=jax.ShapeDtypeStruct((b, 1, xw), F32),
        compiler_params=_cparams(1), name="mem_attn_step")(q3, km, vm)


def _proj_res_kernel(a_ref, w_ref, x_ref, o_ref):
    o_ref[...] = x_ref[...] + _dot(a_ref[...].astype(BF16), w_ref[...])


def _proj_res(a, w, x):
    m, k = a.shape
    n = w.shape[1]
    full = lambda r, c: pl.BlockSpec((r, c), lambda i: (0, 0))
    return pl.pallas_call(
        _proj_res_kernel, grid=(1,), in_specs=[full(m, k), full(k, n), full(m, n)], out_specs=full(m, n),
        out_shape=jax.ShapeDtypeStruct((m, n), F32), compiler_params=_cparams(1), name="proj_res")(a, w, x)


def _ffn_kernel(x_ref, g_ref, wg_ref, wu_ref, cw_ref, wd_ref, hist_ref, fg_ref, out_ref, st_ref,
                xn_ref, acc_ref, halo_ref, ext_ref, *, tt, tf, nf, final_norm):
    t, f = pl.program_id(1), pl.program_id(2)
    hp = _halo_rows(FFN_W)

    @pl.when(f == 0)
    def _():
        xn_ref[...] = _rms(x_ref[...], g_ref[...]).astype(BF16)

    @pl.when(t == 0)
    def _():
        ext_ref[0:hp, :] = jnp.zeros((hp, ext_ref.shape[1]), F32)
        ext_ref[hp - (FFN_W - 1):hp, :] = hist_ref[...]

    @pl.when(t > 0)
    def _():
        ext_ref[0:hp, :] = halo_ref[f]

    xn = xn_ref[...]
    blocks = [(c0, min(512, tf - c0)) for c0 in range(0, tf, 512)]
    n = len(blocks)
    us, acts, down = {}, {}, None
    for s in range(n + 2):
        if s < n:
            c0, w = blocks[s]
            ext_ref[hp:hp + tt, c0:c0 + w] = _dot(xn, wg_ref[:, c0:c0 + w])
            us[s] = _dot(xn, wu_ref[:, c0:c0 + w])
        if 1 <= s <= n:
            c0, w = blocks[s - 1]
            gc = _conv_rows(ext_ref, cw_ref, 0, tt, FFN_W, c0=c0, ncols=w)
            acts[s - 1] = (_silu(gc) * us.pop(s - 1)).astype(BF16)
        if s >= 2:
            c0, w = blocks[s - 2]
            part = _dot(acts.pop(s - 2), wd_ref[c0:c0 + w, :])
            down = part if down is None else down + part
    halo_ref[f] = ext_ref[tt:tt + hp, :]

    @pl.when(f == 0)
    def _():
        acc_ref[...] = down

    @pl.when((f > 0) & (f < nf - 1))
    def _():
        acc_ref[...] += down

    @pl.when(f == nf - 1)
    def _():
        y = x_ref[...] + (acc_ref[...] + down if nf > 1 else down)
        out_ref[...] = _rms(y, fg_ref[...]) if final_norm else y

    @pl.when((f == nf - 1) & (t == pl.num_programs(1) - 1))
    def _():
        for ff in range(nf):
            st_ref[:, ff * tf:(ff + 1) * tf] = halo_ref[ff][hp - (FFN_W - 1):hp, :]


def _ffn(x3, g, w_in, cw, w_out, hist, final_g, final_norm, tt, tf):
    b, t, _ = x3.shape
    nf = D_FF // tf
    xs = pl.BlockSpec((None, tt, D_MODEL), lambda i, j, f: (i, j, 0))
    st = pl.BlockSpec((None, FFN_W - 1, tf), lambda i, j, f: (i, 0, f))
    st_out = pl.BlockSpec((None, FFN_W - 1, D_FF), lambda i, j, f: (i, 0, 0))
    return pl.pallas_call(
        functools.partial(_ffn_kernel, tt=tt, tf=tf, nf=nf, final_norm=final_norm), grid=(b, t // tt, nf),
        in_specs=[xs, pl.BlockSpec((1, D_MODEL), lambda i, j, f: (0, 0)),
                  pl.BlockSpec((D_MODEL, tf), lambda i, j, f: (0, f)),
                  pl.BlockSpec((D_MODEL, tf), lambda i, j, f: (0, nf + f)),
                  pl.BlockSpec((FFN_W, tf), lambda i, j, f: (0, f)),
                  pl.BlockSpec((tf, D_MODEL), lambda i, j, f: (f, 0)), st,
                  pl.BlockSpec((1, D_MODEL), lambda i, j, f: (0, 0))],
        out_specs=[xs, st_out],
        out_shape=[jax.ShapeDtypeStruct((b, t, D_MODEL), F32), jax.ShapeDtypeStruct((b, FFN_W - 1, D_FF), F32)],
        scratch_shapes=[pltpu.VMEM((tt, D_MODEL), BF16), pltpu.VMEM((tt, D_MODEL), F32),
                        pltpu.VMEM((nf, _halo_rows(FFN_W), tf), F32),
                        pltpu.VMEM((_halo_rows(FFN_W) + tt, tf), F32)],
        compiler_params=_cparams(3), name="conv_ffn")(
            x3, g.reshape(1, D_MODEL), w_in, w_in, cw, w_out, hist, final_g.reshape(1, D_MODEL))


def _ffn_step_kernel(x_ref, g_ref, wg_ref, wu_ref, cw_ref, wd_ref, h0_ref, h1_ref, out_ref, s0_ref, s1_ref,
                     xn_ref, acc_ref):
    f = pl.program_id(0)

    @pl.when(f == 0)
    def _():
        xn_ref[...] = _rms(x_ref[...], g_ref[...]).astype(BF16)
        acc_ref[...] = jnp.zeros(acc_ref.shape, F32)

    gcol = _dot(xn_ref[...], wg_ref[...])
    u = _dot(xn_ref[...], wu_ref[...])
    gc = cw_ref[0:1, :] * h0_ref[...] + cw_ref[1:2, :] * h1_ref[...] + cw_ref[2:3, :] * gcol
    s0_ref[...] = h1_ref[...]
    s1_ref[...] = gcol
    acc_ref[...] += _dot((_silu(gc) * u).astype(BF16), wd_ref[...])

    @pl.when(f == pl.num_programs(0) - 1)
    def _():
        out_ref[...] = x_ref[...] + acc_ref[...]


def _ffn_step(x2, g, w_in, cw, w_out, h0, h1, tf):
    b = x2.shape[0]
    nf = D_FF // tf
    xs = pl.BlockSpec((b, D_MODEL), lambda f: (0, 0))
    hs = pl.BlockSpec((b, tf), lambda f: (0, f))
    return pl.pallas_call(
        _ffn_step_kernel, grid=(nf,),
        in_specs=[xs, pl.BlockSpec((1, D_MODEL), lambda f: (0, 0)),
                  pl.BlockSpec((D_MODEL, tf), lambda f: (0, f)),
                  pl.BlockSpec((D_MODEL, tf), lambda f: (0, nf + f)),
                  pl.BlockSpec((FFN_W, tf), lambda f: (0, f)),
                  pl.BlockSpec((tf, D_MODEL), lambda f: (f, 0)), hs, hs],
        out_specs=[xs, hs, hs],
        out_shape=[jax.ShapeDtypeStruct((b, D_MODEL), F32), jax.ShapeDtypeStruct((b, D_FF), F32),
                   jax.ShapeDtypeStruct((b, D_FF), F32)],
        scratch_shapes=[pltpu.VMEM((b, D_MODEL), BF16), pltpu.VMEM((b, D_MODEL), F32)],
        compiler_params=_cparams(1), name="conv_ffn_step")(x2, g.reshape(1, D_MODEL), w_in, w_in, cw, w_out, h0, h1)


def _kv_out_kernel(k_ref, v_ref, *refs):
    kt_ref, vt_ref = refs[-2:]
    kt_ref[...] = k_ref[...].astype(F32).T
    vt_ref[...] = v_ref[...].astype(F32).T


def _kv_out(z3, layer, depth, prev, tt):
    b, t, _ = z3.shape
    col = lambda c: pl.BlockSpec((None, tt, BRANCH), lambda i, j, c=c: (i, j, c // BRANCH))
    out = pl.BlockSpec((None, None, BRANCH, tt), lambda i, j: (i, layer, 0, j))
    shape = jax.ShapeDtypeStruct((b, depth, BRANCH, t), F32)
    in_specs, args, aliases = [col(C_FK), col(C_FV)], [z3, z3], {}
    if prev is not None:
        in_specs += [pl.BlockSpec(memory_space=pl.ANY)] * 2
        args += list(prev)
        aliases = {2: 0, 3: 1}
    return pl.pallas_call(
        _kv_out_kernel, grid=(b, t // tt), in_specs=in_specs, out_specs=[out, out], out_shape=[shape, shape],
        input_output_aliases=aliases, compiler_params=_cparams(2), name="kv_out")(*args)


def _final_norm_kernel(x_ref, g_ref, o_ref):
    o_ref[...] = _rms(x_ref[...], g_ref[...])


def _final_norm(x2, g, tm):
    m = x2.shape[0]
    xs = pl.BlockSpec((tm, D_MODEL), lambda i: (i, 0))
    return pl.pallas_call(
        _final_norm_kernel, grid=(m // tm,),
        in_specs=[xs, pl.BlockSpec((1, D_MODEL), lambda i: (0, 0))], out_specs=xs,
        out_shape=jax.ShapeDtypeStruct((m, D_MODEL), F32),
        compiler_params=_cparams(1), name="final_norm")(x2, g.reshape(1, D_MODEL))


def _prep_layer(l, w_in, w_branch, w_out, w_cq, w_co, w_ffn_in, w_ffn_out, gdn_a_log, gdn_dt_bias, fox_fbias):
    w = jnp.transpose(w_in[l])
    w_main = jnp.concatenate([w[:3584], w[3592:6152], w[6160:]], axis=0).astype(BF16)
    w_small = jnp.concatenate([w[3584:3592], w[6152:6160],
                               jnp.zeros((SMALL - 16, D_MODEL), F32)], axis=0).astype(BF16)
    zeros4 = jnp.zeros((4,), F32)
    bias_row = jnp.concatenate([gdn_dt_bias[l], zeros4, fox_fbias[l], jnp.zeros((SMALL - 16,), F32)]).reshape(1, SMALL)
    alog_row = jnp.concatenate([gdn_a_log[l], jnp.zeros((SMALL - 4,), F32)]).reshape(1, SMALL)
    return dict(w_main=w_main, w_small=w_small, bias_row=bias_row, alog_row=alog_row,
                wb=w_branch[l].astype(BF16), wo=w_out[l].astype(BF16), wcq=w_cq[l].astype(BF16),
                wco=w_co[l].astype(BF16), wfi=w_ffn_in[l].astype(BF16), wfo=w_ffn_out[l].astype(BF16))


def kernel(x_prompt, x_sample, cache_fox_k, cache_fox_v, cache_fox_logf, cache_mem_k, cache_mem_v, state_sconv, state_gdn_conv, state_gdn, state_conf_conv, state_ffn_conv, page_table, mem_prompt, norm_mix, w_in, w_branch, w_out, sconv_w, gdn_conv_w, gdn_a_log, gdn_dt_bias, gdn_norm, conf_dw_w, conf_dw_b, conf_ln_g, conf_ln_b, fox_fbias, norm_cross, norm_mem, w_cq, w_ckv, w_co, norm_ffn, w_ffn_in, ffn_conv_w, w_ffn_out, norm_final):
    depth = w_in.shape[0]
    bp, t, _ = x_prompt.shape
    bs = x_sample.shape[0]
    xw = X_HEADS * X_DH
    prep = [_prep_layer(l, w_in, w_branch, w_out, w_cq, w_co, w_ffn_in, w_ffn_out, gdn_a_log, gdn_dt_bias, fox_fbias)
            for l in range(depth)]

    tt = min(t, 512)
    mp = bp * t
    x = x_prompt.reshape(mp, D_MODEL)
    mem2 = mem_prompt.reshape(bp * MEM_LEN, D_MODEL)
    p_out = {k: [] for k in ("fl", "mk", "mv", "sc", "gc", "gs", "cc", "ff")}
    zero = lambda *s: jnp.zeros(s, F32)
    kt_vt = None
    for l in range(depth):
        w = prep[l]
        z, zs = _rms_matmul(x, norm_mix[l], w["w_main"], w["w_small"], tm=min(mp, 2048), tn=1024, out_dtype=BF16,
                            w_rows_are_outputs=True)
        z3 = z.reshape(bp, t, NZ)
        o_a, st_sc = _sconv(z3, zero(bp, SC_W - 1, BRANCH), sconv_w[l], tt)
        q, k, v, st_gc = _gdn_pre(z3, zero(bp, GDN_W - 1, 3 * BRANCH), gdn_conv_w[l], tt)
        p, pt, qa, ka = _gates(zs.reshape(bp, t, SMALL), w["bias_row"], w["alog_row"], tt)
        o_b, st_gs = _gdn_chunked(q, k, v, p, pt, z3, gdn_norm[l], zero(bp, GDN_HEADS, GDN_DK, GDN_DK), tt)
        o_c, st_cc = _conformer(z3, zero(bp, CONF_W - 1, BRANCH), conf_dw_w[l], conf_dw_b[l],
                                conf_ln_g[l], conf_ln_b[l], tt)
        o_d = _fox_prompt(z3, qa, ka, tt)
        o_list = [o.reshape(mp, BRANCH) for o in (o_a, o_b, o_c, o_d)]
        x = _merge(o_list, z, x, w["wb"], w["wo"], tm=min(mp, 512))
        kv = _rms_matmul(mem2, norm_mem[l], w_ckv[l].astype(BF16), None, tm=min(bp * MEM_LEN, 1024), tn=512)
        kv3 = kv.reshape(bp, MEM_LEN, 2 * xw)
        x = _cross(x.reshape(bp, t, D_MODEL), norm_cross[l], w["wcq"],
                   kv3, pl.BlockSpec((None, MEM_LEN, xw), lambda i, j: (i, 0, 0)),
                   kv3, pl.BlockSpec((None, MEM_LEN, xw), lambda i, j: (i, 0, 1)), w["wco"], tt)
        x, st_ff = _ffn(x, norm_ffn[l], w["wfi"], ffn_conv_w[l], w["wfo"], zero(bp, FFN_W - 1, D_FF),
                        norm_final, l == depth - 1, tt=min(t, 512), tf=D_FF // 2)
        x = x.reshape(mp, D_MODEL)
        kt_vt = _kv_out(z3, l, depth, kt_vt, tt)
        p_out["fl"].append(pt[:, 8:16, :])
        p_out["mk"].append(kv3[:, :, :xw])
        p_out["mv"].append(kv3[:, :, xw:])
        for key, val in (("sc", st_sc), ("gc", st_gc), ("gs", st_gs), ("cc", st_cc), ("ff", st_ff)):
            p_out[key].append(val)
    y_prompt = x.reshape(bp, t, D_MODEL)

    ckt = jnp.transpose(cache_fox_k, (0, 2, 3, 4, 1))
    cvt = jnp.transpose(cache_fox_v, (0, 2, 3, 4, 1))
    clt = jnp.transpose(cache_fox_logf, (0, 2, 3, 1))
    cmk = cache_mem_k.reshape(bs, depth, MEM_LEN * X_HEADS, X_DH)
    cmv = cache_mem_v.reshape(bs, depth, MEM_LEN * X_HEADS, X_DH)
    xs_ = x_sample.reshape(bs, D_MODEL)
    s_out = {k: [] for k in ("fk", "fv", "fl", "sc", "gc", "gs", "cc", "ff")}
    for l in range(depth):
        w = prep[l]
        z, zs = _rms_matmul(xs_, norm_mix[l], w["w_main"], w["w_small"], tm=bs, tn=2048, w_rows_are_outputs=True)
        z3 = z.reshape(bs, 1, NZ)
        o_a, st_sc = _sconv(z3, state_sconv, sconv_w[l], 1, layer=l)
        q, k, v, st_gc = _gdn_pre(z3, state_gdn_conv, gdn_conv_w[l], 1, layer=l)
        p = _gates(zs.reshape(bs, 1, SMALL), w["bias_row"], w["alog_row"], 1)[0]
        o_b, st_gs = _gdn_step(q, k, v, p, z3, gdn_norm[l], state_gdn, l)
        o_c, st_cc = _conformer(z3, state_conf_conv, conf_dw_w[l], conf_dw_b[l], conf_ln_g[l], conf_ln_b[l], 1,
                                layer=l)
        fq, fk, fv = (z3[:, :, c:c + BRANCH] for c in (C_FQ, C_FK, C_FV))
        flog = p[:, :, 8:16]
        fc = jnp.broadcast_to(jnp.swapaxes(flog, 1, 2), (bs, FOX_HEADS, LANES))
        o_d = _fox_decode(page_table, fq, fk, fv, fc, ckt, cvt, clt, l)
        o_list = [o.reshape(bs, BRANCH) for o in (o_a, o_b, o_c, o_d)]
        xs_ = _merge(o_list, z, xs_, w["wb"], w["wo"], tm=bs)
        q3 = _rms_matmul(xs_, norm_cross[l], w["wcq"], None, tm=bs, tn=xw).reshape(bs, 1, xw)
        o3 = _mem_attn_step(q3, cmk, cmv, l)
        xs_ = _proj_res(o3.reshape(bs, xw), w["wco"], xs_)
        xs_, h_a, h_b = _ffn_step(xs_, norm_ffn[l], w["wfi"], ffn_conv_w[l], w["wfo"],
                                  state_ffn_conv[:, l, 0], state_ffn_conv[:, l, 1], tf=256)
        s_out["fk"].append(fk)
        s_out["fv"].append(fv)
        s_out["fl"].append(flog)
        for key, val in (("sc", st_sc), ("gc", st_gc), ("gs", st_gs), ("cc", st_cc),
                         ("ff", jnp.stack([h_a, h_b], axis=1))):
            s_out[key].append(val)
    y_sample = _final_norm(xs_, norm_final, tm=bs).reshape(bs, 1, D_MODEL)

    heads = lambda a: a.reshape(a.shape[0], a.shape[1], depth, FOX_HEADS, FOX_DH)
    memh = lambda a: a.reshape(bp, depth, MEM_LEN, X_HEADS, X_DH)
    from_pos_minor = lambda a: jnp.transpose(a.reshape(bp, depth, FOX_HEADS, FOX_DH, t), (0, 4, 1, 2, 3))
    return (y_prompt, y_sample,
            from_pos_minor(kt_vt[0]), from_pos_minor(kt_vt[1]),
            jnp.transpose(jnp.stack(p_out["fl"], axis=1), (0, 3, 1, 2)),
            memh(jnp.stack(p_out["mk"], axis=1)), memh(jnp.stack(p_out["mv"], axis=1)),
            jnp.stack(p_out["sc"], axis=1), jnp.stack(p_out["gc"], axis=1), jnp.stack(p_out["gs"], axis=1),
            jnp.stack(p_out["cc"], axis=1), jnp.stack(p_out["ff"], axis=1),
            heads(jnp.stack(s_out["fk"], axis=2)), heads(jnp.stack(s_out["fv"], axis=2)),
            jnp.stack(s_out["fl"], axis=2),
            jnp.stack(s_out["sc"], axis=1), jnp.stack(s_out["gc"], axis=1), jnp.stack(s_out["gs"], axis=1),
            jnp.stack(s_out["cc"], axis=1), jnp.stack(s_out["ff"], axis=1))
```
